```python
import math
import jax, jax.numpy as jnp
from jax import lax
import numpy as np

D_MODEL = 2048
BATCH = 1
SEQ = 8192
DEPTH = 2
DEC_BATCH = 2
DEC_SEQ = 4096
PAST_LEN = 128

F32 = jnp.float32
EPS = 1e-6
CHUNK = 128

SSD_HEADS = 32
SSD_HEAD_DIM = 64
SSD_WIDTH = SSD_HEADS * SSD_HEAD_DIM
SSD_GROUPS = 4
SSD_STATE = 128
SSD_CONV = 5
SSD_XBC = SSD_WIDTH + 2 * SSD_GROUPS * SSD_STATE

DA_HEADS = 8
DA_HEAD_DIM = 128
DA_QK_WIDTH = 2 * DA_HEADS * DA_HEAD_DIM
DA_V_WIDTH = DA_HEADS * 2 * DA_HEAD_DIM
Q_BLOCK = 128
ROPE_THETA = 10000.0

AB_IN = SSD_WIDTH + SSD_XBC + 2 * SSD_HEADS + 2 * DA_QK_WIDTH + DA_V_WIDTH
AB_OUT = SSD_WIDTH + DA_V_WIDTH

RET_HEADS = 8
RET_QK_DIM = 256
RET_V_DIM = 512
RET_QK_WIDTH = RET_HEADS * RET_QK_DIM
RET_V_WIDTH = RET_HEADS * RET_V_DIM
RET_IN = 2 * RET_QK_WIDTH + 2 * RET_V_WIDTH

MOE_GROUPS = 4
MOE_EXPERTS_PER_GROUP = 8
MOE_EXPERTS = MOE_GROUPS * MOE_EXPERTS_PER_GROUP
MOE_TOP_K = 2
MOE_FF = 1024
MOE_BLOCK = 128

N_EVEN = (DEPTH + 1) // 2
N_ODD = DEPTH // 2

kernel_name = 'hybrid_ssd_diffattn_retention_hmoe_encoder'


def _rms(x, g=None, eps=EPS):
    xf = x.astype(F32)
    y = xf * lax.rsqrt(jnp.mean(xf * xf, axis=-1, keepdims=True) + eps)
    if g is not None:
        y = y * g.astype(F32)
    return y.astype(x.dtype)


def _rotary(x):
    L, d = x.shape[1], x.shape[-1]
    half = d // 2
    inv = 1.0 / (ROPE_THETA ** (jnp.arange(half, dtype=F32) / half))
    ang = jnp.arange(L, dtype=F32)[:, None] * inv[None, :]
    cos = jnp.cos(ang)[None, :, None, :]
    sin = jnp.sin(ang)[None, :, None, :]
    xf = x.astype(F32)
    x1, x2 = xf[..., :half], xf[..., half:]
    return jnp.concatenate([x1 * cos - x2 * sin, x1 * sin + x2 * cos], axis=-1).astype(x.dtype)


def _flip(a):
    return jnp.flip(a, axis=1)


def _dwconv_centred(x, w, bias):
    pad = SSD_CONV // 2
    y = lax.conv_general_dilated(x, w[:, None, :].astype(x.dtype), window_strides=(1,),
                                 padding=[(pad, pad)], dimension_numbers=('NWC', 'WIO', 'NWC'),
                                 feature_group_count=x.shape[-1])
    return y + bias.astype(x.dtype)


def _ssd_chunked(x, dt, a, bm, cm):
    b, L, H, P = x.shape
    G, N = bm.shape[2], bm.shape[3]
    R = H // G
    nc = L // CHUNK
    xd = (x * dt[..., None]).reshape(b, nc, CHUNK, G, R, P)
    cs = jnp.cumsum((dt * a).reshape(b, nc, CHUNK, G, R), axis=2)
    bc = bm.reshape(b, nc, CHUNK, G, N)
    cc = cm.reshape(b, nc, CHUNK, G, N)
    lower = jnp.tril(jnp.ones((CHUNK, CHUNK), bool))[None, None, :, :, None, None]
    seg = cs[:, :, :, None] - cs[:, :, None, :]
    decay_in = jnp.exp(jnp.where(lower, seg, -jnp.inf))
    scores = jnp.einsum('bclgn,bcsgn->bclsg', cc, bc)
    y_diag = jnp.einsum('bclsg,bclsgr,bcsgrp->bclgrp', scores, decay_in, xd)
    decay_end = jnp.exp(cs[:, :, -1:] - cs)
    states = jnp.einsum('bcsgn,bcsgr,bcsgrp->bcgrpn', bc, decay_end, xd)
    chunk_decay = jnp.exp(cs[:, :, -1])

    def carry_step(h, inp):
        st, dec = inp
        return h * dec[..., None, None] + st, h

    h0 = jnp.zeros((b, G, R, P, N), states.dtype)
    _, h_in = lax.scan(carry_step, h0, (states.swapaxes(0, 1), chunk_decay.swapaxes(0, 1)))
    h_in = h_in.swapaxes(0, 1)
    y_off = jnp.einsum('bclgn,bcgrpn,bclgr->bclgrp', cc, h_in, jnp.exp(cs))
    return (y_diag + y_off).reshape(b, L, H, P)


def _ssd_diff_mixer(h, p, i, layer):
    b, L, _ = h.shape
    o1 = SSD_WIDTH
    o2 = o1 + SSD_XBC
    o3 = o2 + 2 * SSD_HEADS
    o4 = o3 + DA_QK_WIDTH
    o5 = o4 + DA_QK_WIDTH
    proj = h @ p['ab_w_in'][i]
    z, xbc, dt_raw, q, k, v = jnp.split(proj, [o1, o2, o3, o4, o5], axis=-1)

    xbc = jax.nn.silu(_dwconv_centred(xbc, p['ssd_conv_w'][i], p['ssd_conv_b'][i]))
    xs, bm, cm = jnp.split(xbc, [SSD_WIDTH, SSD_WIDTH + SSD_GROUPS * SSD_STATE], axis=-1)
    xs = xs.reshape(b, L, SSD_HEADS, SSD_HEAD_DIM)
    bm = bm.reshape(b, L, SSD_GROUPS, SSD_STATE)
    cm = cm.reshape(b, L, SSD_GROUPS, SSD_STATE)
    dt_f, dt_b = jnp.split(dt_raw.astype(F32), 2, axis=-1)
    dt_f = jax.nn.softplus(dt_f + p['ssd_dt_bias_fwd'][i].astype(F32))
    dt_b = jax.nn.softplus(dt_b + p['ssd_dt_bias_bwd'][i].astype(F32))
    a_f = -jnp.exp(p['ssd_a_log_fwd'][i].astype(F32))
    a_b = -jnp.exp(p['ssd_a_log_bwd'][i].astype(F32))
    y_f = _ssd_chunked(xs, dt_f, a_f, bm, cm)
    y_b = _flip(_ssd_chunked(_flip(xs), _flip(dt_b), a_b, _flip(bm), _flip(cm)))
    y = y_f + y_b + p['ssd_d'][i][:, None] * xs
    y_ssd = _rms(y.reshape(b, L, SSD_WIDTH) * jax.nn.silu(z), p['ssd_norm_g'][i])

    lam_init = 0.8 - 0.6 * math.exp(-0.3 * layer)
    lam = (jnp.exp(jnp.sum(p['da_lambda_q1'][i].astype(F32) * p['da_lambda_k1'][i].astype(F32)))
           - jnp.exp(jnp.sum(p['da_lambda_q2'][i].astype(F32) * p['da_lambda_k2'][i].astype(F32)))
           + lam_init)
    q = _rotary(q.reshape(b, L, 2 * DA_HEADS, DA_HEAD_DIM))
    k = _rotary(k.reshape(b, L, 2 * DA_HEADS, DA_HEAD_DIM))
    v = v.reshape(b, L, DA_HEADS, 2 * DA_HEAD_DIM)
    nb = L // Q_BLOCK
    qb = q.reshape(b, nb, Q_BLOCK, 2 * DA_HEADS, DA_HEAD_DIM).swapaxes(0, 1)
    scale = DA_HEAD_DIM ** -0.5

    def attend(qi):
        s = jnp.einsum('bqhd,bkhd->bhqk', qi, k).astype(F32) * scale
        pr = jax.nn.softmax(s, axis=-1).reshape(b, DA_HEADS, 2, Q_BLOCK, L)
        w = pr[:, :, 0] - lam * pr[:, :, 1]
        return jnp.einsum('bhqk,bkhe->bqhe', w.astype(v.dtype), v)

    o = lax.map(attend, qb).swapaxes(0, 1).reshape(b, L, DA_HEADS, 2 * DA_HEAD_DIM)
    o = (_rms(o, p['da_subln_g'][i], 1e-5) * (1.0 - lam_init)).reshape(b, L, DA_V_WIDTH)

    mixed = jnp.concatenate([y_ssd, o.astype(y_ssd.dtype)], axis=-1)
    return mixed @ p['ab_w_out'][i]


def _retention_dir(q, k, v, lg, strict):
    b, L, H, dk = q.shape
    dv = v.shape[-1]
    nc = L // CHUNK
    pos = jnp.arange(CHUNK, dtype=F32)
    diff = pos[:, None] - pos[None, :]
    mask = (diff > 0) if strict else (diff >= 0)
    inner_decay = jnp.where(mask[None], jnp.exp(jnp.where(mask, diff, 0.0)[None] * lg[:, None, None]), 0.0)
    cross_decay = jnp.exp((pos + 1.0)[:, None] * lg[None, :])
    kv_decay = jnp.exp((CHUNK - 1.0 - pos)[:, None] * lg[None, :])
    chunk_decay = jnp.exp(CHUNK * lg)

    def to_chunks(a):
        return a.reshape(b, nc, CHUNK, H, a.shape[-1]).swapaxes(0, 1)

    def step(state, inp):
        qi, ki, vi = inp
        scores = jnp.einsum('bihd,bjhd->bhij', qi, ki) * inner_decay
        inner = jnp.einsum('bhij,bjhe->bihe', scores, vi)
        cross = jnp.einsum('bihd,bhde->bihe', qi, state) * cross_decay[None, :, :, None]
        state = state * chunk_decay[None, :, None, None] + jnp.einsum('bjhd,jh,bjhe->bhde', ki, kv_decay, vi)
        return state, inner + cross

    s0 = jnp.zeros((b, H, dk, dv), F32)
    _, out = lax.scan(step, s0, (to_chunks(q), to_chunks(k), to_chunks(v)))
    return out.swapaxes(0, 1).reshape(b, L, H, dv)


def _retention_mixer(h, p, j):
    b, L, _ = h.shape
    proj = h @ p['ret_w_in'][j]
    q, k, v, g = jnp.split(proj, [RET_QK_WIDTH, 2 * RET_QK_WIDTH, 2 * RET_QK_WIDTH + RET_V_WIDTH], axis=-1)
    q = _rotary(q.reshape(b, L, RET_HEADS, RET_QK_DIM))
    k = _rotary(k.reshape(b, L, RET_HEADS, RET_QK_DIM)) * (RET_QK_DIM ** -0.5)
    v = v.reshape(b, L, RET_HEADS, RET_V_DIM)
    lg_f = -jnp.exp(p['ret_log_decay_fwd'][j].astype(F32))
    lg_b = -jnp.exp(p['ret_log_decay_bwd'][j].astype(F32))
    o = _retention_dir(q, k, v, lg_f, False) + _flip(_retention_dir(_flip(q), _flip(k), _flip(v), lg_b, True))
    o = _rms(o).reshape(b, L, RET_V_WIDTH) * jax.nn.silu(g)
    return o @ p['ret_w_out'][j]


def _hier_moe(h, p, layer):
    b, L, D = h.shape
    N = b * L
    xt = h.reshape(N, D)
    rows = jnp.arange(N)
    coarse = (xt @ p['moe_w_group'][layer] + p['moe_b_group'][layer]).astype(F32)
    grp = jnp.argmax(coarse, axis=-1)
    p_grp = jax.nn.softmax(coarse, axis=-1)[rows, grp]
    fine = (xt @ p['moe_w_expert'][layer] + p['moe_b_expert'][layer]).astype(F32)
    fine = fine.reshape(N, MOE_GROUPS, MOE_EXPERTS_PER_GROUP)[rows, grp]
    top_v, top_i = lax.top_k(fine, MOE_TOP_K)
    gates = jax.nn.softmax(top_v, axis=-1) * p_grp[:, None]
    n_assign = N * MOE_TOP_K
    e_flat = (grp[:, None] * MOE_EXPERTS_PER_GROUP + top_i).reshape(n_assign).astype(jnp.int32)
    tok_flat = jnp.repeat(rows, MOE_TOP_K).astype(jnp.int32)
    gate_flat = gates.reshape(n_assign)
    order = jnp.argsort(e_flat)
    e_s, tok_s, gate_s = e_flat[order], tok_flat[order], gate_flat[order]
    counts = jax.ops.segment_sum(jnp.ones((n_assign,), jnp.int32), e_flat, num_segments=MOE_EXPERTS)
    padded = (counts + MOE_BLOCK - 1) // MOE_BLOCK * MOE_BLOCK
    start_sorted = jnp.cumsum(counts) - counts
    ends_pad = jnp.cumsum(padded)
    start_pad = ends_pad - padded
    dest = start_pad[e_s] + jnp.arange(n_assign, dtype=jnp.int32) - start_sorted[e_s]
    n_rows = n_assign + MOE_EXPERTS * MOE_BLOCK
    n_blocks = n_rows // MOE_BLOCK
    row_tok = jnp.zeros((n_rows,), jnp.int32).at[dest].set(tok_s)
    row_gate = jnp.zeros((n_rows,), F32).at[dest].set(gate_s)
    block_expert = jnp.minimum(jnp.searchsorted(ends_pad, jnp.arange(n_blocks, dtype=jnp.int32) * MOE_BLOCK,
                                                side='right'), MOE_EXPERTS - 1)
    w1, w3, w2 = p['moe_w1'][layer], p['moe_w3'][layer], p['moe_w2'][layer]

    def expert_block(args):
        tok, gt, e = args
        xb = xt[tok]
        hid = jax.nn.silu(xb @ w1[e]) * (xb @ w3[e])
        return (hid @ w2[e]) * gt[:, None]

    out = lax.map(expert_block, (row_tok.reshape(n_blocks, MOE_BLOCK),
                                 row_gate.reshape(n_blocks, MOE_BLOCK), block_expert))
    y = jax.ops.segment_sum(out.reshape(n_rows, D), row_tok, num_segments=N)
    return y.reshape(b, L, D)


def _trunk(x, c, p):
    for layer in range(DEPTH):
        mod = jax.nn.silu(c) @ p['ada_w'][layer] + p['ada_b'][layer]
        sh1, sc1, g1, sh2, sc2, g2 = jnp.split(mod[:, None, :], 6, axis=-1)
        h = _rms(x, p['norm_mix_g'][layer]) * (1.0 + sc1) + sh1
        if layer % 2 == 0:
            mix = _ssd_diff_mixer(h, p, layer // 2, layer)
        else:
            mix = _retention_mixer(h, p, layer // 2)
        x = x + g1 * mix
        h = _rms(x, p['norm_ffn_g'][layer]) * (1.0 + sc2) + sh2
        x = x + g2 * _hier_moe(h, p, layer)
    return _rms(x, p['final_norm_g'])


def setup_inputs(seed: int = 0) -> dict:
    key = jax.random.key(seed)
    ks = iter(jax.random.split(key, 48))
    D = D_MODEL

    def nrm(shape, scale):
        return jax.random.normal(next(ks), shape, F32) * scale

    def gain(shape):
        return 1.0 + nrm(shape, 0.02)

    def dt_bias(shape):
        u = jax.random.uniform(next(ks), shape, F32)
        dt = jnp.exp(u * (math.log(0.1) - math.log(0.001)) + math.log(0.001))
        return dt + jnp.log(-jnp.expm1(-dt))

    gate_offset = jnp.zeros((6 * D,), F32).at[2 * D:3 * D].set(1.0).at[5 * D:].set(1.0)
    ret_base = jnp.log(-jnp.log(1.0 - 2.0 ** (-5.0 - jnp.arange(RET_HEADS, dtype=F32))))
    inp = {}
    inp['x_prompt'] = nrm((BATCH, SEQ, D), 1.0)
    inp['x_sample'] = nrm((DEC_BATCH, DEC_SEQ, D), 1.0)
    inp['c_prompt'] = nrm((BATCH, D), 1.0)
    inp['c_sample'] = nrm((DEC_BATCH, D), 1.0)
    inp['ada_w'] = nrm((DEPTH, D, 6 * D), 0.2 * D ** -0.5)
    inp['ada_b'] = gate_offset[None] + nrm((DEPTH, 6 * D), 0.02)
    inp['norm_mix_g'] = gain((DEPTH, D))
    inp['norm_ffn_g'] = gain((DEPTH, D))
    inp['ab_w_in'] = nrm((N_EVEN, D, AB_IN), D ** -0.5)
    inp['ssd_conv_w'] = nrm((N_EVEN, SSD_CONV, SSD_XBC), SSD_CONV ** -0.5)
    inp['ssd_conv_b'] = nrm((N_EVEN, SSD_XBC), 0.02)
    inp['ssd_a_log_fwd'] = jnp.log(jax.random.uniform(next(ks), (N_EVEN, SSD_HEADS), F32, 1.0, 16.0))
    inp['ssd_a_log_bwd'] = jnp.log(jax.random.uniform(next(ks), (N_EVEN, SSD_HEADS), F32, 1.0, 16.0))
    inp['ssd_dt_bias_fwd'] = dt_bias((N_EVEN, SSD_HEADS))
    inp['ssd_dt_bias_bwd'] = dt_bias((N_EVEN, SSD_HEADS))
    inp['ssd_d'] = 1.0 + nrm((N_EVEN, SSD_HEADS), 0.1)
    inp['ssd_norm_g'] = gain((N_EVEN, SSD_WIDTH))
    inp['da_lambda_q1'] = nrm((N_EVEN, DA_HEAD_DIM), 0.1)
    inp['da_lambda_k1'] = nrm((N_EVEN, DA_HEAD_DIM), 0.1)
    inp['da_lambda_q2'] = nrm((N_EVEN, DA_HEAD_DIM), 0.1)
    inp['da_lambda_k2'] = nrm((N_EVEN, DA_HEAD_DIM), 0.1)
    inp['da_subln_g'] = gain((N_EVEN, 2 * DA_HEAD_DIM))
    inp['ab_w_out'] = nrm((N_EVEN, AB_OUT, D), AB_OUT ** -0.5)
    inp['ret_w_in'] = nrm((N_ODD, D, RET_IN), D ** -0.5)
    inp['ret_log_decay_fwd'] = ret_base[None] + nrm((N_ODD, RET_HEADS), 0.05)
    inp['ret_log_decay_bwd'] = ret_base[None] + nrm((N_ODD, RET_HEADS), 0.05)
    inp['ret_w_out'] = nrm((N_ODD, RET_V_WIDTH, D), RET_V_WIDTH ** -0.5)
    inp['moe_w_group'] = nrm((DEPTH, D, MOE_GROUPS), D ** -0.5)
    inp['moe_b_group'] = nrm((DEPTH, MOE_GROUPS), 0.01)
    inp['moe_w_expert'] = nrm((DEPTH, D, MOE_EXPERTS), D ** -0.5)
    inp['moe_b_expert'] = nrm((DEPTH, MOE_EXPERTS), 0.01)
    inp['moe_w1'] = nrm((DEPTH, MOE_EXPERTS, D, MOE_FF), D ** -0.5)
    inp['moe_w3'] = nrm((DEPTH, MOE_EXPERTS, D, MOE_FF), D ** -0.5)
    inp['moe_w2'] = nrm((DEPTH, MOE_EXPERTS, MOE_FF, D), MOE_FF ** -0.5)
    inp['final_norm_g'] = gain((D,))
    return inp


def reference(x_prompt, x_sample, c_prompt, c_sample, ada_w, ada_b, norm_mix_g, norm_ffn_g,
              ab_w_in, ssd_conv_w, ssd_conv_b, ssd_a_log_fwd, ssd_a_log_bwd, ssd_dt_bias_fwd,
              ssd_dt_bias_bwd, ssd_d, ssd_norm_g, da_lambda_q1, da_lambda_k1, da_lambda_q2,
              da_lambda_k2, da_subln_g, ab_w_out, ret_w_in, ret_log_decay_fwd, ret_log_decay_bwd,
              ret_w_out, moe_w_group, moe_b_group, moe_w_expert, moe_b_expert, moe_w1, moe_w3,
              moe_w2, final_norm_g):
    p = dict(ada_w=ada_w, ada_b=ada_b, norm_mix_g=norm_mix_g, norm_ffn_g=norm_ffn_g,
             ab_w_in=ab_w_in, ssd_conv_w=ssd_conv_w, ssd_conv_b=ssd_conv_b,
             ssd_a_log_fwd=ssd_a_log_fwd, ssd_a_log_bwd=ssd_a_log_bwd,
             ssd_dt_bias_fwd=ssd_dt_bias_fwd, ssd_dt_bias_bwd=ssd_dt_bias_bwd, ssd_d=ssd_d,
             ssd_norm_g=ssd_norm_g, da_lambda_q1=da_lambda_q1, da_lambda_k1=da_lambda_k1,
             da_lambda_q2=da_lambda_q2, da_lambda_k2=da_lambda_k2, da_subln_g=da_subln_g,
             ab_w_out=ab_w_out, ret_w_in=ret_w_in, ret_log_decay_fwd=ret_log_decay_fwd,
             ret_log_decay_bwd=ret_log_decay_bwd, ret_w_out=ret_w_out, moe_w_group=moe_w_group,
             moe_b_group=moe_b_group, moe_w_expert=moe_w_expert, moe_b_expert=moe_b_expert,
             moe_w1=moe_w1, moe_w3=moe_w3, moe_w2=moe_w2, final_norm_g=final_norm_g)
    y_prompt = _trunk(x_prompt, c_prompt, p)
    y_sample = _trunk(x_sample, c_sample, p)
    return (y_prompt, y_sample)
```

```python
import functools
import math

import jax
import jax.numpy as jnp
from jax import lax
from jax.experimental import pallas as pl
from jax.experimental.pallas import tpu as pltpu

F32 = jnp.float32
BF16 = jnp.bfloat16
HIGHEST = lax.Precision.HIGHEST

EPS = 1e-6
SUBLN_EPS = 1e-5
ROPE_THETA = 10000.0

SSD_HEADS = 32
SSD_HEAD_DIM = 64
SSD_WIDTH = SSD_HEADS * SSD_HEAD_DIM
SSD_GROUPS = 4
SSD_STATE = 128
SSD_CONV = 5
SSD_BC = SSD_GROUPS * SSD_STATE
SSD_XBC = SSD_WIDTH + 2 * SSD_BC
SSD_CHUNK = 128
DA_HEADS = 8
DA_HEAD_DIM = 128
DA_QK_WIDTH = 2 * DA_HEADS * DA_HEAD_DIM
DA_V_WIDTH = DA_HEADS * 2 * DA_HEAD_DIM
RET_HEADS = 8
RET_QK_DIM = 256
RET_V_DIM = 512
RET_QK_WIDTH = RET_HEADS * RET_QK_DIM
RET_V_WIDTH = RET_HEADS * RET_V_DIM
RET_CHUNK = 256
MOE_GROUPS = 4
MOE_EXPERTS_PER_GROUP = 8
MOE_EXPERTS = MOE_GROUPS * MOE_EXPERTS_PER_GROUP
MOE_TOP_K = 2
MOE_ROW_BLOCK = 512
ROUTER_PAD = 128
DT_PAD = 128
HALO = 16

VMEM_LIMIT_BYTES = 56 * 1024 * 1024


def _params(*semantics):
    return pltpu.CompilerParams(dimension_semantics=semantics, vmem_limit_bytes=VMEM_LIMIT_BYTES)


def _silu(x):
    return x * jax.nn.sigmoid(x)


def _softplus(x):
    return jnp.maximum(x, 0.0) + jnp.log1p(jnp.exp(-jnp.abs(x)))


def _tile(n, pref):
    t = min(n, pref)
    assert n % t == 0, (n, pref)
    return t


def _row_tile(t, pref, seq_starts):
    tm = min(t, pref)
    while t % tm or any(st % tm for st in seq_starts):
        tm //= 2
    return tm


def _seq_index(row, seq_starts):
    s = 0
    for st in seq_starts[1:]:
        s = s + jnp.where(row >= st, 1, 0)
    return s


def _is_any(row, values):
    hit = row == values[0]
    for v in values[1:]:
        hit = jnp.logical_or(hit, row == v)
    return hit


def _ada_kernel(c_ref, w_ref, b_ref, o_ref):
    o_ref[...] = jnp.dot(_silu(c_ref[...]), w_ref[...], preferred_element_type=F32,
                         precision=HIGHEST) + b_ref[...]


def _ada_modulation(c_pad, ada_w, ada_b):
    depth, d, n = ada_w.shape
    rows = c_pad.shape[0]
    tn = _tile(n, 1024)
    return pl.pallas_call(
        _ada_kernel,
        grid=(depth, n // tn),
        in_specs=[pl.BlockSpec((rows, d), lambda l, j: (0, 0)),
                  pl.BlockSpec((None, d, tn), lambda l, j: (l, 0, j)),
                  pl.BlockSpec((None, 1, tn), lambda l, j: (l, 0, j))],
        out_specs=pl.BlockSpec((None, rows, tn), lambda l, j: (l, 0, j)),
        out_shape=jax.ShapeDtypeStruct((depth, rows, n), F32),
        compiler_params=_params("arbitrary", "arbitrary"),
        name="ada_modulation",
    )(c_pad, ada_w, ada_b.reshape(depth, 1, n))


def _norm_mod(x, a, sh):
    ms = jnp.mean(x * x, axis=-1, keepdims=True)
    return (x * lax.rsqrt(ms + EPS)) * a + sh


def _nmm_kernel(*refs, with_dt):
    if with_dt:
        x_ref, a_ref, sh_ref, w_ref, wdt_ref, o_ref, odt_ref, h_scr = refs
    else:
        x_ref, a_ref, sh_ref, w_ref, o_ref, h_scr = refs

    @pl.when(pl.program_id(1) == 0)
    def _():
        hb = _norm_mod(x_ref[...], a_ref[...], sh_ref[...]).astype(BF16)
        h_scr[...] = hb
        if with_dt:
            odt_ref[...] = jnp.dot(hb, wdt_ref[...], preferred_element_type=F32)

    o_ref[...] = jnp.dot(h_scr[...], w_ref[...], preferred_element_type=F32).astype(o_ref.dtype)


def _norm_mod_matmul(x, a, sh, w, wdt, seq_starts):
    t, d = x.shape
    n = w.shape[1]
    tm = _row_tile(t, 1024, seq_starts)
    tn = _tile(n, 1024)
    with_dt = wdt is not None

    def seq_map(i, j):
        return (_seq_index(i * tm, seq_starts), 0, 0)

    in_specs = [pl.BlockSpec((tm, d), lambda i, j: (i, 0)),
                pl.BlockSpec((None, 1, d), seq_map),
                pl.BlockSpec((None, 1, d), seq_map),
                pl.BlockSpec((d, tn), lambda i, j: (0, j))]
    out_specs = [pl.BlockSpec((tm, tn), lambda i, j: (i, j))]
    out_shape = [jax.ShapeDtypeStruct((t, n), BF16)]
    args = [x, a, sh, w]
    if with_dt:
        in_specs.append(pl.BlockSpec((d, DT_PAD), lambda i, j: (0, 0)))
        out_specs.append(pl.BlockSpec((tm, DT_PAD), lambda i, j: (i, 0)))
        out_shape.append(jax.ShapeDtypeStruct((t, DT_PAD), F32))
        args.append(wdt)
    res = pl.pallas_call(
        functools.partial(_nmm_kernel, with_dt=with_dt),
        grid=(t // tm, n // tn),
        in_specs=in_specs, out_specs=out_specs, out_shape=out_shape,
        scratch_shapes=[pltpu.VMEM((tm, d), BF16)],
        compiler_params=_params("arbitrary", "arbitrary"),
        name="norm_mod_matmul",
    )(*args)
    return res if with_dt else res[0]


def _conv_kernel(cur_ref, prev_ref, next_ref, w_ref, b_ref, o_ref, ext_scr, *, tm, seq_starts, seq_ends):
    r0 = pl.program_id(0) * tm
    at_start = _is_any(r0, seq_starts)
    at_end = _is_any(r0 + tm, seq_ends)
    prev = prev_ref[...].astype(F32)[HALO - 8:HALO]
    nxt = next_ref[...].astype(F32)[0:8]
    ext_scr[0:8, :] = jnp.where(at_start, 0.0, prev)
    ext_scr[8:8 + tm, :] = cur_ref[...].astype(F32)
    ext_scr[8 + tm:16 + tm, :] = jnp.where(at_end, 0.0, nxt)
    pad = SSD_CONV // 2
    acc = b_ref[...] + w_ref[0:1, :] * ext_scr[8 - pad:8 - pad + tm, :]
    for k in range(1, SSD_CONV):
        acc = acc + w_ref[k:k + 1, :] * ext_scr[8 - pad + k:8 - pad + k + tm, :]
    o_ref[...] = _silu(acc).astype(o_ref.dtype)


def _conv_silu(proj, col0, conv_w, conv_b, seq_starts, seq_ends):
    t = proj.shape[0]
    tm = _row_tile(t, 512, seq_starts)
    tc = 1024
    assert col0 % tc == 0 and SSD_XBC % tc == 0 and tm % HALO == 0
    cb = col0 // tc
    hb = tm // HALO
    last_halo = t // HALO - 1
    return pl.pallas_call(
        functools.partial(_conv_kernel, tm=tm, seq_starts=seq_starts, seq_ends=seq_ends),
        grid=(t // tm, SSD_XBC // tc),
        in_specs=[pl.BlockSpec((tm, tc), lambda i, j: (i, cb + j)),
                  pl.BlockSpec((HALO, tc), lambda i, j: (jnp.maximum(i * hb - 1, 0), cb + j)),
                  pl.BlockSpec((HALO, tc), lambda i, j: (jnp.minimum((i + 1) * hb, last_halo), cb + j)),
                  pl.BlockSpec((SSD_CONV, tc), lambda i, j: (0, j)),
                  pl.BlockSpec((1, tc), lambda i, j: (0, j))],
        out_specs=pl.BlockSpec((tm, tc), lambda i, j: (i, j)),
        out_shape=jax.ShapeDtypeStruct((t, SSD_XBC), BF16),
        scratch_shapes=[pltpu.VMEM((tm + 16, tc), F32)],
        compiler_params=_params("arbitrary", "arbitrary"),
        name="ssd_conv_silu",
    )(proj, proj, proj, conv_w, conv_b.reshape(1, SSD_XBC))


def _ssd_kernel(*refs, reverse, dcol, nchunks, seq_starts, seq_ends):
    ch = SSD_CHUNK
    if reverse:
        (xact_ref, bt_ref, dt_ref, dtt_ref, bias_ref, a_ref, biast_ref, at_ref, e_ref,
         z_ref, yf_ref, dskip_ref, g_ref, o_ref, h_scr, y_scr) = refs
    else:
        (xact_ref, bt_ref, dt_ref, dtt_ref, bias_ref, a_ref, biast_ref, at_ref, e_ref,
         o_ref, h_scr) = refs
    step = pl.program_id(0)
    if reverse:
        row_end = (nchunks - step) * ch
        fresh = _is_any(row_end, seq_ends)
    else:
        fresh = _is_any(step * ch, seq_starts)

    @pl.when(fresh)
    def _():
        h_scr[...] = jnp.zeros_like(h_scr)

    dt = _softplus(dt_ref[:, dcol:dcol + SSD_HEADS] + bias_ref[...])
    dta = dt * a_ref[...]
    dtat = _softplus(dtt_ref[dcol:dcol + SSD_HEADS, :] + biast_ref[...]) * at_ref[...]
    row = lax.broadcasted_iota(jnp.int32, (ch, ch), 0)
    col = lax.broadcasted_iota(jnp.int32, (ch, ch), 1)
    if reverse:
        keep = col >= row
    else:
        keep = col <= row
    tri = jnp.where(keep, 1.0, 0.0).astype(F32)
    trit = jnp.where(keep, 0.0, 1.0).astype(F32) + jnp.where(row == col, 1.0, 0.0).astype(F32)
    cs = jnp.dot(tri, dta, preferred_element_type=F32, precision=HIGHEST)
    cst = jnp.dot(dtat, trit, preferred_element_type=F32, precision=HIGHEST)
    total = cs[0:1, :] if reverse else cs[ch - 1:ch, :]
    e = e_ref[...]
    dt_x = jnp.dot(dt, e, preferred_element_type=F32, precision=HIGHEST)
    ecs_x = jnp.dot(jnp.exp(cs), e, preferred_element_type=F32, precision=HIGHEST)
    dend_x = jnp.dot(jnp.exp(total - cs), e, preferred_element_type=F32, precision=HIGHEST)
    cdec_x = ecs_x[0:1, :] if reverse else ecs_x[ch - 1:ch, :]

    xs = xact_ref[:, 0:SSD_WIDTH].astype(F32)
    xd = xs * dt_x
    xdw = (xd * dend_x).astype(BF16)
    lane = lax.broadcasted_iota(jnp.int32, (1, 2 * SSD_HEAD_DIM), 1)
    first = lane < SSD_HEAD_DIM
    gw = SSD_WIDTH // SSD_GROUPS
    hpg = SSD_HEADS // SSD_GROUPS
    ssq = jnp.zeros((ch, 1), F32)
    for g in range(SSD_GROUPS):
        bgt = bt_ref[g * SSD_STATE:(g + 1) * SSD_STATE, :]
        c0 = SSD_WIDTH + SSD_BC + g * SSD_STATE
        cg = xact_ref[:, c0:c0 + SSD_STATE]
        scores = jnp.dot(cg, bgt, preferred_element_type=F32)
        hg = h_scr[g]
        yoff = jnp.dot(cg, hg.astype(BF16), preferred_element_type=F32) * ecs_x[:, g * gw:(g + 1) * gw]
        for pr in range(hpg // 2):
            h0 = g * hpg + 2 * pr
            lo = g * gw + pr * 2 * SSD_HEAD_DIM
            hi = lo + 2 * SSD_HEAD_DIM
            xpair = xd[:, lo:hi]
            xa = jnp.where(first, xpair, 0.0).astype(BF16)
            xb = jnp.where(first, 0.0, xpair).astype(BF16)
            seg0 = jnp.where(keep, cs[:, h0:h0 + 1] - cst[h0:h0 + 1, :], -jnp.inf)
            seg1 = jnp.where(keep, cs[:, h0 + 1:h0 + 2] - cst[h0 + 1:h0 + 2, :], -jnp.inf)
            m0 = (scores * jnp.exp(seg0)).astype(BF16)
            m1 = (scores * jnp.exp(seg1)).astype(BF16)
            y = (jnp.dot(m0, xa, preferred_element_type=F32) + jnp.dot(m1, xb, preferred_element_type=F32)
                 + yoff[:, pr * 2 * SSD_HEAD_DIM:(pr + 1) * 2 * SSD_HEAD_DIM])
            if reverse:
                y = y + yf_ref[:, lo:hi] + dskip_ref[:, lo:hi] * xs[:, lo:hi]
                y = y * _silu(z_ref[:, lo:hi].astype(F32))
                ssq = ssq + jnp.sum(y * y, axis=-1, keepdims=True)
                y_scr[:, lo:hi] = y
            else:
                o_ref[:, lo:hi] = y
        h_scr[g] = hg * cdec_x[:, g * gw:(g + 1) * gw] + jnp.dot(
            bgt, xdw[:, g * gw:(g + 1) * gw], preferred_element_type=F32)
    if reverse:
        inv = lax.rsqrt(ssq * (1.0 / SSD_WIDTH) + EPS)
        o_ref[...] = (y_scr[...] * inv * g_ref[...]).astype(o_ref.dtype)


def _ssd_direction(xact, bt, dt, dtt, bias, a, expand, seq_starts, seq_ends, reverse, final=None):
    t = xact.shape[0]
    ch = SSD_CHUNK
    nchunks = t // ch
    dcol = SSD_HEADS if reverse else 0

    def rb(i):
        return nchunks - 1 - i if reverse else i

    in_specs = [pl.BlockSpec((ch, SSD_XBC), lambda i: (rb(i), 0)),
                pl.BlockSpec((SSD_BC, ch), lambda i: (0, rb(i))),
                pl.BlockSpec((ch, DT_PAD), lambda i: (rb(i), 0)),
                pl.BlockSpec((DT_PAD, ch), lambda i: (0, rb(i))),
                pl.BlockSpec((1, SSD_HEADS), lambda i: (0, 0)),
                pl.BlockSpec((1, SSD_HEADS), lambda i: (0, 0)),
                pl.BlockSpec((SSD_HEADS, 1), lambda i: (0, 0)),
                pl.BlockSpec((SSD_HEADS, 1), lambda i: (0, 0)),
                pl.BlockSpec((SSD_HEADS, SSD_WIDTH), lambda i: (0, 0))]
    args = [xact, bt, dt, dtt, bias.reshape(1, -1), a.reshape(1, -1), bias.reshape(-1, 1), a.reshape(-1, 1), expand]
    scratch = [pltpu.VMEM((SSD_GROUPS, SSD_STATE, SSD_WIDTH // SSD_GROUPS), F32)]
    if reverse:
        proj, yf, dskip, g = final
        in_specs += [pl.BlockSpec((ch, SSD_WIDTH), lambda i: (rb(i), 0)),
                     pl.BlockSpec((ch, SSD_WIDTH), lambda i: (rb(i), 0)),
                     pl.BlockSpec((1, SSD_WIDTH), lambda i: (0, 0)),
                     pl.BlockSpec((1, SSD_WIDTH), lambda i: (0, 0))]
        args += [proj, yf, dskip, g]
        scratch.append(pltpu.VMEM((ch, SSD_WIDTH), F32))
    return pl.pallas_call(
        functools.partial(_ssd_kernel, reverse=reverse, dcol=dcol, nchunks=nchunks,
                          seq_starts=seq_starts, seq_ends=seq_ends),
        grid=(nchunks,),
        in_specs=in_specs,
        out_specs=pl.BlockSpec((ch, SSD_WIDTH), lambda i: (rb(i), 0)),
        out_shape=jax.ShapeDtypeStruct((t, SSD_WIDTH), BF16 if reverse else F32),
        scratch_shapes=scratch,
        compiler_params=_params("arbitrary"),
        name="ssd_bwd_gate_norm" if reverse else "ssd_fwd",
    )(*args)


def _rope_kernel(x_ref, cos_ref, sin_ref, o_ref, *, head_dim, heads):
    cos = cos_ref[...]
    sin = sin_ref[...]
    half = head_dim // 2
    for h in range(heads):
        x = x_ref[:, h * head_dim:(h + 1) * head_dim].astype(F32)
        if half % 128 == 0:
            rot = jnp.concatenate([x[:, half:], x[:, :half]], axis=-1)
        else:
            rot = pltpu.roll(x, half, 1)
        o_ref[:, h * head_dim:(h + 1) * head_dim] = (x * cos + rot * sin).astype(o_ref.dtype)


def _rope(proj, col0, width, head_dim, cos, sin, seq_starts):
    t = proj.shape[0]
    tm = _row_tile(t, 512, seq_starts)
    tc = 1024
    assert col0 % tc == 0 and width % tc == 0
    cb = col0 // tc
    def pos_map(i, j):
        r = i * tm
        s = _seq_index(r, seq_starts)
        st = 0
        for k, v in enumerate(seq_starts):
            st = st + jnp.where(s == k, v, 0)
        return ((r - st) // tm, 0)

    return pl.pallas_call(
        functools.partial(_rope_kernel, head_dim=head_dim, heads=tc // head_dim),
        grid=(t // tm, width // tc),
        in_specs=[pl.BlockSpec((tm, tc), lambda i, j: (i, cb + j)),
                  pl.BlockSpec((tm, head_dim), pos_map),
                  pl.BlockSpec((tm, head_dim), pos_map)],
        out_specs=pl.BlockSpec((tm, tc), lambda i, j: (i, j)),
        out_shape=jax.ShapeDtypeStruct((t, width), BF16),
        compiler_params=_params("arbitrary", "arbitrary"),
        name="rope",
    )(proj, cos, sin)


def _rope_tables(lmax, head_dim, scale):
    half = head_dim // 2
    inv = 1.0 / (ROPE_THETA ** (jnp.arange(half, dtype=F32) / half))
    ang = jnp.arange(lmax, dtype=F32)[:, None] * inv[None, :]
    cos = jnp.cos(ang) * scale
    sin = jnp.sin(ang) * scale
    return jnp.concatenate([cos, cos], axis=-1), jnp.concatenate([-sin, sin], axis=-1)


def _attn_kernel(lam_ref, q_ref, k_ref, v_ref, g_ref, *rest, tk, nkv, out_scale, aliased):
    if aliased:
        _, o_ref, m_scr, l_scr, acc_scr = rest
    else:
        o_ref, m_scr, l_scr, acc_scr = rest
    dh = DA_HEAD_DIM
    m_scr[...] = jnp.full_like(m_scr, -jnp.inf)
    l_scr[...] = jnp.zeros_like(l_scr)
    acc_scr[...] = jnp.zeros_like(acc_scr)

    def body(j, carry):
        off = pl.multiple_of(j * tk, tk)
        kj = k_ref[pl.ds(off, tk), :]
        vj = v_ref[pl.ds(off, tk), :]
        for u in range(2):
            s = lax.dot_general(q_ref[:, u * dh:(u + 1) * dh], kj[:, u * dh:(u + 1) * dh],
                                (((1,), (1,)), ((), ())), preferred_element_type=F32)
            m_prev = m_scr[u]
            m_new = jnp.maximum(m_prev, jnp.max(s, axis=-1, keepdims=True))
            alpha = jnp.exp(m_prev - m_new)
            p = jnp.exp(s - m_new)
            l_scr[u] = alpha * l_scr[u] + jnp.sum(p, axis=-1, keepdims=True)
            acc_scr[u] = alpha * acc_scr[u] + jnp.dot(p.astype(BF16), vj, preferred_element_type=F32)
            m_scr[u] = m_new
        return carry

    lax.fori_loop(0, nkv, body, 0)
    lam = lam_ref[0]
    o = acc_scr[0] / l_scr[0] - lam * (acc_scr[1] / l_scr[1])
    ms = jnp.mean(o * o, axis=-1, keepdims=True)
    o_ref[...] = (o * lax.rsqrt(ms + SUBLN_EPS) * g_ref[...] * out_scale).astype(o_ref.dtype)


def _diff_attention(qr, kr, proj, vcol0, lam, subln_g, out_scale, groups, prior=None):
    t = qr.shape[0]
    row_start, batch, seqlen = groups
    pw = 2 * DA_HEAD_DIM
    tq = _tile(seqlen, 512)
    tk = _tile(seqlen, 512)
    nq = seqlen // tq
    assert row_start % seqlen == 0 and vcol0 % pw == 0
    qb0 = row_start // tq
    sb0 = row_start // seqlen
    vb0 = vcol0 // pw
    aliased = prior is not None
    in_specs = [pl.BlockSpec(memory_space=pltpu.SMEM),
                pl.BlockSpec((tq, pw), lambda b, h, i: (qb0 + b * nq + i, h)),
                pl.BlockSpec((seqlen, pw), lambda b, h, i: (sb0 + b, h)),
                pl.BlockSpec((seqlen, pw), lambda b, h, i: (sb0 + b, vb0 + h)),
                pl.BlockSpec((1, pw), lambda b, h, i: (0, 0))]
    args = [lam, qr, kr, proj, subln_g]
    if aliased:
        in_specs.append(pl.BlockSpec(memory_space=pl.ANY))
        args.append(prior)
    return pl.pallas_call(
        functools.partial(_attn_kernel, tk=tk, nkv=seqlen // tk, out_scale=out_scale, aliased=aliased),
        grid=(batch, DA_HEADS, nq),
        in_specs=in_specs,
        out_specs=pl.BlockSpec((tq, pw), lambda b, h, i: (qb0 + b * nq + i, h)),
        out_shape=jax.ShapeDtypeStruct((t, DA_V_WIDTH), BF16),
        scratch_shapes=[pltpu.VMEM((2, tq, 1), F32), pltpu.VMEM((2, tq, 1), F32), pltpu.VMEM((2, tq, pw), F32)],
        input_output_aliases={5: 0} if aliased else {},
        compiler_params=_params("arbitrary", "arbitrary", "arbitrary"),
        name="diff_attention",
    )(*args)


def _outproj_kernel(y1_ref, y2_ref, w1_ref, w2_ref, x_ref, g_ref, o_ref):
    acc = jnp.dot(y1_ref[...], w1_ref[...], preferred_element_type=F32)
    acc = acc + jnp.dot(y2_ref[...], w2_ref[...], preferred_element_type=F32)
    o_ref[...] = x_ref[...] + g_ref[...] * acc


def _outproj_residual(y1, c1, y2, c2, w, x, gate, seq_starts):
    t, d = x.shape
    kh = w.shape[0] // 2
    tm = _row_tile(t, 512, seq_starts)
    tn = _tile(d, 1024)
    return pl.pallas_call(
        _outproj_kernel,
        grid=(d // tn, t // tm),
        in_specs=[pl.BlockSpec((tm, kh), lambda j, i: (i, c1)),
                  pl.BlockSpec((tm, kh), lambda j, i: (i, c2)),
                  pl.BlockSpec((kh, tn), lambda j, i: (0, j)),
                  pl.BlockSpec((kh, tn), lambda j, i: (1, j)),
                  pl.BlockSpec((tm, tn), lambda j, i: (i, j)),
                  pl.BlockSpec((None, 1, tn), lambda j, i: (_seq_index(i * tm, seq_starts), 0, j))],
        out_specs=pl.BlockSpec((tm, tn), lambda j, i: (i, j)),
        out_shape=jax.ShapeDtypeStruct((t, d), F32),
        compiler_params=_params("arbitrary", "arbitrary"),
        name="outproj_residual",
    )(y1, y2, w, w, x, gate)


def _router_kernel(x_ref, a_ref, sh_ref, wr_ref, br_ref, h_ref, lg_ref):
    h = _norm_mod(x_ref[...], a_ref[...], sh_ref[...])
    h_ref[...] = h.astype(h_ref.dtype)
    lg_ref[...] = jnp.dot(h, wr_ref[...], preferred_element_type=F32, precision=HIGHEST) + br_ref[...]


def _norm_mod_router(x, a, sh, wr, br, seq_starts):
    t, d = x.shape
    tm = _row_tile(t, 512, seq_starts)

    def seq_map(i):
        return (_seq_index(i * tm, seq_starts), 0, 0)

    return pl.pallas_call(
        _router_kernel,
        grid=(t // tm,),
        in_specs=[pl.BlockSpec((tm, d), lambda i: (i, 0)),
                  pl.BlockSpec((None, 1, d), seq_map),
                  pl.BlockSpec((None, 1, d), seq_map),
                  pl.BlockSpec((d, ROUTER_PAD), lambda i: (0, 0)),
                  pl.BlockSpec((1, ROUTER_PAD), lambda i: (0, 0))],
        out_specs=[pl.BlockSpec((tm, d), lambda i: (i, 0)),
                   pl.BlockSpec((tm, ROUTER_PAD), lambda i: (i, 0))],
        out_shape=[jax.ShapeDtypeStruct((t, d), BF16), jax.ShapeDtypeStruct((t, ROUTER_PAD), F32)],
        compiler_params=_params("arbitrary"),
        name="norm_mod_router",
    )(x, a, sh, wr, br)


def _moe_ffn_kernel(be_ref, nu_ref, xs_ref, w1_ref, w3_ref, w2_ref, o_ref):
    del be_ref

    @pl.when(pl.program_id(0) < nu_ref[0])
    def _():
        x = xs_ref[...]
        h1 = jnp.dot(x, w1_ref[...], preferred_element_type=F32)
        h3 = jnp.dot(x, w3_ref[...], preferred_element_type=F32)
        hid = (_silu(h1) * h3).astype(BF16)
        o_ref[...] = jnp.dot(hid, w2_ref[...], preferred_element_type=F32).astype(o_ref.dtype)


def _moe_ffn(xs, block_expert, n_used, w1, w3, w2):
    n_rows, d = xs.shape
    ff = w1.shape[2]
    blk = MOE_ROW_BLOCK
    n_blocks = n_rows // blk

    def row_map(i, be, nu):
        return (jnp.minimum(i, nu[0] - 1), 0)

    grid_spec = pltpu.PrefetchScalarGridSpec(
        num_scalar_prefetch=2,
        grid=(n_blocks,),
        in_specs=[pl.BlockSpec((blk, d), row_map),
                  pl.BlockSpec((None, d, ff), lambda i, be, nu: (be[i], 0, 0)),
                  pl.BlockSpec((None, d, ff), lambda i, be, nu: (be[i], 0, 0)),
                  pl.BlockSpec((None, ff, d), lambda i, be, nu: (be[i], 0, 0))],
        out_specs=pl.BlockSpec((blk, d), row_map))
    return pl.pallas_call(
        _moe_ffn_kernel,
        grid_spec=grid_spec,
        out_shape=jax.ShapeDtypeStruct((n_rows, d), BF16),
        compiler_params=_params("arbitrary"),
        name="moe_ffn",
    )(block_expert, n_used, xs, w1, w3, w2)


def _combine_kernel(x_ref, o0_ref, o1_ref, gt_ref, g_ref, *rest, final):
    if final:
        fg_ref, o_ref = rest
    else:
        (o_ref,) = rest
    gt = gt_ref[...]
    y = gt[:, 0:1] * o0_ref[...].astype(F32) + gt[:, 1:2] * o1_ref[...].astype(F32)
    x = x_ref[...] + g_ref[...] * y
    if final:
        ms = jnp.mean(x * x, axis=-1, keepdims=True)
        x = x * lax.rsqrt(ms + EPS) * fg_ref[...]
    o_ref[...] = x


def _moe_combine(x, o0, o1, gates, gate_mod, final_g, seq_starts):
    t, d = x.shape
    tm = _row_tile(t, 512, seq_starts)
    final = final_g is not None
    in_specs = [pl.BlockSpec((tm, d), lambda i: (i, 0)),
                pl.BlockSpec((tm, d), lambda i: (i, 0)),
                pl.BlockSpec((tm, d), lambda i: (i, 0)),
                pl.BlockSpec((tm, ROUTER_PAD), lambda i: (i, 0)),
                pl.BlockSpec((None, 1, d), lambda i: (_seq_index(i * tm, seq_starts), 0, 0))]
    args = [x, o0, o1, gates, gate_mod]
    if final:
        in_specs.append(pl.BlockSpec((1, d), lambda i: (0, 0)))
        args.append(final_g)
    return pl.pallas_call(
        functools.partial(_combine_kernel, final=final),
        grid=(t // tm,),
        in_specs=in_specs,
        out_specs=pl.BlockSpec((tm, d), lambda i: (i, 0)),
        out_shape=jax.ShapeDtypeStruct((t, d), F32),
        compiler_params=_params("arbitrary"),
        name="moe_combine_residual",
    )(*args)


def _ret_kernel(*refs, reverse, nchunks, seq_starts, seq_ends):
    rc = RET_CHUNK
    if reverse:
        (cd_ref, q_ref, kt_ref, v_ref, inner_ref, cross_ref, kvd_ref, of_ref, gate_ref, o_ref, st_scr) = refs
    else:
        (cd_ref, q_ref, kt_ref, v_ref, inner_ref, cross_ref, kvd_ref, o_ref, st_scr) = refs
    step = pl.program_id(0)
    if reverse:
        fresh = _is_any((nchunks - step) * rc, seq_ends)
    else:
        fresh = _is_any(step * rc, seq_starts)

    @pl.when(fresh)
    def _():
        st_scr[...] = jnp.zeros_like(st_scr)

    dk, dv = RET_QK_DIM, RET_V_DIM
    for h in range(RET_HEADS):
        q = q_ref[:, h * dk:(h + 1) * dk]
        kt = kt_ref[h * dk:(h + 1) * dk, :]
        v = v_ref[:, h * dv:(h + 1) * dv]
        s = jnp.dot(q, kt, preferred_element_type=F32) * inner_ref[h]
        st = st_scr[h]
        o = jnp.dot(s.astype(BF16), v, preferred_element_type=F32)
        o = o + jnp.dot(q, st.astype(BF16), preferred_element_type=F32) * cross_ref[h]
        ktd = (kt.astype(F32) * kvd_ref[h]).astype(BF16)
        st_scr[h] = st * cd_ref[h] + jnp.dot(ktd, v, preferred_element_type=F32)
        if reverse:
            o = o + of_ref[:, h * dv:(h + 1) * dv]
            ms = jnp.mean(o * o, axis=-1, keepdims=True)
            o = o * lax.rsqrt(ms + EPS) * _silu(gate_ref[:, h * dv:(h + 1) * dv].astype(F32))
        o_ref[:, h * dv:(h + 1) * dv] = o.astype(o_ref.dtype)


def _retention_direction(qr, krt, proj, vcol0, gcol0, log_decay, seq_starts, seq_ends, reverse, o_fwd=None):
    t = qr.shape[0]
    rc = _tile(min(s2 - s1 for s1, s2 in zip(seq_starts, seq_ends)), RET_CHUNK)
    assert rc == RET_CHUNK
    nchunks = t // rc
    pos = jnp.arange(rc, dtype=F32)
    diff = pos[:, None] - pos[None, :]
    lg = log_decay.astype(F32)
    if reverse:
        mask = diff < 0
        dist = -diff
        cross = jnp.exp((rc - pos)[None, :, None] * lg[:, None, None])
        kvd = jnp.exp(pos[None, None, :] * lg[:, None, None])
    else:
        mask = diff >= 0
        dist = diff
        cross = jnp.exp((pos + 1.0)[None, :, None] * lg[:, None, None])
        kvd = jnp.exp((rc - 1.0 - pos)[None, None, :] * lg[:, None, None])
    inner = jnp.where(mask[None], jnp.exp(jnp.where(mask, dist, 0.0)[None] * lg[:, None, None]), 0.0)
    cross = jnp.broadcast_to(cross, (RET_HEADS, rc, RET_V_DIM))
    kvd = jnp.broadcast_to(kvd, (RET_HEADS, RET_QK_DIM, rc))
    chunk_decay = jnp.exp(rc * lg)

    def rb(i):
        return nchunks - 1 - i if reverse else i

    vb = vcol0 // RET_V_WIDTH
    assert vcol0 % RET_V_WIDTH == 0 and gcol0 % RET_V_WIDTH == 0
    in_specs = [pl.BlockSpec(memory_space=pltpu.SMEM),
                pl.BlockSpec((rc, RET_QK_WIDTH), lambda i: (rb(i), 0)),
                pl.BlockSpec((RET_QK_WIDTH, rc), lambda i: (0, rb(i))),
                pl.BlockSpec((rc, RET_V_WIDTH), lambda i: (rb(i), vb)),
                pl.BlockSpec((RET_HEADS, rc, rc), lambda i: (0, 0, 0)),
                pl.BlockSpec((RET_HEADS, rc, RET_V_DIM), lambda i: (0, 0, 0)),
                pl.BlockSpec((RET_HEADS, RET_QK_DIM, rc), lambda i: (0, 0, 0))]
    args = [chunk_decay, qr, krt, proj, inner, cross, kvd]
    if reverse:
        gb = gcol0 // RET_V_WIDTH
        in_specs += [pl.BlockSpec((rc, RET_V_WIDTH), lambda i: (rb(i), 0)),
                     pl.BlockSpec((rc, RET_V_WIDTH), lambda i: (rb(i), gb))]
        args += [o_fwd, proj]
    return pl.pallas_call(
        functools.partial(_ret_kernel, reverse=reverse, nchunks=nchunks,
                          seq_starts=seq_starts, seq_ends=seq_ends),
        grid=(nchunks,),
        in_specs=in_specs,
        out_specs=pl.BlockSpec((rc, RET_V_WIDTH), lambda i: (rb(i), 0)),
        out_shape=jax.ShapeDtypeStruct((t, RET_V_WIDTH), BF16 if reverse else F32),
        scratch_shapes=[pltpu.VMEM((RET_HEADS, RET_QK_DIM, RET_V_DIM), F32)],
        compiler_params=_params("arbitrary"),
        name="retention_bwd_norm_gate" if reverse else "retention_fwd",
    )(*args)


def _ssd_diff_layer(x, a1, sh1, g1, p, i, layer, lay):
    seq_starts, seq_ends, groups = lay
    d = x.shape[1]
    w_in = p['ab_w_in'][i]
    o2 = SSD_WIDTH + SSD_XBC
    o3 = o2 + 2 * SSD_HEADS
    w_main = jnp.concatenate([w_in[:, :o2], w_in[:, o3:]], axis=1).astype(BF16)
    w_dt = jnp.pad(w_in[:, o2:o3], ((0, 0), (0, DT_PAD - 2 * SSD_HEADS))).astype(BF16)
    proj, dt = _norm_mod_matmul(x, a1, sh1, w_main, w_dt, seq_starts)
    qcol = o2
    kcol = qcol + DA_QK_WIDTH
    vcol = kcol + DA_QK_WIDTH

    xact = _conv_silu(proj, SSD_WIDTH, p['ssd_conv_w'][i], p['ssd_conv_b'][i], seq_starts, seq_ends)
    bt = xact[:, SSD_WIDTH:SSD_WIDTH + SSD_BC].T
    dtt = dt.T
    expand = (jnp.arange(SSD_WIDTH)[None, :] // SSD_HEAD_DIM == jnp.arange(SSD_HEADS)[:, None]).astype(F32)
    a_f = -jnp.exp(p['ssd_a_log_fwd'][i].astype(F32))
    a_b = -jnp.exp(p['ssd_a_log_bwd'][i].astype(F32))
    yf = _ssd_direction(xact, bt, dt, dtt, p['ssd_dt_bias_fwd'][i].astype(F32), a_f, expand,
                        seq_starts, seq_ends, reverse=False)
    dskip = jnp.repeat(p['ssd_d'][i].astype(F32), SSD_HEAD_DIM).reshape(1, SSD_WIDTH)
    y_ssd = _ssd_direction(xact, bt, dt, dtt, p['ssd_dt_bias_bwd'][i].astype(F32), a_b, expand,
                           seq_starts, seq_ends, reverse=True,
                           final=(proj, yf, dskip, p['ssd_norm_g'][i].astype(F32).reshape(1, SSD_WIDTH)))

    lam_init = 0.8 - 0.6 * math.exp(-0.3 * layer)
    lam = (jnp.exp(jnp.sum(p['da_lambda_q1'][i].astype(F32) * p['da_lambda_k1'][i].astype(F32)))
           - jnp.exp(jnp.sum(p['da_lambda_q2'][i].astype(F32) * p['da_lambda_k2'][i].astype(F32)))
           + lam_init).reshape(1)
    lmax = max(g[2] for g in groups)
    cos_q, sin_q = _rope_tables(lmax, DA_HEAD_DIM, DA_HEAD_DIM ** -0.5)
    cos_k, sin_k = _rope_tables(lmax, DA_HEAD_DIM, 1.0)
    qr = _rope(proj, qcol, DA_QK_WIDTH, DA_HEAD_DIM, cos_q, sin_q, seq_starts)
    kr = _rope(proj, kcol, DA_QK_WIDTH, DA_HEAD_DIM, cos_k, sin_k, seq_starts)
    subln = p['da_subln_g'][i].astype(F32).reshape(1, 2 * DA_HEAD_DIM)
    o = None
    for grp in groups:
        o = _diff_attention(qr, kr, proj, vcol, lam, subln, 1.0 - lam_init, grp, prior=o)

    w_out = p['ab_w_out'][i].astype(BF16)
    return _outproj_residual(y_ssd, 0, o, 0, w_out, x, g1, seq_starts)


def _retention_layer(x, a1, sh1, g1, p, j, lay):
    seq_starts, seq_ends, groups = lay
    w_in = p['ret_w_in'][j].astype(BF16)
    proj = _norm_mod_matmul(x, a1, sh1, w_in, None, seq_starts)
    lmax = max(g[2] for g in groups)
    cos_q, sin_q = _rope_tables(lmax, RET_QK_DIM, 1.0)
    cos_k, sin_k = _rope_tables(lmax, RET_QK_DIM, RET_QK_DIM ** -0.5)
    qr = _rope(proj, 0, RET_QK_WIDTH, RET_QK_DIM, cos_q, sin_q, seq_starts)
    kr = _rope(proj, RET_QK_WIDTH, RET_QK_WIDTH, RET_QK_DIM, cos_k, sin_k, seq_starts)
    krt = kr.T
    vcol = 2 * RET_QK_WIDTH
    gcol = vcol + RET_V_WIDTH
    lg_f = -jnp.exp(p['ret_log_decay_fwd'][j].astype(F32))
    lg_b = -jnp.exp(p['ret_log_decay_bwd'][j].astype(F32))
    of = _retention_direction(qr, krt, proj, vcol, gcol, lg_f, seq_starts, seq_ends, reverse=False)
    o = _retention_direction(qr, krt, proj, vcol, gcol, lg_b, seq_starts, seq_ends, reverse=True, o_fwd=of)
    w_out = p['ret_w_out'][j].astype(BF16)
    return _outproj_residual(o, 0, o, 1, w_out, x, g1, seq_starts)


def _route(logits):
    coarse = logits[:, :MOE_GROUPS]
    grp = jnp.argmax(coarse, axis=-1)
    p_grp = jnp.max(jax.nn.softmax(coarse, axis=-1), axis=-1)
    fine = logits[:, MOE_GROUPS:MOE_GROUPS + MOE_EXPERTS].reshape(-1, MOE_GROUPS, MOE_EXPERTS_PER_GROUP)
    fine = jnp.take_along_axis(fine, grp[:, None, None], axis=1)[:, 0]
    top_v, top_i = lax.top_k(fine, MOE_TOP_K)
    gates = jax.nn.softmax(top_v, axis=-1) * p_grp[:, None]
    experts = (grp[:, None] * MOE_EXPERTS_PER_GROUP + top_i).astype(jnp.int32)
    return experts, gates


def _moe_layer(x, a2, sh2, g2, p, layer, lay, final_g):
    seq_starts, _, _ = lay
    t, d = x.shape
    wr = jnp.concatenate([p['moe_w_group'][layer], p['moe_w_expert'][layer]], axis=1).astype(F32)
    br = jnp.concatenate([p['moe_b_group'][layer], p['moe_b_expert'][layer]]).astype(F32)
    nr = wr.shape[1]
    wr = jnp.pad(wr, ((0, 0), (0, ROUTER_PAD - nr)))
    br = jnp.pad(br, (0, ROUTER_PAD - nr)).reshape(1, ROUTER_PAD)
    h, logits = _norm_mod_router(x, a2, sh2, wr, br, seq_starts)
    experts, gates = _route(logits)

    blk = MOE_ROW_BLOCK
    n_assign = t * MOE_TOP_K
    e_flat = experts.reshape(n_assign)
    onehot = (e_flat[:, None] == jnp.arange(MOE_EXPERTS, dtype=jnp.int32)[None, :]).astype(jnp.int32)
    csum = jnp.cumsum(onehot, axis=0)
    counts = csum[-1]
    rank = jnp.sum(csum * onehot, axis=1) - 1
    padded = (counts + blk - 1) // blk * blk
    ends_pad = jnp.cumsum(padded)
    start_pad = ends_pad - padded
    dest = (start_pad[e_flat] + rank).astype(jnp.int32)
    n_rows = n_assign + MOE_EXPERTS * blk
    n_blocks = n_rows // blk
    tok_flat = jnp.repeat(jnp.arange(t, dtype=jnp.int32), MOE_TOP_K)
    row_tok = jnp.zeros((n_rows,), jnp.int32).at[dest].set(tok_flat)
    block_expert = jnp.minimum(
        jnp.searchsorted(ends_pad, jnp.arange(n_blocks, dtype=jnp.int32) * blk, side='right'),
        MOE_EXPERTS - 1).astype(jnp.int32)
    n_used = (ends_pad[-1] // blk).astype(jnp.int32).reshape(1)

    xs = jnp.take(h, row_tok, axis=0)
    out = _moe_ffn(xs, block_expert, n_used, p['moe_w1'][layer].astype(BF16),
                   p['moe_w3'][layer].astype(BF16), p['moe_w2'][layer].astype(BF16))
    dest2 = dest.reshape(t, MOE_TOP_K)
    o0 = jnp.take(out, dest2[:, 0], axis=0)
    o1 = jnp.take(out, dest2[:, 1], axis=0)
    gates_pad = jnp.pad(gates, ((0, 0), (0, ROUTER_PAD - MOE_TOP_K)))
    return _moe_combine(x, o0, o1, gates_pad, g2, final_g, seq_starts)


def kernel(x_prompt, x_sample, c_prompt, c_sample, ada_w, ada_b, norm_mix_g, norm_ffn_g, ab_w_in, ssd_conv_w, ssd_conv_b, ssd_a_log_fwd, ssd_a_log_bwd, ssd_dt_bias_fwd, ssd_dt_bias_bwd, ssd_d, ssd_norm_g, da_lambda_q1, da_lambda_k1, da_lambda_q2, da_lambda_k2, da_subln_g, ab_w_out, ret_w_in, ret_log_decay_fwd, ret_log_decay_bwd, ret_w_out, moe_w_group, moe_b_group, moe_w_expert, moe_b_expert, moe_w1, moe_w3, moe_w2, final_norm_g):
    p = dict(ab_w_in=ab_w_in, ssd_conv_w=ssd_conv_w, ssd_conv_b=ssd_conv_b,
             ssd_a_log_fwd=ssd_a_log_fwd, ssd_a_log_bwd=ssd_a_log_bwd,
             ssd_dt_bias_fwd=ssd_dt_bias_fwd, ssd_dt_bias_bwd=ssd_dt_bias_bwd, ssd_d=ssd_d,
             ssd_norm_g=ssd_norm_g, da_lambda_q1=da_lambda_q1, da_lambda_k1=da_lambda_k1,
             da_lambda_q2=da_lambda_q2, da_lambda_k2=da_lambda_k2, da_subln_g=da_subln_g,
             ab_w_out=ab_w_out, ret_w_in=ret_w_in, ret_log_decay_fwd=ret_log_decay_fwd,
             ret_log_decay_bwd=ret_log_decay_bwd, ret_w_out=ret_w_out, moe_w_group=moe_w_group,
             moe_b_group=moe_b_group, moe_w_expert=moe_w_expert, moe_b_expert=moe_b_expert,
             moe_w1=moe_w1, moe_w3=moe_w3, moe_w2=moe_w2)
    bp, lp, d = x_prompt.shape
    bs, ls, _ = x_sample.shape
    depth = ada_w.shape[0]
    groups = ((0, bp, lp), (bp * lp, bs, ls))
    seq_starts = tuple(r0 + b * l for (r0, nb, l) in groups for b in range(nb))
    seq_ends = tuple(r0 + (b + 1) * l for (r0, nb, l) in groups for b in range(nb))
    lay = (seq_starts, seq_ends, groups)
    nseq = len(seq_starts)
    x = jnp.concatenate([x_prompt.reshape(bp * lp, d), x_sample.reshape(bs * ls, d)], axis=0)
    c = jnp.concatenate([c_prompt, c_sample], axis=0).astype(F32)
    c_pad = jnp.pad(c, ((0, -nseq % 8), (0, 0)))
    mod = _ada_modulation(c_pad, ada_w.astype(F32), ada_b.astype(F32))[:, :nseq]

    for layer in range(depth):
        sh1, sc1, g1, sh2, sc2, g2 = [m.reshape(nseq, 1, d) for m in jnp.split(mod[layer], 6, axis=-1)]
        a1 = norm_mix_g[layer].astype(F32)[None, None, :] * (1.0 + sc1)
        a2 = norm_ffn_g[layer].astype(F32)[None, None, :] * (1.0 + sc2)
        if layer % 2 == 0:
            x = _ssd_diff_layer(x, a1, sh1, g1, p, layer // 2, layer, lay)
        else:
            x = _retention_layer(x, a1, sh1, g1, p, layer // 2, lay)
        final_g = final_norm_g.astype(F32).reshape(1, d) if layer == depth - 1 else None
        x = _moe_layer(x, a2, sh2, g2, p, layer, lay, final_g)
    y_prompt = x[:bp * lp].reshape(bp, lp, d)
    y_sample = x[bp * lp:].reshape(bs, ls, d)
    return (y_prompt, y_sample)
```

```python
import functools
import math

import jax
import jax.numpy as jnp
from jax import lax
from jax.experimental import pallas as pl
from jax.experimental.pallas import tpu as pltpu

F32 = jnp.float32
BF16 = jnp.bfloat16
HIGHEST = lax.Precision.HIGHEST

EPS = 1e-6
SUBLN_EPS = 1e-5
ROPE_THETA = 10000.0

SSD_HEADS = 32
SSD_HEAD_DIM = 64
SSD_WIDTH = SSD_HEADS * SSD_HEAD_DIM
SSD_GROUPS = 4
SSD_STATE = 128
SSD_CONV = 5
SSD_BC = SSD_GROUPS * SSD_STATE
SSD_XBC = SSD_WIDTH + 2 * SSD_BC
SSD_CHUNK = 128
DA_HEADS = 8
DA_HEAD_DIM = 128
DA_QK_WIDTH = 2 * DA_HEADS * DA_HEAD_DIM
DA_V_WIDTH = DA_HEADS * 2 * DA_HEAD_DIM
RET_HEADS = 8
RET_QK_DIM = 256
RET_V_DIM = 512
RET_QK_WIDTH = RET_HEADS * RET_QK_DIM
RET_V_WIDTH = RET_HEADS * RET_V_DIM
RET_CHUNK = 256
MOE_GROUPS = 4
MOE_EXPERTS_PER_GROUP = 8
MOE_EXPERTS = MOE_GROUPS * MOE_EXPERTS_PER_GROUP
MOE_TOP_K = 2
MOE_ROW_BLOCK = 512
ROUTER_PAD = 128
DT_PAD = 128
HALO = 16

VMEM_LIMIT_BYTES = 56 * 1024 * 1024


def _params(*semantics):
    return pltpu.CompilerParams(dimension_semantics=semantics, vmem_limit_bytes=VMEM_LIMIT_BYTES)


def _silu(x):
    return x * jax.nn.sigmoid(x)


def _softplus(x):
    return jnp.maximum(x, 0.0) + jnp.log1p(jnp.exp(-jnp.abs(x)))


def _tile(n, pref):
    t = min(n, pref)
    assert n % t == 0, (n, pref)
    return t


def _row_tile(t, pref, seq_starts):
    tm = min(t, pref)
    while t % tm or any(st % tm for st in seq_starts):
        tm //= 2
    return tm


def _seq_index(row, seq_starts):
    s = 0
    for st in seq_starts[1:]:
        s = s + jnp.where(row >= st, 1, 0)
    return s


def _is_any(row, values):
    hit = row == values[0]
    for v in values[1:]:
        hit = jnp.logical_or(hit, row == v)
    return hit


def _ada_kernel(c_ref, w_ref, b_ref, o_ref):
    o_ref[...] = jnp.dot(_silu(c_ref[...]), w_ref[...], preferred_element_type=F32,
                         precision=HIGHEST) + b_ref[...]


def _ada_modulation(c_pad, ada_w, ada_b):
    depth, d, n = ada_w.shape
    rows = c_pad.shape[0]
    tn = _tile(n, 1024)
    return pl.pallas_call(
        _ada_kernel,
        grid=(depth, n // tn),
        in_specs=[pl.BlockSpec((rows, d), lambda l, j: (0, 0)),
                  pl.BlockSpec((None, d, tn), lambda l, j: (l, 0, j)),
                  pl.BlockSpec((None, 1, tn), lambda l, j: (l, 0, j))],
        out_specs=pl.BlockSpec((None, rows, tn), lambda l, j: (l, 0, j)),
        out_shape=jax.ShapeDtypeStruct((depth, rows, n), F32),
        compiler_params=_params("arbitrary", "arbitrary"),
        name="ada_modulation",
    )(c_pad, ada_w, ada_b.reshape(depth, 1, n))


def _norm_mod(x, a, sh):
    ms = jnp.mean(x * x, axis=-1, keepdims=True)
    return (x * lax.rsqrt(ms + EPS)) * a + sh


def _nmm_kernel(*refs, with_dt):
    if with_dt:
        x_ref, a_ref, sh_ref, w_ref, wdt_ref, o_ref, odt_ref, h_scr = refs
    else:
        x_ref, a_ref, sh_ref, w_ref, o_ref, h_scr = refs

    @pl.when(pl.program_id(1) == 0)
    def _():
        hb = _norm_mod(x_ref[...], a_ref[...], sh_ref[...]).astype(BF16)
        h_scr[...] = hb
        if with_dt:
            odt_ref[...] = jnp.dot(hb, wdt_ref[...], preferred_element_type=F32)

    o_ref[...] = jnp.dot(h_scr[...], w_ref[...], preferred_element_type=F32).astype(o_ref.dtype)


def _norm_mod_matmul(x, a, sh, w, wdt, seq_starts):
    t, d = x.shape
    n = w.shape[1]
    tm = _row_tile(t, 1024, seq_starts)
    tn = _tile(n, 1024)
    with_dt = wdt is not None

    def seq_map(i, j):
        return (_seq_index(i * tm, seq_starts), 0, 0)

    in_specs = [pl.BlockSpec((tm, d), lambda i, j: (i, 0)),
                pl.BlockSpec((None, 1, d), seq_map),
                pl.BlockSpec((None, 1, d), seq_map),
                pl.BlockSpec((d, tn), lambda i, j: (0, j))]
    out_specs = [pl.BlockSpec((tm, tn), lambda i, j: (i, j))]
    out_shape = [jax.ShapeDtypeStruct((t, n), BF16)]
    args = [x, a, sh, w]
    if with_dt:
        in_specs.append(pl.BlockSpec((d, DT_PAD), lambda i, j: (0, 0)))
        out_specs.append(pl.BlockSpec((tm, DT_PAD), lambda i, j: (i, 0)))
        out_shape.append(jax.ShapeDtypeStruct((t, DT_PAD), F32))
        args.append(wdt)
    res = pl.pallas_call(
        functools.partial(_nmm_kernel, with_dt=with_dt),
        grid=(t // tm, n // tn),
        in_specs=in_specs, out_specs=out_specs, out_shape=out_shape,
        scratch_shapes=[pltpu.VMEM((tm, d), BF16)],
        compiler_params=_params("arbitrary", "arbitrary"),
        name="norm_mod_matmul",
    )(*args)
    return res if with_dt else res[0]


def _conv_kernel(cur_ref, prev_ref, next_ref, w_ref, b_ref, o_ref, ext_scr, *, tm, seq_starts, seq_ends):
    r0 = pl.program_id(0) * tm
    at_start = _is_any(r0, seq_starts)
    at_end = _is_any(r0 + tm, seq_ends)
    prev = prev_ref[...].astype(F32)[HALO - 8:HALO]
    nxt = next_ref[...].astype(F32)[0:8]
    ext_scr[0:8, :] = jnp.where(at_start, 0.0, prev)
    ext_scr[8:8 + tm, :] = cur_ref[...].astype(F32)
    ext_scr[8 + tm:16 + tm, :] = jnp.where(at_end, 0.0, nxt)
    pad = SSD_CONV // 2
    acc = b_ref[...] + w_ref[0:1, :] * ext_scr[8 - pad:8 - pad + tm, :]
    for k in range(1, SSD_CONV):
        acc = acc + w_ref[k:k + 1, :] * ext_scr[8 - pad + k:8 - pad + k + tm, :]
    o_ref[...] = _silu(acc).astype(o_ref.dtype)


def _conv_silu(proj, col0, conv_w, conv_b, seq_starts, seq_ends):
    t = proj.shape[0]
    tm = _row_tile(t, 512, seq_starts)
    tc = 1024
    assert col0 % tc == 0 and SSD_XBC % tc == 0 and tm % HALO == 0
    cb = col0 // tc
    hb = tm // HALO
    last_halo = t // HALO - 1
    return pl.pallas_call(
        functools.partial(_conv_kernel, tm=tm, seq_starts=seq_starts, seq_ends=seq_ends),
        grid=(t // tm, SSD_XBC // tc),
        in_specs=[pl.BlockSpec((tm, tc), lambda i, j: (i, cb + j)),
                  pl.BlockSpec((HALO, tc), lambda i, j: (jnp.maximum(i * hb - 1, 0), cb + j)),
                  pl.BlockSpec((HALO, tc), lambda i, j: (jnp.minimum((i + 1) * hb, last_halo), cb + j)),
                  pl.BlockSpec((SSD_CONV, tc), lambda i, j: (0, j)),
                  pl.BlockSpec((1, tc), lambda i, j: (0, j))],
        out_specs=pl.BlockSpec((tm, tc), lambda i, j: (i, j)),
        out_shape=jax.ShapeDtypeStruct((t, SSD_XBC), BF16),
        scratch_shapes=[pltpu.VMEM((tm + 16, tc), F32)],
        compiler_params=_params("arbitrary", "arbitrary"),
        name="ssd_conv_silu",
    )(proj, proj, proj, conv_w, conv_b.reshape(1, SSD_XBC))


def _ssd_kernel(*refs, reverse, dcol, nchunks, seq_starts, seq_ends):
    ch = SSD_CHUNK
    if reverse:
        (xact_ref, bt_ref, dt_ref, dtt_ref, bias_ref, a_ref, biast_ref, at_ref, e_ref,
         z_ref, yf_ref, dskip_ref, g_ref, o_ref, h_scr, y_scr) = refs
    else:
        (xact_ref, bt_ref, dt_ref, dtt_ref, bias_ref, a_ref, biast_ref, at_ref, e_ref,
         o_ref, h_scr) = refs
    step = pl.program_id(0)
    if reverse:
        row_end = (nchunks - step) * ch
        fresh = _is_any(row_end, seq_ends)
    else:
        fresh = _is_any(step * ch, seq_starts)

    @pl.when(fresh)
    def _():
        h_scr[...] = jnp.zeros_like(h_scr)

    dt = _softplus(dt_ref[:, dcol:dcol + SSD_HEADS] + bias_ref[...])
    dta = dt * a_ref[...]
    dtat = _softplus(dtt_ref[dcol:dcol + SSD_HEADS, :] + biast_ref[...]) * at_ref[...]
    row = lax.broadcasted_iota(jnp.int32, (ch, ch), 0)
    col = lax.broadcasted_iota(jnp.int32, (ch, ch), 1)
    if reverse:
        keep = col >= row
    else:
        keep = col <= row
    tri = jnp.where(keep, 1.0, 0.0).astype(F32)
    trit = jnp.where(keep, 0.0, 1.0).astype(F32) + jnp.where(row == col, 1.0, 0.0).astype(F32)
    cs = jnp.dot(tri, dta, preferred_element_type=F32, precision=HIGHEST)
    cst = jnp.dot(dtat, trit, preferred_element_type=F32, precision=HIGHEST)
    total = cs[0:1, :] if reverse else cs[ch - 1:ch, :]
    e = e_ref[...]
    dt_x = jnp.dot(dt, e, preferred_element_type=F32, precision=HIGHEST)
    ecs_x = jnp.dot(jnp.exp(cs), e, preferred_element_type=F32, precision=HIGHEST)
    dend_x = jnp.dot(jnp.exp(total - cs), e, preferred_element_type=F32, precision=HIGHEST)
    cdec_x = ecs_x[0:1, :] if reverse else ecs_x[ch - 1:ch, :]

    xs = xact_ref[:, 0:SSD_WIDTH].astype(F32)
    xd = xs * dt_x
    xdw = (xd * dend_x).astype(BF16)
    lane = lax.broadcasted_iota(jnp.int32, (1, 2 * SSD_HEAD_DIM), 1)
    first = lane < SSD_HEAD_DIM
    gw = SSD_WIDTH // SSD_GROUPS
    hpg = SSD_HEADS // SSD_GROUPS
    ssq = jnp.zeros((ch, 1), F32)
    for g in range(SSD_GROUPS):
        bgt = bt_ref[g * SSD_STATE:(g + 1) * SSD_STATE, :]
        c0 = SSD_WIDTH + SSD_BC + g * SSD_STATE
        cg = xact_ref[:, c0:c0 + SSD_STATE]
        scores = jnp.dot(cg, bgt, preferred_element_type=F32)
        hg = h_scr[g]
        yoff = jnp.dot(cg, hg.astype(BF16), preferred_element_type=F32) * ecs_x[:, g * gw:(g + 1) * gw]
        for pr in range(hpg // 2):
            h0 = g * hpg + 2 * pr
            lo = g * gw + pr * 2 * SSD_HEAD_DIM
            hi = lo + 2 * SSD_HEAD_DIM
            xpair = xd[:, lo:hi]
            xa = jnp.where(first, xpair, 0.0).astype(BF16)
            xb = jnp.where(first, 0.0, xpair).astype(BF16)
            seg0 = jnp.where(keep, cs[:, h0:h0 + 1] - cst[h0:h0 + 1, :], -jnp.inf)
            seg1 = jnp.where(keep, cs[:, h0 + 1:h0 + 2] - cst[h0 + 1:h0 + 2, :], -jnp.inf)
            m0 = (scores * jnp.exp(seg0)).astype(BF16)
            m1 = (scores * jnp.exp(seg1)).astype(BF16)
            y = (jnp.dot(m0, xa, preferred_element_type=F32) + jnp.dot(m1, xb, preferred_element_type=F32)
                 + yoff[:, pr * 2 * SSD_HEAD_DIM:(pr + 1) * 2 * SSD_HEAD_DIM])
            if reverse:
                y = y + yf_ref[:, lo:hi] + dskip_ref[:, lo:hi] * xs[:, lo:hi]
                y = y * _silu(z_ref[:, lo:hi].astype(F32))
                ssq = ssq + jnp.sum(y * y, axis=-1, keepdims=True)
                y_scr[:, lo:hi] = y
            else:
                o_ref[:, lo:hi] = y
        h_scr[g] = hg * cdec_x[:, g * gw:(g + 1) * gw] + jnp.dot(
            bgt, xdw[:, g * gw:(g + 1) * gw], preferred_element_type=F32)
    if reverse:
        inv = lax.rsqrt(ssq * (1.0 / SSD_WIDTH) + EPS)
        o_ref[...] = (y_scr[...] * inv * g_ref[...]).astype(o_ref.dtype)


def _ssd_direction(xact, bt, dt, dtt, bias, a, expand, seq_starts, seq_ends, reverse, final=None):
    t = xact.shape[0]
    ch = SSD_CHUNK
    nchunks = t // ch
    dcol = SSD_HEADS if reverse else 0

    def rb(i):
        return nchunks - 1 - i if reverse else i

    in_specs = [pl.BlockSpec((ch, SSD_XBC), lambda i: (rb(i), 0)),
                pl.BlockSpec((SSD_BC, ch), lambda i: (0, rb(i))),
                pl.BlockSpec((ch, DT_PAD), lambda i: (rb(i), 0)),
                pl.BlockSpec((DT_PAD, ch), lambda i: (0, rb(i))),
                pl.BlockSpec((1, SSD_HEADS), lambda i: (0, 0)),
                pl.BlockSpec((1, SSD_HEADS), lambda i: (0, 0)),
                pl.BlockSpec((SSD_HEADS, 1), lambda i: (0, 0)),
                pl.BlockSpec((SSD_HEADS, 1), lambda i: (0, 0)),
                pl.BlockSpec((SSD_HEADS, SSD_WIDTH), lambda i: (0, 0))]
    args = [xact, bt, dt, dtt, bias.reshape(1, -1), a.reshape(1, -1), bias.reshape(-1, 1), a.reshape(-1, 1), expand]
    scratch = [pltpu.VMEM((SSD_GROUPS, SSD_STATE, SSD_WIDTH // SSD_GROUPS), F32)]
    if reverse:
        proj, yf, dskip, g = final
        in_specs += [pl.BlockSpec((ch, SSD_WIDTH), lambda i: (rb(i), 0)),
                     pl.BlockSpec((ch, SSD_WIDTH), lambda i: (rb(i), 0)),
                     pl.BlockSpec((1, SSD_WIDTH), lambda i: (0, 0)),
                     pl.BlockSpec((1, SSD_WIDTH), lambda i: (0, 0))]
        args += [proj, yf, dskip, g]
        scratch.append(pltpu.VMEM((ch, SSD_WIDTH), F32))
    return pl.pallas_call(
        functools.partial(_ssd_kernel, reverse=reverse, dcol=dcol, nchunks=nchunks,
                          seq_starts=seq_starts, seq_ends=seq_ends),
        grid=(nchunks,),
        in_specs=in_specs,
        out_specs=pl.BlockSpec((ch, SSD_WIDTH), lambda i: (rb(i), 0)),
        out_shape=jax.ShapeDtypeStruct((t, SSD_WIDTH), BF16 if reverse else F32),
        scratch_shapes=scratch,
        compiler_params=_params("arbitrary"),
        name="ssd_bwd_gate_norm" if reverse else "ssd_fwd",
    )(*args)


def _rope_kernel(x_ref, cos_ref, sin_ref, o_ref, *, head_dim, heads):
    cos = cos_ref[...]
    sin = sin_ref[...]
    half = head_dim // 2
    for h in range(heads):
        x = x_ref[:, h * head_dim:(h + 1) * head_dim].astype(F32)
        if half % 128 == 0:
            rot = jnp.concatenate([x[:, half:], x[:, :half]], axis=-1)
        else:
            rot = pltpu.roll(x, half, 1)
        o_ref[:, h * head_dim:(h + 1) * head_dim] = (x * cos + rot * sin).astype(o_ref.dtype)


def _rope(proj, col0, width, head_dim, cos, sin, seq_starts):
    t = proj.shape[0]
    tm = _row_tile(t, 512, seq_starts)
    tc = 1024
    assert col0 % tc == 0 and width % tc == 0
    cb = col0 // tc
    def pos_map(i, j):
        r = i * tm
        s = _seq_index(r, seq_starts)
        st = 0
        for k, v in enumerate(seq_starts):
            st = st + jnp.where(s == k, v, 0)
        return ((r - st) // tm, 0)

    return pl.pallas_call(
        functools.partial(_rope_kernel, head_dim=head_dim, heads=tc // head_dim),
        grid=(t // tm, width // tc),
        in_specs=[pl.BlockSpec((tm, tc), lambda i, j: (i, cb + j)),
                  pl.BlockSpec((tm, head_dim), pos_map),
                  pl.BlockSpec((tm, head_dim), pos_map)],
        out_specs=pl.BlockSpec((tm, tc), lambda i, j: (i, j)),
        out_shape=jax.ShapeDtypeStruct((t, width), BF16),
        compiler_params=_params("arbitrary", "arbitrary"),
        name="rope",
    )(proj, cos, sin)


def _rope_tables(lmax, head_dim, scale):
    half = head_dim // 2
    inv = 1.0 / (ROPE_THETA ** (jnp.arange(half, dtype=F32) / half))
    ang = jnp.arange(lmax, dtype=F32)[:, None] * inv[None, :]
    cos = jnp.cos(ang) * scale
    sin = jnp.sin(ang) * scale
    return jnp.concatenate([cos, cos], axis=-1), jnp.concatenate([-sin, sin], axis=-1)


def _attn_kernel(lam_ref, q_ref, k_ref, v_ref, g_ref, *rest, tk, nkv, strip, out_scale, aliased):
    if aliased:
        rest = rest[1:]
    o_ref, m_scr, l_scr, acc_scr, s_scr, p_scr, alpha_scr = rest
    dh = DA_HEAD_DIM
    tq = q_ref.shape[0]
    lanes = m_scr.shape[-1]
    m_scr[...] = jnp.full_like(m_scr, -jnp.inf)
    l_scr[...] = jnp.zeros_like(l_scr)
    acc_scr[...] = jnp.zeros_like(acc_scr)

    def body(j, carry):
        off = pl.multiple_of(j * tk, tk)
        for u in range(2):
            s_scr[...] = lax.dot_general(q_ref[:, u * dh:(u + 1) * dh], k_ref[pl.ds(off, tk), u * dh:(u + 1) * dh],
                                         (((1,), (1,)), ((), ())), preferred_element_type=F32)
            for r in range(tq // strip):
                rows = slice(r * strip, (r + 1) * strip)
                s = s_scr[rows, :]
                m_prev = m_scr[u, rows, :]
                m_new = jnp.maximum(m_prev, jnp.max(s, axis=-1, keepdims=True))
                alpha = jnp.exp2(m_prev - m_new)
                p = jnp.exp2(s - pltpu.repeat(m_new, tk // lanes, 1))
                l_scr[u, rows, :] = alpha * l_scr[u, rows, :] + jnp.sum(p, axis=-1, keepdims=True)
                m_scr[u, rows, :] = m_new
                alpha_scr[rows, :] = alpha
                p_scr[rows, :] = p.astype(BF16)
            pv = jnp.dot(p_scr[...], v_ref[pl.ds(off, tk), :], preferred_element_type=F32)
            acc_scr[u] = acc_scr[u] * pltpu.repeat(alpha_scr[...], acc_scr.shape[-1] // lanes, 1) + pv
        return carry

    lax.fori_loop(0, nkv, body, 0)
    lam = lam_ref[0]
    rep = acc_scr.shape[-1] // lanes
    o = (acc_scr[0] / pltpu.repeat(l_scr[0], rep, 1)
         - lam * (acc_scr[1] / pltpu.repeat(l_scr[1], rep, 1)))
    ms = jnp.mean(o * o, axis=-1, keepdims=True)
    o_ref[...] = (o * lax.rsqrt(ms + SUBLN_EPS) * g_ref[...] * out_scale).astype(o_ref.dtype)


def _diff_attention(qr, kr, proj, vcol0, lam, subln_g, out_scale, groups, prior=None):
    t = qr.shape[0]
    row_start, batch, seqlen = groups
    pw = 2 * DA_HEAD_DIM
    tq = _tile(seqlen, 512)
    tk = _tile(seqlen, 1024)
    strip = 32
    lanes = 128
    nq = seqlen // tq
    assert row_start % seqlen == 0 and vcol0 % pw == 0
    qb0 = row_start // tq
    sb0 = row_start // seqlen
    vb0 = vcol0 // pw
    aliased = prior is not None
    in_specs = [pl.BlockSpec(memory_space=pltpu.SMEM),
                pl.BlockSpec((tq, pw), lambda b, h, i: (qb0 + b * nq + i, h)),
                pl.BlockSpec((seqlen, pw), lambda b, h, i: (sb0 + b, h)),
                pl.BlockSpec((seqlen, pw), lambda b, h, i: (sb0 + b, vb0 + h)),
                pl.BlockSpec((1, pw), lambda b, h, i: (0, 0))]
    args = [lam, qr, kr, proj, subln_g]
    if aliased:
        in_specs.append(pl.BlockSpec(memory_space=pl.ANY))
        args.append(prior)
    return pl.pallas_call(
        functools.partial(_attn_kernel, tk=tk, nkv=seqlen // tk, strip=strip, out_scale=out_scale,
                          aliased=aliased),
        grid=(batch, DA_HEADS, nq),
        in_specs=in_specs,
        out_specs=pl.BlockSpec((tq, pw), lambda b, h, i: (qb0 + b * nq + i, h)),
        out_shape=jax.ShapeDtypeStruct((t, DA_V_WIDTH), BF16),
        scratch_shapes=[pltpu.VMEM((2, tq, lanes), F32), pltpu.VMEM((2, tq, lanes), F32),
                        pltpu.VMEM((2, tq, pw), F32), pltpu.VMEM((tq, tk), F32), pltpu.VMEM((tq, tk), BF16),
                        pltpu.VMEM((tq, lanes), F32)],
        input_output_aliases={5: 0} if aliased else {},
        compiler_params=_params("arbitrary", "arbitrary", "arbitrary"),
        name="diff_attention",
    )(*args)


def _outproj_kernel(y1_ref, y2_ref, w1_ref, w2_ref, x_ref, g_ref, o_ref):
    acc = jnp.dot(y1_ref[...], w1_ref[...], preferred_element_type=F32)
    acc = acc + jnp.dot(y2_ref[...], w2_ref[...], preferred_element_type=F32)
    o_ref[...] = x_ref[...] + g_ref[...] * acc


def _outproj_residual(y1, c1, y2, c2, w, x, gate, seq_starts):
    t, d = x.shape
    kh = w.shape[0] // 2
    tm = _row_tile(t, 512, seq_starts)
    tn = _tile(d, 1024)
    return pl.pallas_call(
        _outproj_kernel,
        grid=(d // tn, t // tm),
        in_specs=[pl.BlockSpec((tm, kh), lambda j, i: (i, c1)),
                  pl.BlockSpec((tm, kh), lambda j, i: (i, c2)),
                  pl.BlockSpec((kh, tn), lambda j, i: (0, j)),
                  pl.BlockSpec((kh, tn), lambda j, i: (1, j)),
                  pl.BlockSpec((tm, tn), lambda j, i: (i, j)),
                  pl.BlockSpec((None, 1, tn), lambda j, i: (_seq_index(i * tm, seq_starts), 0, j))],
        out_specs=pl.BlockSpec((tm, tn), lambda j, i: (i, j)),
        out_shape=jax.ShapeDtypeStruct((t, d), F32),
        compiler_params=_params("arbitrary", "arbitrary"),
        name="outproj_residual",
    )(y1, y2, w, w, x, gate)


def _router_kernel(x_ref, a_ref, sh_ref, wr_ref, br_ref, h_ref, lg_ref):
    h = _norm_mod(x_ref[...], a_ref[...], sh_ref[...])
    h_ref[...] = h.astype(h_ref.dtype)
    lg_ref[...] = jnp.dot(h, wr_ref[...], preferred_element_type=F32, precision=HIGHEST) + br_ref[...]


def _norm_mod_router(x, a, sh, wr, br, seq_starts):
    t, d = x.shape
    tm = _row_tile(t, 512, seq_starts)

    def seq_map(i):
        return (_seq_index(i * tm, seq_starts), 0, 0)

    return pl.pallas_call(
        _router_kernel,
        grid=(t // tm,),
        in_specs=[pl.BlockSpec((tm, d), lambda i: (i, 0)),
                  pl.BlockSpec((None, 1, d), seq_map),
                  pl.BlockSpec((None, 1, d), seq_map),
                  pl.BlockSpec((d, ROUTER_PAD), lambda i: (0, 0)),
                  pl.BlockSpec((1, ROUTER_PAD), lambda i: (0, 0))],
        out_specs=[pl.BlockSpec((tm, d), lambda i: (i, 0)),
                   pl.BlockSpec((tm, ROUTER_PAD), lambda i: (i, 0))],
        out_shape=[jax.ShapeDtypeStruct((t, d), BF16), jax.ShapeDtypeStruct((t, ROUTER_PAD), F32)],
        compiler_params=_params("arbitrary"),
        name="norm_mod_router",
    )(x, a, sh, wr, br)


def _moe_ffn_kernel(be_ref, nu_ref, xs_ref, w1_ref, w3_ref, w2_ref, o_ref):
    del be_ref

    @pl.when(pl.program_id(0) < nu_ref[0])
    def _():
        x = xs_ref[...]
        h1 = jnp.dot(x, w1_ref[...], preferred_element_type=F32)
        h3 = jnp.dot(x, w3_ref[...], preferred_element_type=F32)
        hid = (_silu(h1) * h3).astype(BF16)
        o_ref[...] = jnp.dot(hid, w2_ref[...], preferred_element_type=F32).astype(o_ref.dtype)


def _moe_ffn(xs, block_expert, n_used, w1, w3, w2):
    n_rows, d = xs.shape
    ff = w1.shape[2]
    blk = MOE_ROW_BLOCK
    n_blocks = n_rows // blk

    def row_map(i, be, nu):
        return (jnp.minimum(i, nu[0] - 1), 0)

    grid_spec = pltpu.PrefetchScalarGridSpec(
        num_scalar_prefetch=2,
        grid=(n_blocks,),
        in_specs=[pl.BlockSpec((blk, d), row_map),
                  pl.BlockSpec((None, d, ff), lambda i, be, nu: (be[i], 0, 0)),
                  pl.BlockSpec((None, d, ff), lambda i, be, nu: (be[i], 0, 0)),
                  pl.BlockSpec((None, ff, d), lambda i, be, nu: (be[i], 0, 0))],
        out_specs=pl.BlockSpec((blk, d), row_map))
    return pl.pallas_call(
        _moe_ffn_kernel,
        grid_spec=grid_spec,
        out_shape=jax.ShapeDtypeStruct((n_rows, d), BF16),
        compiler_params=_params("arbitrary"),
        name="moe_ffn",
    )(block_expert, n_used, xs, w1, w3, w2)


def _combine_kernel(x_ref, o0_ref, o1_ref, gt_ref, g_ref, *rest, final):
    if final:
        fg_ref, o_ref = rest
    else:
        (o_ref,) = rest
    gt = gt_ref[...]
    y = gt[:, 0:1] * o0_ref[...].astype(F32) + gt[:, 1:2] * o1_ref[...].astype(F32)
    x = x_ref[...] + g_ref[...] * y
    if final:
        ms = jnp.mean(x * x, axis=-1, keepdims=True)
        x = x * lax.rsqrt(ms + EPS) * fg_ref[...]
    o_ref[...] = x


def _moe_combine(x, o0, o1, gates, gate_mod, final_g, seq_starts):
    t, d = x.shape
    tm = _row_tile(t, 512, seq_starts)
    final = final_g is not None
    in_specs = [pl.BlockSpec((tm, d), lambda i: (i, 0)),
                pl.BlockSpec((tm, d), lambda i: (i, 0)),
                pl.BlockSpec((tm, d), lambda i: (i, 0)),
                pl.BlockSpec((tm, ROUTER_PAD), lambda i: (i, 0)),
                pl.BlockSpec((None, 1, d), lambda i: (_seq_index(i * tm, seq_starts), 0, 0))]
    args = [x, o0, o1, gates, gate_mod]
    if final:
        in_specs.append(pl.BlockSpec((1, d), lambda i: (0, 0)))
        args.append(final_g)
    return pl.pallas_call(
        functools.partial(_combine_kernel, final=final),
        grid=(t // tm,),
        in_specs=in_specs,
        out_specs=pl.BlockSpec((tm, d), lambda i: (i, 0)),
        out_shape=jax.ShapeDtypeStruct((t, d), F32),
        compiler_params=_params("arbitrary"),
        name="moe_combine_residual",
    )(*args)


def _ret_kernel(*refs, reverse, nchunks, seq_starts, seq_ends):
    rc = RET_CHUNK
    if reverse:
        (cd_ref, q_ref, kt_ref, v_ref, inner_ref, cross_ref, kvd_ref, of_ref, gate_ref, o_ref, st_scr) = refs
    else:
        (cd_ref, q_ref, kt_ref, v_ref, inner_ref, cross_ref, kvd_ref, o_ref, st_scr) = refs
    step = pl.program_id(0)
    if reverse:
        fresh = _is_any((nchunks - step) * rc, seq_ends)
    else:
        fresh = _is_any(step * rc, seq_starts)

    @pl.when(fresh)
    def _():
        st_scr[...] = jnp.zeros_like(st_scr)

    dk, dv = RET_QK_DIM, RET_V_DIM
    for h in range(RET_HEADS):
        q = q_ref[:, h * dk:(h + 1) * dk]
        kt = kt_ref[h * dk:(h + 1) * dk, :]
        v = v_ref[:, h * dv:(h + 1) * dv]
        s = jnp.dot(q, kt, preferred_element_type=F32) * inner_ref[h]
        st = st_scr[h]
        o = jnp.dot(s.astype(BF16), v, preferred_element_type=F32)
        o = o + jnp.dot(q, st.astype(BF16), preferred_element_type=F32) * cross_ref[h]
        ktd = (kt.astype(F32) * kvd_ref[h]).astype(BF16)
        st_scr[h] = st * cd_ref[h] + jnp.dot(ktd, v, preferred_element_type=F32)
        if reverse:
            o = o + of_ref[:, h * dv:(h + 1) * dv]
            ms = jnp.mean(o * o, axis=-1, keepdims=True)
            o = o * lax.rsqrt(ms + EPS) * _silu(gate_ref[:, h * dv:(h + 1) * dv].astype(F32))
        o_ref[:, h * dv:(h + 1) * dv] = o.astype(o_ref.dtype)


def _retention_direction(qr, krt, proj, vcol0, gcol0, log_decay, seq_starts, seq_ends, reverse, o_fwd=None):
    t = qr.shape[0]
    rc = _tile(min(s2 - s1 for s1, s2 in zip(seq_starts, seq_ends)), RET_CHUNK)
    assert rc == RET_CHUNK
    nchunks = t // rc
    pos = jnp.arange(rc, dtype=F32)
    diff = pos[:, None] - pos[None, :]
    lg = log_decay.astype(F32)
    if reverse:
        mask = diff < 0
        dist = -diff
        cross = jnp.exp((rc - pos)[None, :, None] * lg[:, None, None])
        kvd = jnp.exp(pos[None, None, :] * lg[:, None, None])
    else:
        mask = diff >= 0
        dist = diff
        cross = jnp.exp((pos + 1.0)[None, :, None] * lg[:, None, None])
        kvd = jnp.exp((rc - 1.0 - pos)[None, None, :] * lg[:, None, None])
    inner = jnp.where(mask[None], jnp.exp(jnp.where(mask, dist, 0.0)[None] * lg[:, None, None]), 0.0)
    cross = jnp.broadcast_to(cross, (RET_HEADS, rc, RET_V_DIM))
    kvd = jnp.broadcast_to(kvd, (RET_HEADS, RET_QK_DIM, rc))
    chunk_decay = jnp.exp(rc * lg)

    def rb(i):
        return nchunks - 1 - i if reverse else i

    vb = vcol0 // RET_V_WIDTH
    assert vcol0 % RET_V_WIDTH == 0 and gcol0 % RET_V_WIDTH == 0
    in_specs = [pl.BlockSpec(memory_space=pltpu.SMEM),
                pl.BlockSpec((rc, RET_QK_WIDTH), lambda i: (rb(i), 0)),
                pl.BlockSpec((RET_QK_WIDTH, rc), lambda i: (0, rb(i))),
                pl.BlockSpec((rc, RET_V_WIDTH), lambda i: (rb(i), vb)),
                pl.BlockSpec((RET_HEADS, rc, rc), lambda i: (0, 0, 0)),
                pl.BlockSpec((RET_HEADS, rc, RET_V_DIM), lambda i: (0, 0, 0)),
                pl.BlockSpec((RET_HEADS, RET_QK_DIM, rc), lambda i: (0, 0, 0))]
    args = [chunk_decay, qr, krt, proj, inner, cross, kvd]
    if reverse:
        gb = gcol0 // RET_V_WIDTH
        in_specs += [pl.BlockSpec((rc, RET_V_WIDTH), lambda i: (rb(i), 0)),
                     pl.BlockSpec((rc, RET_V_WIDTH), lambda i: (rb(i), gb))]
        args += [o_fwd, proj]
    return pl.pallas_call(
        functools.partial(_ret_kernel, reverse=reverse, nchunks=nchunks,
                          seq_starts=seq_starts, seq_ends=seq_ends),
        grid=(nchunks,),
        in_specs=in_specs,
        out_specs=pl.BlockSpec((rc, RET_V_WIDTH), lambda i: (rb(i), 0)),
        out_shape=jax.ShapeDtypeStruct((t, RET_V_WIDTH), BF16 if reverse else F32),
        scratch_shapes=[pltpu.VMEM((RET_HEADS, RET_QK_DIM, RET_V_DIM), F32)],
        compiler_params=_params("arbitrary"),
        name="retention_bwd_norm_gate" if reverse else "retention_fwd",
    )(*args)


def _ssd_diff_layer(x, a1, sh1, g1, p, i, layer, lay):
    seq_starts, seq_ends, groups = lay
    d = x.shape[1]
    w_in = p['ab_w_in'][i]
    o2 = SSD_WIDTH + SSD_XBC
    o3 = o2 + 2 * SSD_HEADS
    w_main = jnp.concatenate([w_in[:, :o2], w_in[:, o3:]], axis=1).astype(BF16)
    w_dt = jnp.pad(w_in[:, o2:o3], ((0, 0), (0, DT_PAD - 2 * SSD_HEADS))).astype(BF16)
    proj, dt = _norm_mod_matmul(x, a1, sh1, w_main, w_dt, seq_starts)
    qcol = o2
    kcol = qcol + DA_QK_WIDTH
    vcol = kcol + DA_QK_WIDTH

    xact = _conv_silu(proj, SSD_WIDTH, p['ssd_conv_w'][i], p['ssd_conv_b'][i], seq_starts, seq_ends)
    bt = xact[:, SSD_WIDTH:SSD_WIDTH + SSD_BC].T
    dtt = dt.T
    expand = (jnp.arange(SSD_WIDTH)[None, :] // SSD_HEAD_DIM == jnp.arange(SSD_HEADS)[:, None]).astype(F32)
    a_f = -jnp.exp(p['ssd_a_log_fwd'][i].astype(F32))
    a_b = -jnp.exp(p['ssd_a_log_bwd'][i].astype(F32))
    yf = _ssd_direction(xact, bt, dt, dtt, p['ssd_dt_bias_fwd'][i].astype(F32), a_f, expand,
                        seq_starts, seq_ends, reverse=False)
    dskip = jnp.repeat(p['ssd_d'][i].astype(F32), SSD_HEAD_DIM).reshape(1, SSD_WIDTH)
    y_ssd = _ssd_direction(xact, bt, dt, dtt, p['ssd_dt_bias_bwd'][i].astype(F32), a_b, expand,
                           seq_starts, seq_ends, reverse=True,
                           final=(proj, yf, dskip, p['ssd_norm_g'][i].astype(F32).reshape(1, SSD_WIDTH)))

    lam_init = 0.8 - 0.6 * math.exp(-0.3 * layer)
    lam = (jnp.exp(jnp.sum(p['da_lambda_q1'][i].astype(F32) * p['da_lambda_k1'][i].astype(F32)))
           - jnp.exp(jnp.sum(p['da_lambda_q2'][i].astype(F32) * p['da_lambda_k2'][i].astype(F32)))
           + lam_init).reshape(1)
    lmax = max(g[2] for g in groups)
    cos_q, sin_q = _rope_tables(lmax, DA_HEAD_DIM, DA_HEAD_DIM ** -0.5 * math.log2(math.e))
    cos_k, sin_k = _rope_tables(lmax, DA_HEAD_DIM, 1.0)
    qr = _rope(proj, qcol, DA_QK_WIDTH, DA_HEAD_DIM, cos_q, sin_q, seq_starts)
    kr = _rope(proj, kcol, DA_QK_WIDTH, DA_HEAD_DIM, cos_k, sin_k, seq_starts)
    subln = p['da_subln_g'][i].astype(F32).reshape(1, 2 * DA_HEAD_DIM)
    o = None
    for grp in groups:
        o = _diff_attention(qr, kr, proj, vcol, lam, subln, 1.0 - lam_init, grp, prior=o)

    w_out = p['ab_w_out'][i].astype(BF16)
    return _outproj_residual(y_ssd, 0, o, 0, w_out, x, g1, seq_starts)


def _retention_layer(x, a1, sh1, g1, p, j, lay):
    seq_starts, seq_ends, groups = lay
    w_in = p['ret_w_in'][j].astype(BF16)
    proj = _norm_mod_matmul(x, a1, sh1, w_in, None, seq_starts)
    lmax = max(g[2] for g in groups)
    cos_q, sin_q = _rope_tables(lmax, RET_QK_DIM, 1.0)
    cos_k, sin_k = _rope_tables(lmax, RET_QK_DIM, RET_QK_DIM ** -0.5)
    qr = _rope(proj, 0, RET_QK_WIDTH, RET_QK_DIM, cos_q, sin_q, seq_starts)
    kr = _rope(proj, RET_QK_WIDTH, RET_QK_WIDTH, RET_QK_DIM, cos_k, sin_k, seq_starts)
    krt = kr.T
    vcol = 2 * RET_QK_WIDTH
    gcol = vcol + RET_V_WIDTH
    lg_f = -jnp.exp(p['ret_log_decay_fwd'][j].astype(F32))
    lg_b = -jnp.exp(p['ret_log_decay_bwd'][j].astype(F32))
    of = _retention_direction(qr, krt, proj, vcol, gcol, lg_f, seq_starts, seq_ends, reverse=False)
    o = _retention_direction(qr, krt, proj, vcol, gcol, lg_b, seq_starts, seq_ends, reverse=True, o_fwd=of)
    w_out = p['ret_w_out'][j].astype(BF16)
    return _outproj_residual(o, 0, o, 1, w_out, x, g1, seq_starts)


def _route(logits):
    coarse = logits[:, :MOE_GROUPS]
    grp = jnp.argmax(coarse, axis=-1)
    p_grp = jnp.max(jax.nn.softmax(coarse, axis=-1), axis=-1)
    fine = logits[:, MOE_GROUPS:MOE_GROUPS + MOE_EXPERTS].reshape(-1, MOE_GROUPS, MOE_EXPERTS_PER_GROUP)
    fine = jnp.take_along_axis(fine, grp[:, None, None], axis=1)[:, 0]
    top_v, top_i = lax.top_k(fine, MOE_TOP_K)
    gates = jax.nn.softmax(top_v, axis=-1) * p_grp[:, None]
    experts = (grp[:, None] * MOE_EXPERTS_PER_GROUP + top_i).astype(jnp.int32)
    return experts, gates


def _moe_layer(x, a2, sh2, g2, p, layer, lay, final_g):
    seq_starts, _, _ = lay
    t, d = x.shape
    wr = jnp.concatenate([p['moe_w_group'][layer], p['moe_w_expert'][layer]], axis=1).astype(F32)
    br = jnp.concatenate([p['moe_b_group'][layer], p['moe_b_expert'][layer]]).astype(F32)
    nr = wr.shape[1]
    wr = jnp.pad(wr, ((0, 0), (0, ROUTER_PAD - nr)))
    br = jnp.pad(br, (0, ROUTER_PAD - nr)).reshape(1, ROUTER_PAD)
    h, logits = _norm_mod_router(x, a2, sh2, wr, br, seq_starts)
    experts, gates = _route(logits)

    blk = MOE_ROW_BLOCK
    n_assign = t * MOE_TOP_K
    e_flat = experts.reshape(n_assign)
    onehot = (e_flat[:, None] == jnp.arange(MOE_EXPERTS, dtype=jnp.int32)[None, :]).astype(jnp.int32)
    csum = jnp.cumsum(onehot, axis=0)
    counts = csum[-1]
    rank = jnp.sum(csum * onehot, axis=1) - 1
    padded = (counts + blk - 1) // blk * blk
    ends_pad = jnp.cumsum(padded)
    start_pad = ends_pad - padded
    dest = (start_pad[e_flat] + rank).astype(jnp.int32)
    n_rows = n_assign + MOE_EXPERTS * blk
    n_blocks = n_rows // blk
    tok_flat = jnp.repeat(jnp.arange(t, dtype=jnp.int32), MOE_TOP_K)
    row_tok = (jnp.arange(n_rows, dtype=jnp.int32) % t).at[dest].set(tok_flat)
    block_expert = jnp.minimum(
        jnp.searchsorted(ends_pad, jnp.arange(n_blocks, dtype=jnp.int32) * blk, side='right'),
        MOE_EXPERTS - 1).astype(jnp.int32)
    n_used = (ends_pad[-1] // blk).astype(jnp.int32).reshape(1)

    xs = jnp.take(h, row_tok, axis=0)
    out = _moe_ffn(xs, block_expert, n_used, p['moe_w1'][layer].astype(BF16),
                   p['moe_w3'][layer].astype(BF16), p['moe_w2'][layer].astype(BF16))
    dest2 = dest.reshape(t, MOE_TOP_K)
    o0 = jnp.take(out, dest2[:, 0], axis=0)
    o1 = jnp.take(out, dest2[:, 1], axis=0)
    gates_pad = jnp.pad(gates, ((0, 0), (0, ROUTER_PAD - MOE_TOP_K)))
    return _moe_combine(x, o0, o1, gates_pad, g2, final_g, seq_starts)


def kernel(x_prompt, x_sample, c_prompt, c_sample, ada_w, ada_b, norm_mix_g, norm_ffn_g, ab_w_in, ssd_conv_w, ssd_conv_b, ssd_a_log_fwd, ssd_a_log_bwd, ssd_dt_bias_fwd, ssd_dt_bias_bwd, ssd_d, ssd_norm_g, da_lambda_q1, da_lambda_k1, da_lambda_q2, da_lambda_k2, da_subln_g, ab_w_out, ret_w_in, ret_log_decay_fwd, ret_log_decay_bwd, ret_w_out, moe_w_group, moe_b_group, moe_w_expert, moe_b_expert, moe_w1, moe_w3, moe_w2, final_norm_g):
    p = dict(ab_w_in=ab_w_in, ssd_conv_w=ssd_conv_w, ssd_conv_b=ssd_conv_b,
             ssd_a_log_fwd=ssd_a_log_fwd, ssd_a_log_bwd=ssd_a_log_bwd,
             ssd_dt_bias_fwd=ssd_dt_bias_fwd, ssd_dt_bias_bwd=ssd_dt_bias_bwd, ssd_d=ssd_d,
             ssd_norm_g=ssd_norm_g, da_lambda_q1=da_lambda_q1, da_lambda_k1=da_lambda_k1,
             da_lambda_q2=da_lambda_q2, da_lambda_k2=da_lambda_k2, da_subln_g=da_subln_g,
             ab_w_out=ab_w_out, ret_w_in=ret_w_in, ret_log_decay_fwd=ret_log_decay_fwd,
             ret_log_decay_bwd=ret_log_decay_bwd, ret_w_out=ret_w_out, moe_w_group=moe_w_group,
             moe_b_group=moe_b_group, moe_w_expert=moe_w_expert, moe_b_expert=moe_b_expert,
             moe_w1=moe_w1, moe_w3=moe_w3, moe_w2=moe_w2)
    bp, lp, d = x_prompt.shape
    bs, ls, _ = x_sample.shape
    depth = ada_w.shape[0]
    groups = ((0, bp, lp), (bp * lp, bs, ls))
    seq_starts = tuple(r0 + b * l for (r0, nb, l) in groups for b in range(nb))
    seq_ends = tuple(r0 + (b + 1) * l for (r0, nb, l) in groups for b in range(nb))
    lay = (seq_starts, seq_ends, groups)
    nseq = len(seq_starts)
    x = jnp.concatenate([x_prompt.reshape(bp * lp, d), x_sample.reshape(bs * ls, d)], axis=0)
    c = jnp.concatenate([c_prompt, c_sample], axis=0).astype(F32)
    c_pad = jnp.pad(c, ((0, -nseq % 8), (0, 0)))
    mod = _ada_modulation(c_pad, ada_w.astype(F32), ada_b.astype(F32))[:, :nseq]

    for layer in range(depth):
        sh1, sc1, g1, sh2, sc2, g2 = [m.reshape(nseq, 1, d) for m in jnp.split(mod[layer], 6, axis=-1)]
        a1 = norm_mix_g[layer].astype(F32)[None, None, :] * (1.0 + sc1)
        a2 = norm_ffn_g[layer].astype(F32)[None, None, :] * (1.0 + sc2)
        if layer % 2 == 0:
            x = _ssd_diff_layer(x, a1, sh1, g1, p, layer // 2, layer, lay)
        else:
            x = _retention_layer(x, a1, sh1, g1, p, layer // 2, lay)
        final_g = final_norm_g.astype(F32).reshape(1, d) if layer == depth - 1 else None
        x = _moe_layer(x, a2, sh2, g2, p, layer, lay, final_g)
    y_prompt = x[:bp * lp].reshape(bp, lp, d)
    y_sample = x[bp * lp:].reshape(bs, ls, d)
    return (y_prompt, y_sample)
```

```python
import functools
import math

import jax
import jax.numpy as jnp
from jax import lax
from jax.experimental import pallas as pl
from jax.experimental.pallas import tpu as pltpu

F32 = jnp.float32
BF16 = jnp.bfloat16
HIGHEST = lax.Precision.HIGHEST

EPS = 1e-6
SUBLN_EPS = 1e-5
ROPE_THETA = 10000.0

SSD_HEADS = 32
SSD_HEAD_DIM = 64
SSD_WIDTH = SSD_HEADS * SSD_HEAD_DIM
SSD_GROUPS = 4
SSD_STATE = 128
SSD_CONV = 5
SSD_BC = SSD_GROUPS * SSD_STATE
SSD_XBC = SSD_WIDTH + 2 * SSD_BC
SSD_CHUNK = 128
DA_HEADS = 8
DA_HEAD_DIM = 128
DA_QK_WIDTH = 2 * DA_HEADS * DA_HEAD_DIM
DA_V_WIDTH = DA_HEADS * 2 * DA_HEAD_DIM
RET_HEADS = 8
RET_QK_DIM = 256
RET_V_DIM = 512
RET_QK_WIDTH = RET_HEADS * RET_QK_DIM
RET_V_WIDTH = RET_HEADS * RET_V_DIM
RET_CHUNK = 256
MOE_GROUPS = 4
MOE_EXPERTS_PER_GROUP = 8
MOE_EXPERTS = MOE_GROUPS * MOE_EXPERTS_PER_GROUP
MOE_TOP_K = 2
MOE_ROW_BLOCK = 512
ROUTER_PAD = 128
DT_PAD = 128
HALO = 16

VMEM_LIMIT_BYTES = 56 * 1024 * 1024


def _params(*semantics):
    return pltpu.CompilerParams(dimension_semantics=semantics, vmem_limit_bytes=VMEM_LIMIT_BYTES)


def _silu(x):
    return x * jax.nn.sigmoid(x)


def _softplus(x):
    return jnp.maximum(x, 0.0) + jnp.log1p(jnp.exp(-jnp.abs(x)))


def _tile(n, pref):
    t = min(n, pref)
    assert n % t == 0, (n, pref)
    return t


def _row_tile(t, pref, seq_starts):
    tm = min(t, pref)
    while t % tm or any(st % tm for st in seq_starts):
        tm //= 2
    return tm


def _seq_index(row, seq_starts):
    s = 0
    for st in seq_starts[1:]:
        s = s + jnp.where(row >= st, 1, 0)
    return s


def _is_any(row, values):
    hit = row == values[0]
    for v in values[1:]:
        hit = jnp.logical_or(hit, row == v)
    return hit


def _ada_kernel(c_ref, w_ref, b_ref, o_ref):
    o_ref[...] = jnp.dot(_silu(c_ref[...]), w_ref[...], preferred_element_type=F32,
                         precision=HIGHEST) + b_ref[...]


def _ada_modulation(c_pad, ada_w, ada_b):
    depth, d, n = ada_w.shape
    rows = c_pad.shape[0]
    tn = _tile(n, 1024)
    return pl.pallas_call(
        _ada_kernel,
        grid=(depth, n // tn),
        in_specs=[pl.BlockSpec((rows, d), lambda l, j: (0, 0)),
                  pl.BlockSpec((None, d, tn), lambda l, j: (l, 0, j)),
                  pl.BlockSpec((None, 1, tn), lambda l, j: (l, 0, j))],
        out_specs=pl.BlockSpec((None, rows, tn), lambda l, j: (l, 0, j)),
        out_shape=jax.ShapeDtypeStruct((depth, rows, n), F32),
        compiler_params=_params("arbitrary", "arbitrary"),
        name="ada_modulation",
    )(c_pad, ada_w, ada_b.reshape(depth, 1, n))


def _norm_mod(x, a, sh):
    ms = jnp.mean(x * x, axis=-1, keepdims=True)
    return (x * lax.rsqrt(ms + EPS)) * a + sh


def _nmm_kernel(*refs, with_dt):
    if with_dt:
        x_ref, a_ref, sh_ref, w_ref, wdt_ref, o_ref, odt_ref, h_scr = refs
    else:
        x_ref, a_ref, sh_ref, w_ref, o_ref, h_scr = refs

    @pl.when(pl.program_id(1) == 0)
    def _():
        hb = _norm_mod(x_ref[...], a_ref[...], sh_ref[...]).astype(BF16)
        h_scr[...] = hb
        if with_dt:
            odt_ref[...] = jnp.dot(hb, wdt_ref[...], preferred_element_type=F32)

    o_ref[...] = jnp.dot(h_scr[...], w_ref[...], preferred_element_type=F32).astype(o_ref.dtype)


def _norm_mod_matmul(x, a, sh, w, wdt, seq_starts):
    t, d = x.shape
    n = w.shape[1]
    tm = _row_tile(t, 1024, seq_starts)
    tn = _tile(n, 1024)
    with_dt = wdt is not None

    def seq_map(i, j):
        return (_seq_index(i * tm, seq_starts), 0, 0)

    in_specs = [pl.BlockSpec((tm, d), lambda i, j: (i, 0)),
                pl.BlockSpec((None, 1, d), seq_map),
                pl.BlockSpec((None, 1, d), seq_map),
                pl.BlockSpec((d, tn), lambda i, j: (0, j))]
    out_specs = [pl.BlockSpec((tm, tn), lambda i, j: (i, j))]
    out_shape = [jax.ShapeDtypeStruct((t, n), BF16)]
    args = [x, a, sh, w]
    if with_dt:
        in_specs.append(pl.BlockSpec((d, DT_PAD), lambda i, j: (0, 0)))
        out_specs.append(pl.BlockSpec((tm, DT_PAD), lambda i, j: (i, 0)))
        out_shape.append(jax.ShapeDtypeStruct((t, DT_PAD), F32))
        args.append(wdt)
    res = pl.pallas_call(
        functools.partial(_nmm_kernel, with_dt=with_dt),
        grid=(t // tm, n // tn),
        in_specs=in_specs, out_specs=out_specs, out_shape=out_shape,
        scratch_shapes=[pltpu.VMEM((tm, d), BF16)],
        compiler_params=_params("arbitrary", "arbitrary"),
        name="norm_mod_matmul",
    )(*args)
    return res if with_dt else res[0]


def _conv_kernel(cur_ref, prev_ref, next_ref, w_ref, b_ref, o_ref, ext_scr, *, tm, seq_starts, seq_ends):
    r0 = pl.program_id(0) * tm
    at_start = _is_any(r0, seq_starts)
    at_end = _is_any(r0 + tm, seq_ends)
    prev = prev_ref[...].astype(F32)[HALO - 8:HALO]
    nxt = next_ref[...].astype(F32)[0:8]
    ext_scr[0:8, :] = jnp.where(at_start, 0.0, prev)
    ext_scr[8:8 + tm, :] = cur_ref[...].astype(F32)
    ext_scr[8 + tm:16 + tm, :] = jnp.where(at_end, 0.0, nxt)
    pad = SSD_CONV // 2
    acc = b_ref[...] + w_ref[0:1, :] * ext_scr[8 - pad:8 - pad + tm, :]
    for k in range(1, SSD_CONV):
        acc = acc + w_ref[k:k + 1, :] * ext_scr[8 - pad + k:8 - pad + k + tm, :]
    o_ref[...] = _silu(acc).astype(o_ref.dtype)


def _conv_silu(proj, col0, conv_w, conv_b, seq_starts, seq_ends):
    t = proj.shape[0]
    tm = _row_tile(t, 512, seq_starts)
    tc = 1024
    assert col0 % tc == 0 and SSD_XBC % tc == 0 and tm % HALO == 0
    cb = col0 // tc
    hb = tm // HALO
    last_halo = t // HALO - 1
    return pl.pallas_call(
        functools.partial(_conv_kernel, tm=tm, seq_starts=seq_starts, seq_ends=seq_ends),
        grid=(t // tm, SSD_XBC // tc),
        in_specs=[pl.BlockSpec((tm, tc), lambda i, j: (i, cb + j)),
                  pl.BlockSpec((HALO, tc), lambda i, j: (jnp.maximum(i * hb - 1, 0), cb + j)),
                  pl.BlockSpec((HALO, tc), lambda i, j: (jnp.minimum((i + 1) * hb, last_halo), cb + j)),
                  pl.BlockSpec((SSD_CONV, tc), lambda i, j: (0, j)),
                  pl.BlockSpec((1, tc), lambda i, j: (0, j))],
        out_specs=pl.BlockSpec((tm, tc), lambda i, j: (i, j)),
        out_shape=jax.ShapeDtypeStruct((t, SSD_XBC), BF16),
        scratch_shapes=[pltpu.VMEM((tm + 16, tc), F32)],
        compiler_params=_params("arbitrary", "arbitrary"),
        name="ssd_conv_silu",
    )(proj, proj, proj, conv_w, conv_b.reshape(1, SSD_XBC))


def _ssd_kernel(*refs, reverse, dcol, nchunks, seq_starts, seq_ends):
    ch = SSD_CHUNK
    if reverse:
        (xact_ref, bt_ref, dt_ref, dtt_ref, bias_ref, a_ref, biast_ref, at_ref, e_ref,
         z_ref, yf_ref, dskip_ref, g_ref, o_ref, h_scr, y_scr) = refs
    else:
        (xact_ref, bt_ref, dt_ref, dtt_ref, bias_ref, a_ref, biast_ref, at_ref, e_ref,
         o_ref, h_scr) = refs
    step = pl.program_id(0)
    if reverse:
        row_end = (nchunks - step) * ch
        fresh = _is_any(row_end, seq_ends)
    else:
        fresh = _is_any(step * ch, seq_starts)

    @pl.when(fresh)
    def _():
        h_scr[...] = jnp.zeros_like(h_scr)

    dt = _softplus(dt_ref[:, dcol:dcol + SSD_HEADS] + bias_ref[...])
    dta = dt * a_ref[...]
    dtt = _softplus(dtt_ref[dcol:dcol + SSD_HEADS, :] + biast_ref[...])
    dtat = dtt * at_ref[...]
    row = lax.broadcasted_iota(jnp.int32, (ch, ch), 0)
    col = lax.broadcasted_iota(jnp.int32, (ch, ch), 1)
    if reverse:
        keep = col >= row
    else:
        keep = col <= row
    tri = jnp.where(keep, 1.0, 0.0).astype(F32)
    trit = jnp.where(keep, 0.0, 1.0).astype(F32) + jnp.where(row == col, 1.0, 0.0).astype(F32)
    cs = jnp.dot(tri, dta, preferred_element_type=F32, precision=HIGHEST)
    cst = jnp.dot(dtat, trit, preferred_element_type=F32, precision=HIGHEST)
    last = 0 if reverse else ch - 1
    ecs = jnp.exp(cs)
    wt = dtt * jnp.exp(cst[:, last:last + 1] - cst)
    cdec_x = jnp.dot(jnp.broadcast_to(ecs[last:last + 1, :], (8, SSD_HEADS)), e_ref[...],
                     preferred_element_type=F32, precision=HIGHEST)[0:1, :]

    lane = lax.broadcasted_iota(jnp.int32, (1, 2 * SSD_HEAD_DIM), 1)
    first = lane < SSD_HEAD_DIM
    gw = SSD_WIDTH // SSD_GROUPS
    hpg = SSD_HEADS // SSD_GROUPS
    pw = 2 * SSD_HEAD_DIM
    ssq = jnp.zeros((ch, pw), F32)
    for g in range(SSD_GROUPS):
        bgt = bt_ref[g * SSD_STATE:(g + 1) * SSD_STATE, :]
        c0 = SSD_WIDTH + SSD_BC + g * SSD_STATE
        cg = xact_ref[:, c0:c0 + SSD_STATE]
        scores = jnp.dot(cg, bgt, preferred_element_type=F32)
        cgf = cg.astype(F32)
        bgtf = bgt.astype(F32)
        for pr in range(hpg // 2):
            lo = g * gw + pr * pw
            hi = lo + pw
            xpair = xact_ref[:, lo:hi].astype(F32)
            hpair = h_scr[g, :, pr * pw:(pr + 1) * pw]
            xsel = (jnp.where(first, xpair, 0.0).astype(BF16), jnp.where(first, 0.0, xpair).astype(BF16))
            hsel = (jnp.where(first, hpair, 0.0).astype(BF16), jnp.where(first, 0.0, hpair).astype(BF16))
            y = None
            bws = []
            for u in range(2):
                h = g * hpg + 2 * pr + u
                csb = jnp.broadcast_to(cs[:, h:h + 1], (ch, ch))
                seg = jnp.where(keep, csb - cst[h:h + 1, :], -jnp.inf)
                m = (scores * jnp.exp(seg) * dtt[h:h + 1, :]).astype(BF16)
                ce = (cgf * jnp.exp(csb)).astype(BF16)
                part = jnp.dot(jnp.concatenate([m, ce], axis=1), jnp.concatenate([xsel[u], hsel[u]], axis=0),
                               preferred_element_type=F32)
                y = part if y is None else y + part
                bws.append((bgtf * wt[h:h + 1, :]).astype(BF16))
            h_scr[g, :, pr * pw:(pr + 1) * pw] = hpair * cdec_x[:, lo:hi] + jnp.dot(
                jnp.concatenate(bws, axis=1), jnp.concatenate(xsel, axis=0), preferred_element_type=F32)
            if reverse:
                y = y + yf_ref[:, lo:hi] + dskip_ref[:, lo:hi] * xpair
                y = y * _silu(z_ref[:, lo:hi].astype(F32))
                ssq = ssq + y * y
                y_scr[:, lo:hi] = y
            else:
                o_ref[:, lo:hi] = y
    if reverse:
        inv = lax.rsqrt(jnp.sum(ssq, axis=-1, keepdims=True) * (1.0 / SSD_WIDTH) + EPS)
        o_ref[...] = (y_scr[...] * inv * g_ref[...]).astype(o_ref.dtype)


def _ssd_direction(xact, bt, dt, dtt, bias, a, expand, seq_starts, seq_ends, reverse, final=None):
    t = xact.shape[0]
    ch = SSD_CHUNK
    nchunks = t // ch
    dcol = SSD_HEADS if reverse else 0

    def rb(i):
        return nchunks - 1 - i if reverse else i

    in_specs = [pl.BlockSpec((ch, SSD_XBC), lambda i: (rb(i), 0)),
                pl.BlockSpec((SSD_BC, ch), lambda i: (0, rb(i))),
                pl.BlockSpec((ch, DT_PAD), lambda i: (rb(i), 0)),
                pl.BlockSpec((DT_PAD, ch), lambda i: (0, rb(i))),
                pl.BlockSpec((1, SSD_HEADS), lambda i: (0, 0)),
                pl.BlockSpec((1, SSD_HEADS), lambda i: (0, 0)),
                pl.BlockSpec((SSD_HEADS, 1), lambda i: (0, 0)),
                pl.BlockSpec((SSD_HEADS, 1), lambda i: (0, 0)),
                pl.BlockSpec((SSD_HEADS, SSD_WIDTH), lambda i: (0, 0))]
    args = [xact, bt, dt, dtt, bias.reshape(1, -1), a.reshape(1, -1), bias.reshape(-1, 1), a.reshape(-1, 1), expand]
    scratch = [pltpu.VMEM((SSD_GROUPS, SSD_STATE, SSD_WIDTH // SSD_GROUPS), F32)]
    if reverse:
        proj, yf, dskip, g = final
        in_specs += [pl.BlockSpec((ch, SSD_WIDTH), lambda i: (rb(i), 0)),
                     pl.BlockSpec((ch, SSD_WIDTH), lambda i: (rb(i), 0)),
                     pl.BlockSpec((1, SSD_WIDTH), lambda i: (0, 0)),
                     pl.BlockSpec((1, SSD_WIDTH), lambda i: (0, 0))]
        args += [proj, yf, dskip, g]
        scratch.append(pltpu.VMEM((ch, SSD_WIDTH), F32))
    return pl.pallas_call(
        functools.partial(_ssd_kernel, reverse=reverse, dcol=dcol, nchunks=nchunks,
                          seq_starts=seq_starts, seq_ends=seq_ends),
        grid=(nchunks,),
        in_specs=in_specs,
        out_specs=pl.BlockSpec((ch, SSD_WIDTH), lambda i: (rb(i), 0)),
        out_shape=jax.ShapeDtypeStruct((t, SSD_WIDTH), BF16 if reverse else F32),
        scratch_shapes=scratch,
        compiler_params=_params("arbitrary"),
        name="ssd_bwd_gate_norm" if reverse else "ssd_fwd",
    )(*args)


def _rope_kernel(x_ref, cos_ref, sin_ref, o_ref, *, head_dim, heads):
    cos = cos_ref[...]
    sin = sin_ref[...]
    half = head_dim // 2
    for h in range(heads):
        x = x_ref[:, h * head_dim:(h + 1) * head_dim].astype(F32)
        if half % 128 == 0:
            rot = jnp.concatenate([x[:, half:], x[:, :half]], axis=-1)
        else:
            rot = pltpu.roll(x, half, 1)
        o_ref[:, h * head_dim:(h + 1) * head_dim] = (x * cos + rot * sin).astype(o_ref.dtype)


def _rope(proj, col0, width, head_dim, cos, sin, seq_starts):
    t = proj.shape[0]
    tm = _row_tile(t, 512, seq_starts)
    tc = 1024
    assert col0 % tc == 0 and width % tc == 0
    cb = col0 // tc
    def pos_map(i, j):
        r = i * tm
        s = _seq_index(r, seq_starts)
        st = 0
        for k, v in enumerate(seq_starts):
            st = st + jnp.where(s == k, v, 0)
        return ((r - st) // tm, 0)

    return pl.pallas_call(
        functools.partial(_rope_kernel, head_dim=head_dim, heads=tc // head_dim),
        grid=(t // tm, width // tc),
        in_specs=[pl.BlockSpec((tm, tc), lambda i, j: (i, cb + j)),
                  pl.BlockSpec((tm, head_dim), pos_map),
                  pl.BlockSpec((tm, head_dim), pos_map)],
        out_specs=pl.BlockSpec((tm, tc), lambda i, j: (i, j)),
        out_shape=jax.ShapeDtypeStruct((t, width), BF16),
        compiler_params=_params("arbitrary", "arbitrary"),
        name="rope",
    )(proj, cos, sin)


def _rope_tables(lmax, head_dim, scale):
    half = head_dim // 2
    inv = 1.0 / (ROPE_THETA ** (jnp.arange(half, dtype=F32) / half))
    ang = jnp.arange(lmax, dtype=F32)[:, None] * inv[None, :]
    cos = jnp.cos(ang) * scale
    sin = jnp.sin(ang) * scale
    return jnp.concatenate([cos, cos], axis=-1), jnp.concatenate([-sin, sin], axis=-1)


def _attn_kernel(lam_ref, q_ref, k_ref, v_ref, g_ref, *rest, tk, nkv, strip, out_scale, aliased):
    if aliased:
        rest = rest[1:]
    o_ref, m_scr, l_scr, acc_scr, s_scr, p_scr, alpha_scr = rest
    dh = DA_HEAD_DIM
    tq = q_ref.shape[0]
    lanes = m_scr.shape[-1]
    m_scr[...] = jnp.full_like(m_scr, -jnp.inf)
    l_scr[...] = jnp.zeros_like(l_scr)
    acc_scr[...] = jnp.zeros_like(acc_scr)

    def body(j, carry):
        off = pl.multiple_of(j * tk, tk)
        for u in range(2):
            s_scr[u] = lax.dot_general(q_ref[:, u * dh:(u + 1) * dh], k_ref[pl.ds(off, tk), u * dh:(u + 1) * dh],
                                       (((1,), (1,)), ((), ())), preferred_element_type=F32)
        for u in range(2):
            for r in range(tq // strip):
                rows = slice(r * strip, (r + 1) * strip)
                s = s_scr[u, rows, :]
                m_prev = m_scr[u, rows, :]
                m_new = jnp.maximum(m_prev, jnp.max(s, axis=-1, keepdims=True))
                alpha = jnp.exp2(m_prev - m_new)
                p = jnp.exp2(s - pltpu.repeat(m_new, tk // lanes, 1))
                l_scr[u, rows, :] = alpha * l_scr[u, rows, :] + jnp.sum(p, axis=-1, keepdims=True)
                m_scr[u, rows, :] = m_new
                alpha_scr[u, rows, :] = alpha
                p_scr[u, rows, :] = p.astype(BF16)
            pv = jnp.dot(p_scr[u], v_ref[pl.ds(off, tk), :], preferred_element_type=F32)
            acc_scr[u] = acc_scr[u] * pltpu.repeat(alpha_scr[u], acc_scr.shape[-1] // lanes, 1) + pv
        return carry

    lax.fori_loop(0, nkv, body, 0, unroll=2 if nkv % 2 == 0 else 1)
    lam = lam_ref[0]
    rep =acc_scr.shape[-1] // lanes
    o = (acc_scr[0] / pltpu.repeat(l_scr[0], rep, 1)
         - lam * (acc_scr[1] / pltpu.repeat(l_scr[1], rep, 1)))
    ms = jnp.mean(o * o, axis=-1, keepdims=True)
    o_ref[...] = (o * lax.rsqrt(ms + SUBLN_EPS) * g_ref[...] * out_scale).astype(o_ref.dtype)


def _diff_attention(qr, kr, proj, vcol0, lam, subln_g, out_scale, groups, prior=None):
    t = qr.shape[0]
    row_start, batch, seqlen = groups
    pw = 2 * DA_HEAD_DIM
    tq = _tile(seqlen, 512)
    tk = _tile(seqlen, 1024)
    strip = 32
    lanes = 128
    nq = seqlen // tq
    assert row_start % seqlen == 0 and vcol0 % pw == 0
    qb0 = row_start // tq
    sb0 = row_start // seqlen
    vb0 = vcol0 // pw
    aliased = prior is not None
    in_specs = [pl.BlockSpec(memory_space=pltpu.SMEM),
                pl.BlockSpec((tq, pw), lambda b, h, i: (qb0 + b * nq + i, h)),
                pl.BlockSpec((seqlen, pw), lambda b, h, i: (sb0 + b, h)),
                pl.BlockSpec((seqlen, pw), lambda b, h, i: (sb0 + b, vb0 + h)),
                pl.BlockSpec((1, pw), lambda b, h, i: (0, 0))]
    args = [lam, qr, kr, proj, subln_g]
    if aliased:
        in_specs.append(pl.BlockSpec(memory_space=pl.ANY))
        args.append(prior)
    return pl.pallas_call(
        functools.partial(_attn_kernel, tk=tk, nkv=seqlen // tk, strip=strip, out_scale=out_scale,
                          aliased=aliased),
        grid=(batch, DA_HEADS, nq),
        in_specs=in_specs,
        out_specs=pl.BlockSpec((tq, pw), lambda b, h, i: (qb0 + b * nq + i, h)),
        out_shape=jax.ShapeDtypeStruct((t, DA_V_WIDTH), BF16),
        scratch_shapes=[pltpu.VMEM((2, tq, lanes), F32), pltpu.VMEM((2, tq, lanes), F32),
                        pltpu.VMEM((2, tq, pw), F32), pltpu.VMEM((2, tq, tk), F32), pltpu.VMEM((2, tq, tk), BF16),
                        pltpu.VMEM((2, tq, lanes), F32)],
        input_output_aliases={5: 0} if aliased else {},
        compiler_params=_params("arbitrary", "arbitrary", "arbitrary"),
        name="diff_attention",
    )(*args)


def _outproj_kernel(y1_ref, y2_ref, w1_ref, w2_ref, x_ref, g_ref, o_ref):
    acc = jnp.dot(y1_ref[...], w1_ref[...], preferred_element_type=F32)
    acc = acc + jnp.dot(y2_ref[...], w2_ref[...], preferred_element_type=F32)
    o_ref[...] = x_ref[...] + g_ref[...] * acc


def _outproj_residual(y1, c1, y2, c2, w, x, gate, seq_starts):
    t, d = x.shape
    kh = w.shape[0] // 2
    tm = _row_tile(t, 512, seq_starts)
    tn = _tile(d, 1024)
    return pl.pallas_call(
        _outproj_kernel,
        grid=(d // tn, t // tm),
        in_specs=[pl.BlockSpec((tm, kh), lambda j, i: (i, c1)),
                  pl.BlockSpec((tm, kh), lambda j, i: (i, c2)),
                  pl.BlockSpec((kh, tn), lambda j, i: (0, j)),
                  pl.BlockSpec((kh, tn), lambda j, i: (1, j)),
                  pl.BlockSpec((tm, tn), lambda j, i: (i, j)),
                  pl.BlockSpec((None, 1, tn), lambda j, i: (_seq_index(i * tm, seq_starts), 0, j))],
        out_specs=pl.BlockSpec((tm, tn), lambda j, i: (i, j)),
        out_shape=jax.ShapeDtypeStruct((t, d), F32),
        compiler_params=_params("arbitrary", "arbitrary"),
        name="outproj_residual",
    )(y1, y2, w, w, x, gate)


def _router_kernel(x_ref, a_ref, sh_ref, wr_ref, br_ref, h_ref, lg_ref):
    h = _norm_mod(x_ref[...], a_ref[...], sh_ref[...])
    h_ref[...] = h.astype(h_ref.dtype)
    lg_ref[...] = jnp.dot(h, wr_ref[...], preferred_element_type=F32, precision=HIGHEST) + br_ref[...]


def _norm_mod_router(x, a, sh, wr, br, seq_starts):
    t, d = x.shape
    tm = _row_tile(t, 512, seq_starts)

    def seq_map(i):
        return (_seq_index(i * tm, seq_starts), 0, 0)

    return pl.pallas_call(
        _router_kernel,
        grid=(t // tm,),
        in_specs=[pl.BlockSpec((tm, d), lambda i: (i, 0)),
                  pl.BlockSpec((None, 1, d), seq_map),
                  pl.BlockSpec((None, 1, d), seq_map),
                  pl.BlockSpec((d, ROUTER_PAD), lambda i: (0, 0)),
                  pl.BlockSpec((1, ROUTER_PAD), lambda i: (0, 0))],
        out_specs=[pl.BlockSpec((tm, d), lambda i: (i, 0)),
                   pl.BlockSpec((tm, ROUTER_PAD), lambda i: (i, 0))],
        out_shape=[jax.ShapeDtypeStruct((t, d), BF16), jax.ShapeDtypeStruct((t, ROUTER_PAD), F32)],
        compiler_params=_params("arbitrary"),
        name="norm_mod_router",
    )(x, a, sh, wr, br)


def _moe_ffn_kernel(be_ref, nu_ref, xs_ref, w1_ref, w3_ref, w2_ref, o_ref):
    del be_ref

    @pl.when(pl.program_id(0) < nu_ref[0])
    def _():
        x = xs_ref[...]
        h1 = jnp.dot(x, w1_ref[...], preferred_element_type=F32)
        h3 = jnp.dot(x, w3_ref[...], preferred_element_type=F32)
        hid = (_silu(h1) * h3).astype(BF16)
        o_ref[...] = jnp.dot(hid, w2_ref[...], preferred_element_type=F32).astype(o_ref.dtype)


def _moe_ffn(xs, block_expert, n_used, w1, w3, w2, layer):
    n_rows, d = xs.shape
    ff = w1.shape[3]
    blk = MOE_ROW_BLOCK
    n_blocks = n_rows // blk

    def row_map(i, be, nu):
        return (jnp.minimum(i, nu[0] - 1), 0)

    def w_map(i, be, nu):
        return (layer, be[i], 0, 0)

    grid_spec = pltpu.PrefetchScalarGridSpec(
        num_scalar_prefetch=2,
        grid=(n_blocks,),
        in_specs=[pl.BlockSpec((blk, d), row_map),
                  pl.BlockSpec((None, None, d, ff), w_map),
                  pl.BlockSpec((None, None, d, ff), w_map),
                  pl.BlockSpec((None, None, ff, d), w_map)],
        out_specs=pl.BlockSpec((blk, d), row_map))
    return pl.pallas_call(
        _moe_ffn_kernel,
        grid_spec=grid_spec,
        out_shape=jax.ShapeDtypeStruct((n_rows, d), BF16),
        compiler_params=_params("arbitrary"),
        name="moe_ffn",
    )(block_expert, n_used, xs, w1, w3, w2)


def _combine_kernel(x_ref, o0_ref, o1_ref, gt_ref, g_ref, *rest, final):
    if final:
        fg_ref, o_ref = rest
    else:
        (o_ref,) = rest
    gt = gt_ref[...]
    y = gt[:, 0:1] * o0_ref[...].astype(F32) + gt[:, 1:2] * o1_ref[...].astype(F32)
    x = x_ref[...] + g_ref[...] * y
    if final:
        ms = jnp.mean(x * x, axis=-1, keepdims=True)
        x = x * lax.rsqrt(ms + EPS) * fg_ref[...]
    o_ref[...] = x


def _moe_combine(x, o0, o1, gates, gate_mod, final_g, seq_starts):
    t, d = x.shape
    tm = _row_tile(t, 512, seq_starts)
    final = final_g is not None
    in_specs = [pl.BlockSpec((tm, d), lambda i: (i, 0)),
                pl.BlockSpec((tm, d), lambda i: (i, 0)),
                pl.BlockSpec((tm, d), lambda i: (i, 0)),
                pl.BlockSpec((tm, ROUTER_PAD), lambda i: (i, 0)),
                pl.BlockSpec((None, 1, d), lambda i: (_seq_index(i * tm, seq_starts), 0, 0))]
    args = [x, o0, o1, gates, gate_mod]
    if final:
        in_specs.append(pl.BlockSpec((1, d), lambda i: (0, 0)))
        args.append(final_g)
    return pl.pallas_call(
        functools.partial(_combine_kernel, final=final),
        grid=(t // tm,),
        in_specs=in_specs,
        out_specs=pl.BlockSpec((tm, d), lambda i: (i, 0)),
        out_shape=jax.ShapeDtypeStruct((t, d), F32),
        compiler_params=_params("arbitrary"),
        name="moe_combine_residual",
    )(*args)


def _ret_kernel(*refs, reverse, nchunks, seq_starts, seq_ends):
    rc = RET_CHUNK
    if reverse:
        (cd_ref, q_ref, kt_ref, v_ref, inner_ref, cross_ref, kvd_ref, of_ref, gate_ref, o_ref, st_scr) = refs
    else:
        (cd_ref, q_ref, kt_ref, v_ref, inner_ref, cross_ref, kvd_ref, o_ref, st_scr) = refs
    step = pl.program_id(0)
    if reverse:
        fresh = _is_any((nchunks - step) * rc, seq_ends)
    else:
        fresh = _is_any(step * rc, seq_starts)

    @pl.when(fresh)
    def _():
        st_scr[...] = jnp.zeros_like(st_scr)

    dk, dv = RET_QK_DIM, RET_V_DIM
    for h in range(RET_HEADS):
        q = q_ref[:, h * dk:(h + 1) * dk]
        kt = kt_ref[h * dk:(h + 1) * dk, :]
        v = v_ref[:, h * dv:(h + 1) * dv]
        s = jnp.dot(q, kt, preferred_element_type=F32) * inner_ref[h]
        st = st_scr[h]
        o = jnp.dot(s.astype(BF16), v, preferred_element_type=F32)
        o = o + jnp.dot(q, st.astype(BF16), preferred_element_type=F32) * cross_ref[h]
        ktd = (kt.astype(F32) * kvd_ref[h]).astype(BF16)
        st_scr[h] = st * cd_ref[h] + jnp.dot(ktd, v, preferred_element_type=F32)
        if reverse:
            o = o + of_ref[:, h * dv:(h + 1) * dv]
            ms = jnp.mean(o * o, axis=-1, keepdims=True)
            o = o * lax.rsqrt(ms + EPS) * _silu(gate_ref[:, h * dv:(h + 1) * dv].astype(F32))
        o_ref[:, h * dv:(h + 1) * dv] = o.astype(o_ref.dtype)


def _retention_direction(qr, krt, proj, vcol0, gcol0, log_decay, seq_starts, seq_ends, reverse, o_fwd=None):
    t = qr.shape[0]
    rc = _tile(min(s2 - s1 for s1, s2 in zip(seq_starts, seq_ends)), RET_CHUNK)
    assert rc == RET_CHUNK
    nchunks = t // rc
    pos = jnp.arange(rc, dtype=F32)
    diff = pos[:, None] - pos[None, :]
    lg = log_decay.astype(F32)
    if reverse:
        mask = diff < 0
        dist = -diff
        cross = jnp.exp((rc - pos)[None, :, None] * lg[:, None, None])
        kvd = jnp.exp(pos[None, None, :] * lg[:, None, None])
    else:
        mask = diff >= 0
        dist = diff
        cross = jnp.exp((pos + 1.0)[None, :, None] * lg[:, None, None])
        kvd = jnp.exp((rc - 1.0 - pos)[None, None, :] * lg[:, None, None])
    inner = jnp.where(mask[None], jnp.exp(jnp.where(mask, dist, 0.0)[None] * lg[:, None, None]), 0.0)
    cross = jnp.broadcast_to(cross, (RET_HEADS, rc, RET_V_DIM))
    kvd = jnp.broadcast_to(kvd, (RET_HEADS, RET_QK_DIM, rc))
    chunk_decay = jnp.exp(rc * lg)

    def rb(i):
        return nchunks - 1 - i if reverse else i

    vb = vcol0 // RET_V_WIDTH
    assert vcol0 % RET_V_WIDTH == 0 and gcol0 % RET_V_WIDTH == 0
    in_specs = [pl.BlockSpec(memory_space=pltpu.SMEM),
                pl.BlockSpec((rc, RET_QK_WIDTH), lambda i: (rb(i), 0)),
                pl.BlockSpec((RET_QK_WIDTH, rc), lambda i: (0, rb(i))),
                pl.BlockSpec((rc, RET_V_WIDTH), lambda i: (rb(i), vb)),
                pl.BlockSpec((RET_HEADS, rc, rc), lambda i: (0, 0, 0)),
                pl.BlockSpec((RET_HEADS, rc, RET_V_DIM), lambda i: (0, 0, 0)),
                pl.BlockSpec((RET_HEADS, RET_QK_DIM, rc), lambda i: (0, 0, 0))]
    args = [chunk_decay, qr, krt, proj, inner, cross, kvd]
    if reverse:
        gb = gcol0 // RET_V_WIDTH
        in_specs += [pl.BlockSpec((rc, RET_V_WIDTH), lambda i: (rb(i), 0)),
                     pl.BlockSpec((rc, RET_V_WIDTH), lambda i: (rb(i), gb))]
        args += [o_fwd, proj]
    return pl.pallas_call(
        functools.partial(_ret_kernel, reverse=reverse, nchunks=nchunks,
                          seq_starts=seq_starts, seq_ends=seq_ends),
        grid=(nchunks,),
        in_specs=in_specs,
        out_specs=pl.BlockSpec((rc, RET_V_WIDTH), lambda i: (rb(i), 0)),
        out_shape=jax.ShapeDtypeStruct((t, RET_V_WIDTH), BF16 if reverse else F32),
        scratch_shapes=[pltpu.VMEM((RET_HEADS, RET_QK_DIM, RET_V_DIM), F32)],
        compiler_params=_params("arbitrary"),
        name="retention_bwd_norm_gate" if reverse else "retention_fwd",
    )(*args)


def _ssd_diff_layer(x, a1, sh1, g1, p, i, layer, lay):
    seq_starts, seq_ends, groups = lay
    d = x.shape[1]
    w_in = p['ab_w_in'][i]
    o2 = SSD_WIDTH + SSD_XBC
    o3 = o2 + 2 * SSD_HEADS
    w_main = jnp.concatenate([w_in[:, :o2], w_in[:, o3:]], axis=1).astype(BF16)
    w_dt = jnp.pad(w_in[:, o2:o3], ((0, 0), (0, DT_PAD - 2 * SSD_HEADS))).astype(BF16)
    proj, dt = _norm_mod_matmul(x, a1, sh1, w_main, w_dt, seq_starts)
    qcol = o2
    kcol = qcol + DA_QK_WIDTH
    vcol = kcol + DA_QK_WIDTH

    xact = _conv_silu(proj, SSD_WIDTH, p['ssd_conv_w'][i], p['ssd_conv_b'][i], seq_starts, seq_ends)
    bt = xact[:, SSD_WIDTH:SSD_WIDTH + SSD_BC].T
    dtt = dt.T
    expand = (jnp.arange(SSD_WIDTH)[None, :] // SSD_HEAD_DIM == jnp.arange(SSD_HEADS)[:, None]).astype(F32)
    a_f = -jnp.exp(p['ssd_a_log_fwd'][i].astype(F32))
    a_b = -jnp.exp(p['ssd_a_log_bwd'][i].astype(F32))
    yf = _ssd_direction(xact, bt, dt, dtt, p['ssd_dt_bias_fwd'][i].astype(F32), a_f, expand,
                        seq_starts, seq_ends, reverse=False)
    dskip = jnp.repeat(p['ssd_d'][i].astype(F32), SSD_HEAD_DIM).reshape(1, SSD_WIDTH)
    y_ssd = _ssd_direction(xact, bt, dt, dtt, p['ssd_dt_bias_bwd'][i].astype(F32), a_b, expand,
                           seq_starts, seq_ends, reverse=True,
                           final=(proj, yf, dskip, p['ssd_norm_g'][i].astype(F32).reshape(1, SSD_WIDTH)))

    lam_init = 0.8 - 0.6 * math.exp(-0.3 * layer)
    lam = (jnp.exp(jnp.sum(p['da_lambda_q1'][i].astype(F32) * p['da_lambda_k1'][i].astype(F32)))
           - jnp.exp(jnp.sum(p['da_lambda_q2'][i].astype(F32) * p['da_lambda_k2'][i].astype(F32)))
           + lam_init).reshape(1)
    lmax = max(g[2] for g in groups)
    cos_q, sin_q = _rope_tables(lmax, DA_HEAD_DIM, DA_HEAD_DIM ** -0.5 * math.log2(math.e))
    cos_k, sin_k = _rope_tables(lmax, DA_HEAD_DIM, 1.0)
    qr = _rope(proj, qcol, DA_QK_WIDTH, DA_HEAD_DIM, cos_q, sin_q, seq_starts)
    kr = _rope(proj, kcol, DA_QK_WIDTH, DA_HEAD_DIM, cos_k, sin_k, seq_starts)
    subln = p['da_subln_g'][i].astype(F32).reshape(1, 2 * DA_HEAD_DIM)
    o = None
    for grp in groups:
        o = _diff_attention(qr, kr, proj, vcol, lam, subln, 1.0 - lam_init, grp, prior=o)

    w_out = p['ab_w_out'][i].astype(BF16)
    return _outproj_residual(y_ssd, 0, o, 0, w_out, x, g1, seq_starts)


def _retention_layer(x, a1, sh1, g1, p, j, lay):
    seq_starts, seq_ends, groups = lay
    w_in = p['ret_w_in'][j].astype(BF16)
    proj = _norm_mod_matmul(x, a1, sh1, w_in, None, seq_starts)
    lmax = max(g[2] for g in groups)
    cos_q, sin_q = _rope_tables(lmax, RET_QK_DIM, 1.0)
    cos_k, sin_k = _rope_tables(lmax, RET_QK_DIM, RET_QK_DIM ** -0.5)
    qr = _rope(proj, 0, RET_QK_WIDTH, RET_QK_DIM, cos_q, sin_q, seq_starts)
    kr = _rope(proj, RET_QK_WIDTH, RET_QK_WIDTH, RET_QK_DIM, cos_k, sin_k, seq_starts)
    krt = kr.T
    vcol = 2 * RET_QK_WIDTH
    gcol = vcol + RET_V_WIDTH
    lg_f = -jnp.exp(p['ret_log_decay_fwd'][j].astype(F32))
    lg_b = -jnp.exp(p['ret_log_decay_bwd'][j].astype(F32))
    of = _retention_direction(qr, krt, proj, vcol, gcol, lg_f, seq_starts, seq_ends, reverse=False)
    o = _retention_direction(qr, krt, proj, vcol, gcol, lg_b, seq_starts, seq_ends, reverse=True, o_fwd=of)
    w_out = p['ret_w_out'][j].astype(BF16)
    return _outproj_residual(o, 0, o, 1, w_out, x, g1, seq_starts)


def _route(logits):
    coarse = logits[:, :MOE_GROUPS]
    grp = jnp.argmax(coarse, axis=-1)
    p_grp = jnp.max(jax.nn.softmax(coarse, axis=-1), axis=-1)
    fine = logits[:, MOE_GROUPS:MOE_GROUPS + MOE_EXPERTS].reshape(-1, MOE_GROUPS, MOE_EXPERTS_PER_GROUP)
    fine = jnp.take_along_axis(fine, grp[:, None, None], axis=1)[:, 0]
    top_v, top_i = lax.top_k(fine, MOE_TOP_K)
    gates = jax.nn.softmax(top_v, axis=-1) * p_grp[:, None]
    experts = (grp[:, None] * MOE_EXPERTS_PER_GROUP + top_i).astype(jnp.int32)
    return experts, gates


def _moe_layer(x, a2, sh2, g2, p, layer, lay, final_g):
    seq_starts, _, _ = lay
    t, d = x.shape
    wr = jnp.concatenate([p['moe_w_group'][layer], p['moe_w_expert'][layer]], axis=1).astype(F32)
    br = jnp.concatenate([p['moe_b_group'][layer], p['moe_b_expert'][layer]]).astype(F32)
    nr = wr.shape[1]
    wr = jnp.pad(wr, ((0, 0), (0, ROUTER_PAD - nr)))
    br = jnp.pad(br, (0, ROUTER_PAD - nr)).reshape(1, ROUTER_PAD)
    h, logits = _norm_mod_router(x, a2, sh2, wr, br, seq_starts)
    experts, gates = _route(logits)

    blk = MOE_ROW_BLOCK
    n_assign = t * MOE_TOP_K
    e_flat = experts.reshape(n_assign)
    onehot = (e_flat[:, None] == jnp.arange(MOE_EXPERTS, dtype=jnp.int32)[None, :]).astype(jnp.int32)
    csum = jnp.cumsum(onehot, axis=0)
    counts = csum[-1]
    rank = jnp.sum(csum * onehot, axis=1) - 1
    padded = (counts + blk - 1) // blk * blk
    ends_pad = jnp.cumsum(padded)
    start_pad = ends_pad - padded
    dest = (start_pad[e_flat] + rank).astype(jnp.int32)
    n_rows = n_assign + MOE_EXPERTS * blk
    n_blocks = n_rows // blk
    tok_flat = jnp.repeat(jnp.arange(t, dtype=jnp.int32), MOE_TOP_K)
    row_tok = (jnp.arange(n_rows, dtype=jnp.int32) % t).at[dest].set(tok_flat)
    block_expert = jnp.minimum(
        jnp.searchsorted(ends_pad, jnp.arange(n_blocks, dtype=jnp.int32) * blk, side='right'),
        MOE_EXPERTS - 1).astype(jnp.int32)
    n_used = (ends_pad[-1] // blk).astype(jnp.int32).reshape(1)

    xs = jnp.take(h, row_tok, axis=0, mode='clip')
    out = _moe_ffn(xs, block_expert, n_used, p['moe_w1'].astype(BF16),
                   p['moe_w3'].astype(BF16), p['moe_w2'].astype(BF16), layer)
    dest2 = dest.reshape(t, MOE_TOP_K)
    o0 = jnp.take(out, dest2[:, 0], axis=0, mode='clip')
    o1 = jnp.take(out, dest2[:, 1], axis=0, mode='clip')
    gates_pad = jnp.pad(gates, ((0, 0), (0, ROUTER_PAD - MOE_TOP_K)))
    return _moe_combine(x, o0, o1, gates_pad, g2, final_g, seq_starts)


def kernel(x_prompt, x_sample, c_prompt, c_sample, ada_w, ada_b, norm_mix_g, norm_ffn_g, ab_w_in, ssd_conv_w, ssd_conv_b, ssd_a_log_fwd, ssd_a_log_bwd, ssd_dt_bias_fwd, ssd_dt_bias_bwd, ssd_d, ssd_norm_g, da_lambda_q1, da_lambda_k1, da_lambda_q2, da_lambda_k2, da_subln_g, ab_w_out, ret_w_in, ret_log_decay_fwd, ret_log_decay_bwd, ret_w_out, moe_w_group, moe_b_group, moe_w_expert, moe_b_expert, moe_w1, moe_w3, moe_w2, final_norm_g):
    p = dict(ab_w_in=ab_w_in, ssd_conv_w=ssd_conv_w, ssd_conv_b=ssd_conv_b,
             ssd_a_log_fwd=ssd_a_log_fwd, ssd_a_log_bwd=ssd_a_log_bwd,
             ssd_dt_bias_fwd=ssd_dt_bias_fwd, ssd_dt_bias_bwd=ssd_dt_bias_bwd, ssd_d=ssd_d,
             ssd_norm_g=ssd_norm_g, da_lambda_q1=da_lambda_q1, da_lambda_k1=da_lambda_k1,
             da_lambda_q2=da_lambda_q2, da_lambda_k2=da_lambda_k2, da_subln_g=da_subln_g,
             ab_w_out=ab_w_out, ret_w_in=ret_w_in, ret_log_decay_fwd=ret_log_decay_fwd,
             ret_log_decay_bwd=ret_log_decay_bwd, ret_w_out=ret_w_out, moe_w_group=moe_w_group,
             moe_b_group=moe_b_group, moe_w_expert=moe_w_expert, moe_b_expert=moe_b_expert,
             moe_w1=moe_w1, moe_w3=moe_w3, moe_w2=moe_w2)
    bp, lp, d = x_prompt.shape
    bs, ls, _ = x_sample.shape
    depth = ada_w.shape[0]
    groups = ((0, bp, lp), (bp * lp, bs, ls))
    seq_starts = tuple(r0 + b * l for (r0, nb, l) in groups for b in range(nb))
    seq_ends = tuple(r0 + (b + 1) * l for (r0, nb, l) in groups for b in range(nb))
    lay = (seq_starts, seq_ends, groups)
    nseq = len(seq_starts)
    x = jnp.concatenate([x_prompt.reshape(bp * lp, d), x_sample.reshape(bs * ls, d)], axis=0)
    c = jnp.concatenate([c_prompt, c_sample], axis=0).astype(F32)
    c_pad = jnp.pad(c, ((0, -nseq % 8), (0, 0)))
    mod = _ada_modulation(c_pad, ada_w.astype(F32), ada_b.astype(F32))[:, :nseq]

    for layer in range(depth):
        sh1, sc1, g1, sh2, sc2, g2 = [m.reshape(nseq, 1, d) for m in jnp.split(mod[layer], 6, axis=-1)]
        a1 = norm_mix_g[layer].astype(F32)[None, None, :] * (1.0 + sc1)
        a2 = norm_ffn_g[layer].astype(F32)[None, None, :] * (1.0 + sc2)
        if layer % 2 == 0:
            x = _ssd_diff_layer(x, a1, sh1, g1, p, layer // 2, layer, lay)
        else:
            x = _retention_layer(x, a1, sh1, g1, p, layer // 2, lay)
        final_g = final_norm_g.astype(F32).reshape(1, d) if layer == depth - 1 else None
        x = _moe_layer(x, a2, sh2, g2, p, layer, lay, final_g)
    y_prompt = x[:bp * lp].reshape(bp, lp, d)
    y_sample = x[bp * lp:].reshape(bs, ls, d)
    return (y_prompt, y_sample)
```

```python
import functools
import math

import jax
import jax.numpy as jnp
from jax import lax
from jax.experimental import pallas as pl
from jax.experimental.pallas import tpu as pltpu

F32 = jnp.float32
BF16 = jnp.bfloat16
HIGHEST = lax.Precision.HIGHEST

EPS = 1e-6
SUBLN_EPS = 1e-5
ROPE_THETA = 10000.0

SSD_HEADS = 32
SSD_HEAD_DIM = 64
SSD_WIDTH = SSD_HEADS * SSD_HEAD_DIM
SSD_GROUPS = 4
SSD_STATE = 128
SSD_CONV = 5
SSD_BC = SSD_GROUPS * SSD_STATE
SSD_XBC = SSD_WIDTH + 2 * SSD_BC
SSD_CHUNK = 128
DA_HEADS = 8
DA_HEAD_DIM = 128
DA_QK_WIDTH = 2 * DA_HEADS * DA_HEAD_DIM
DA_V_WIDTH = DA_HEADS * 2 * DA_HEAD_DIM
RET_HEADS = 8
RET_QK_DIM = 256
RET_V_DIM = 512
RET_QK_WIDTH = RET_HEADS * RET_QK_DIM
RET_V_WIDTH = RET_HEADS * RET_V_DIM
RET_CHUNK = 256
MOE_GROUPS = 4
MOE_EXPERTS_PER_GROUP = 8
MOE_EXPERTS = MOE_GROUPS * MOE_EXPERTS_PER_GROUP
MOE_TOP_K = 2
MOE_ROW_BLOCK = 512
MOE_FF_SPLIT = 2
ROUTER_PAD = 128
DT_PAD = 128
HALO = 16

VMEM_LIMIT_BYTES = 56 * 1024 * 1024


def _params(*semantics):
    return pltpu.CompilerParams(dimension_semantics=semantics, vmem_limit_bytes=VMEM_LIMIT_BYTES)


def _silu(x):
    return x * jax.nn.sigmoid(x)


def _softplus(x):
    return jnp.maximum(x, 0.0) + jnp.log1p(jnp.exp(-jnp.abs(x)))


def _lane_repeat(x, n):
    return x if n == 1 else jnp.concatenate([x] * n, axis=1)


def _tile(n, pref):
    t = min(n, pref)
    assert n % t == 0, (n, pref)
    return t


def _row_tile(t, pref, seq_starts):
    tm = min(t, pref)
    while t % tm or any(st % tm for st in seq_starts):
        tm //= 2
    return tm


def _seq_index(row, seq_starts):
    s = 0
    for st in seq_starts[1:]:
        s = s + jnp.where(row >= st, 1, 0)
    return s


def _is_any(row, values):
    hit = row == values[0]
    for v in values[1:]:
        hit = jnp.logical_or(hit, row == v)
    return hit


def _ada_kernel(c_ref, w_ref, b_ref, o_ref):
    o_ref[...] = jnp.dot(_silu(c_ref[...]), w_ref[...], preferred_element_type=F32,
                         precision=HIGHEST) + b_ref[...]


def _ada_modulation(c_pad, ada_w, ada_b):
    depth, d, n = ada_w.shape
    rows = c_pad.shape[0]
    tn = _tile(n, 1024)
    return pl.pallas_call(
        _ada_kernel,
        grid=(depth, n // tn),
        in_specs=[pl.BlockSpec((rows, d), lambda l, j: (0, 0)),
                  pl.BlockSpec((None, d, tn), lambda l, j: (l, 0, j)),
                  pl.BlockSpec((None, 1, tn), lambda l, j: (l, 0, j))],
        out_specs=pl.BlockSpec((None, rows, tn), lambda l, j: (l, 0, j)),
        out_shape=jax.ShapeDtypeStruct((depth, rows, n), F32),
        compiler_params=_params("arbitrary", "arbitrary"),
        name="ada_modulation",
    )(c_pad, ada_w, ada_b.reshape(depth, 1, n))


def _norm_mod(x, a, sh):
    ms = jnp.mean(x * x, axis=-1, keepdims=True)
    return (x * lax.rsqrt(ms + EPS)) * a + sh


def _rotate_half_tile(x, cos, sin, head_dim):
    half = head_dim // 2
    outs = []
    for h in range(x.shape[1] // head_dim):
        xh = x[:, h * head_dim:(h + 1) * head_dim]
        if half % 128 == 0:
            rot = jnp.concatenate([xh[:, half:], xh[:, :half]], axis=-1)
        else:
            rot = pltpu.roll(xh, half, 1)
        outs.append(xh * cos + rot * sin)
    return jnp.concatenate(outs, axis=-1)


def _nmm_kernel(*refs, with_dt, rope_tiles, head_dim):
    x_ref, a_ref, sh_ref, w_ref, cq_ref, sq_ref, ck_ref, sk_ref = refs[:8]
    if with_dt:
        wdt_ref, o_ref, odt_ref, h_scr = refs[8:]
    else:
        o_ref, h_scr = refs[8:]
    j = pl.program_id(1)

    @pl.when(j == 0)
    def _():
        hb = _norm_mod(x_ref[...], a_ref[...], sh_ref[...]).astype(BF16)
        h_scr[...] = hb
        if with_dt:
            odt_ref[...] = jnp.dot(hb, wdt_ref[...], preferred_element_type=F32)

    acc = jnp.dot(h_scr[...], w_ref[...], preferred_element_type=F32)
    (q_lo, q_hi), (k_lo, k_hi) = rope_tiles
    is_q = jnp.logical_and(j >= q_lo, j < q_hi)
    is_k = jnp.logical_and(j >= k_lo, j < k_hi)

    @pl.when(is_q)
    def _():
        o_ref[...] = _rotate_half_tile(acc, cq_ref[...], sq_ref[...], head_dim).astype(o_ref.dtype)

    @pl.when(is_k)
    def _():
        o_ref[...] = _rotate_half_tile(acc, ck_ref[...], sk_ref[...], head_dim).astype(o_ref.dtype)

    @pl.when(jnp.logical_not(jnp.logical_or(is_q, is_k)))
    def _():
        o_ref[...] = acc.astype(o_ref.dtype)


def _norm_mod_matmul(x, a, sh, w, wdt, seq_starts, rope_tiles, head_dim, tables):
    t, d = x.shape
    n = w.shape[1]
    tm = _row_tile(t, 1024, seq_starts)
    tn = _tile(n, 1024)
    with_dt = wdt is not None

    def seq_map(i, j):
        return (_seq_index(i * tm, seq_starts), 0, 0)

    def pos_map(i, j):
        r = i * tm
        s = _seq_index(r, seq_starts)
        st = 0
        for k, v in enumerate(seq_starts):
            st = st + jnp.where(s == k, v, 0)
        return ((r - st) // tm, 0)

    in_specs = [pl.BlockSpec((tm, d), lambda i, j: (i, 0)),
                pl.BlockSpec((None, 1, d), seq_map),
                pl.BlockSpec((None, 1, d), seq_map),
                pl.BlockSpec((d, tn), lambda i, j: (0, j))] + [pl.BlockSpec((tm, head_dim), pos_map)] * 4
    out_specs = [pl.BlockSpec((tm, tn), lambda i, j: (i, j))]
    out_shape = [jax.ShapeDtypeStruct((t, n), BF16)]
    args = [x, a, sh, w, *tables]
    if with_dt:
        in_specs.append(pl.BlockSpec((d, DT_PAD), lambda i, j: (0, 0)))
        out_specs.append(pl.BlockSpec((tm, DT_PAD), lambda i, j: (i, 0)))
        out_shape.append(jax.ShapeDtypeStruct((t, DT_PAD), F32))
        args.append(wdt)
    res = pl.pallas_call(
        functools.partial(_nmm_kernel, with_dt=with_dt, rope_tiles=rope_tiles, head_dim=head_dim),
        grid=(t // tm, n // tn),
        in_specs=in_specs, out_specs=out_specs, out_shape=out_shape,
        scratch_shapes=[pltpu.VMEM((tm, d), BF16)],
        compiler_params=_params("arbitrary", "arbitrary"),
        name="norm_mod_matmul",
    )(*args)
    return res if with_dt else res[0]


def _conv_kernel(cur_ref, prev_ref, next_ref, w_ref, b_ref, o_ref, ext_scr, *, tm, seq_starts, seq_ends):
    r0 = pl.program_id(0) * tm
    at_start = _is_any(r0, seq_starts)
    at_end = _is_any(r0 + tm, seq_ends)
    prev = prev_ref[...].astype(F32)[HALO - 8:HALO]
    nxt = next_ref[...].astype(F32)[0:8]
    ext_scr[0:8, :] = jnp.where(at_start, 0.0, prev)
    ext_scr[8:8 + tm, :] = cur_ref[...].astype(F32)
    ext_scr[8 + tm:16 + tm, :] = jnp.where(at_end, 0.0, nxt)
    pad = SSD_CONV // 2
    acc = b_ref[...] + w_ref[0:1, :] * ext_scr[8 - pad:8 - pad + tm, :]
    for k in range(1, SSD_CONV):
        acc = acc + w_ref[k:k + 1, :] * ext_scr[8 - pad + k:8 - pad + k + tm, :]
    o_ref[...] = _silu(acc).astype(o_ref.dtype)


def _conv_silu(proj, col0, conv_w, conv_b, seq_starts, seq_ends):
    t = proj.shape[0]
    tm = _row_tile(t, 512, seq_starts)
    tc = 1024
    assert col0 % tc == 0 and SSD_XBC % tc == 0 and tm % HALO == 0
    cb = col0 // tc
    hb = tm // HALO
    last_halo = t // HALO - 1
    return pl.pallas_call(
        functools.partial(_conv_kernel, tm=tm, seq_starts=seq_starts, seq_ends=seq_ends),
        grid=(t // tm, SSD_XBC // tc),
        in_specs=[pl.BlockSpec((tm, tc), lambda i, j: (i, cb + j)),
                  pl.BlockSpec((HALO, tc), lambda i, j: (jnp.maximum(i * hb - 1, 0), cb + j)),
                  pl.BlockSpec((HALO, tc), lambda i, j: (jnp.minimum((i + 1) * hb, last_halo), cb + j)),
                  pl.BlockSpec((SSD_CONV, tc), lambda i, j: (0, j)),
                  pl.BlockSpec((1, tc), lambda i, j: (0, j))],
        out_specs=pl.BlockSpec((tm, tc), lambda i, j: (i, j)),
        out_shape=jax.ShapeDtypeStruct((t, SSD_XBC), BF16),
        scratch_shapes=[pltpu.VMEM((tm + 16, tc), F32)],
        compiler_params=_params("arbitrary", "arbitrary"),
        name="ssd_conv_silu",
    )(proj, proj, proj, conv_w, conv_b.reshape(1, SSD_XBC))


def _ssd_kernel(*refs, reverse, dcol, nchunks, seq_starts, seq_ends):
    ch = SSD_CHUNK
    if reverse:
        (xact_ref, bt_ref, dt_ref, dtt_ref, bias_ref, a_ref, biast_ref, at_ref, e_ref,
         z_ref, yf_ref, dskip_ref, g_ref, o_ref, h_scr, y_scr) = refs
    else:
        (xact_ref, bt_ref, dt_ref, dtt_ref, bias_ref, a_ref, biast_ref, at_ref, e_ref,
         o_ref, h_scr) = refs
    step = pl.program_id(0)
    if reverse:
        row_end = (nchunks - step) * ch
        fresh = _is_any(row_end, seq_ends)
    else:
        fresh = _is_any(step * ch, seq_starts)

    @pl.when(fresh)
    def _():
        h_scr[...] = jnp.zeros_like(h_scr)

    dt = _softplus(dt_ref[:, dcol:dcol + SSD_HEADS] + bias_ref[...])
    dta = dt * a_ref[...]
    dtt = _softplus(dtt_ref[dcol:dcol + SSD_HEADS, :] + biast_ref[...])
    dtat = dtt * at_ref[...]
    row = lax.broadcasted_iota(jnp.int32, (ch, ch), 0)
    col = lax.broadcasted_iota(jnp.int32, (ch, ch), 1)
    if reverse:
        keep = col >= row
    else:
        keep = col <= row
    tri = jnp.where(keep, 1.0, 0.0).astype(F32)
    trit = jnp.where(keep, 0.0, 1.0).astype(F32) + jnp.where(row == col, 1.0, 0.0).astype(F32)
    cs = jnp.dot(tri, dta, preferred_element_type=F32, precision=HIGHEST)
    cst = jnp.dot(dtat, trit, preferred_element_type=F32, precision=HIGHEST)
    last = 0 if reverse else ch - 1
    ecs = jnp.exp(cs)
    wt = dtt * jnp.exp(cst[:, last:last + 1] - cst)
    cdec_x = jnp.dot(jnp.broadcast_to(ecs[last:last + 1, :], (8, SSD_HEADS)), e_ref[...],
                     preferred_element_type=F32, precision=HIGHEST)[0:1, :]

    lane = lax.broadcasted_iota(jnp.int32, (1, 2 * SSD_HEAD_DIM), 1)
    first = lane < SSD_HEAD_DIM
    gw = SSD_WIDTH // SSD_GROUPS
    hpg = SSD_HEADS // SSD_GROUPS
    pw = 2 * SSD_HEAD_DIM
    ssq = jnp.zeros((ch, pw), F32)
    for g in range(SSD_GROUPS):
        bgt = bt_ref[g * SSD_STATE:(g + 1) * SSD_STATE, :]
        c0 = SSD_WIDTH + SSD_BC + g * SSD_STATE
        cg = xact_ref[:, c0:c0 + SSD_STATE]
        scores = jnp.dot(cg, bgt, preferred_element_type=F32)
        cgf = cg.astype(F32)
        bgtf = bgt.astype(F32)
        for pr in range(hpg // 2):
            lo = g * gw + pr * pw
            hi = lo + pw
            xpair = xact_ref[:, lo:hi].astype(F32)
            hpair = h_scr[g, :, pr * pw:(pr + 1) * pw]
            xsel = (jnp.where(first, xpair, 0.0).astype(BF16), jnp.where(first, 0.0, xpair).astype(BF16))
            hsel = (jnp.where(first, hpair, 0.0).astype(BF16), jnp.where(first, 0.0, hpair).astype(BF16))
            y = None
            bws = []
            for u in range(2):
                h = g * hpg + 2 * pr + u
                csb = jnp.broadcast_to(cs[:, h:h + 1], (ch, ch))
                seg = jnp.where(keep, csb - cst[h:h + 1, :], -jnp.inf)
                m = (scores * jnp.exp(seg) * dtt[h:h + 1, :]).astype(BF16)
                ce = (cgf * jnp.exp(csb)).astype(BF16)
                part = jnp.dot(jnp.concatenate([m, ce], axis=1), jnp.concatenate([xsel[u], hsel[u]], axis=0),
                               preferred_element_type=F32)
                y = part if y is None else y + part
                bws.append((bgtf * wt[h:h + 1, :]).astype(BF16))
            h_scr[g, :, pr * pw:(pr + 1) * pw] = hpair * cdec_x[:, lo:hi] + jnp.dot(
                jnp.concatenate(bws, axis=1), jnp.concatenate(xsel, axis=0), preferred_element_type=F32)
            if reverse:
                y = y + yf_ref[:, lo:hi] + dskip_ref[:, lo:hi] * xpair
                y = y * _silu(z_ref[:, lo:hi].astype(F32))
                ssq = ssq + y * y
                y_scr[:, lo:hi] = y
            else:
                o_ref[:, lo:hi] = y
    if reverse:
        inv = lax.rsqrt(jnp.sum(ssq, axis=-1, keepdims=True) * (1.0 / SSD_WIDTH) + EPS)
        o_ref[...] = (y_scr[...] * inv * g_ref[...]).astype(o_ref.dtype)


def _ssd_direction(xact, bt, dt, dtt, bias, a, expand, seq_starts, seq_ends, reverse, final=None):
    t = xact.shape[0]
    ch = SSD_CHUNK
    nchunks = t // ch
    dcol = SSD_HEADS if reverse else 0

    def rb(i):
        return nchunks - 1 - i if reverse else i

    in_specs = [pl.BlockSpec((ch, SSD_XBC), lambda i: (rb(i), 0)),
                pl.BlockSpec((SSD_BC, ch), lambda i: (0, rb(i))),
                pl.BlockSpec((ch, DT_PAD), lambda i: (rb(i), 0)),
                pl.BlockSpec((DT_PAD, ch), lambda i: (0, rb(i))),
                pl.BlockSpec((1, SSD_HEADS), lambda i: (0, 0)),
                pl.BlockSpec((1, SSD_HEADS), lambda i: (0, 0)),
                pl.BlockSpec((SSD_HEADS, 1), lambda i: (0, 0)),
                pl.BlockSpec((SSD_HEADS, 1), lambda i: (0, 0)),
                pl.BlockSpec((SSD_HEADS, SSD_WIDTH), lambda i: (0, 0))]
    args = [xact, bt, dt, dtt, bias.reshape(1, -1), a.reshape(1, -1), bias.reshape(-1, 1), a.reshape(-1, 1), expand]
    scratch = [pltpu.VMEM((SSD_GROUPS, SSD_STATE, SSD_WIDTH // SSD_GROUPS), F32)]
    if reverse:
        proj, yf, dskip, g = final
        in_specs += [pl.BlockSpec((ch, SSD_WIDTH), lambda i: (rb(i), 0)),
                     pl.BlockSpec((ch, SSD_WIDTH), lambda i: (rb(i), 0)),
                     pl.BlockSpec((1, SSD_WIDTH), lambda i: (0, 0)),
                     pl.BlockSpec((1, SSD_WIDTH), lambda i: (0, 0))]
        args += [proj, yf, dskip, g]
        scratch.append(pltpu.VMEM((ch, SSD_WIDTH), F32))
    return pl.pallas_call(
        functools.partial(_ssd_kernel, reverse=reverse, dcol=dcol, nchunks=nchunks,
                          seq_starts=seq_starts, seq_ends=seq_ends),
        grid=(nchunks,),
        in_specs=in_specs,
        out_specs=pl.BlockSpec((ch, SSD_WIDTH), lambda i: (rb(i), 0)),
        out_shape=jax.ShapeDtypeStruct((t, SSD_WIDTH), BF16 if reverse else F32),
        scratch_shapes=scratch,
        compiler_params=_params("arbitrary"),
        name="ssd_bwd_gate_norm" if reverse else "ssd_fwd",
    )(*args)


def _rope_tables(lmax, head_dim, scale):
    half = head_dim // 2
    inv = 1.0 / (ROPE_THETA ** (jnp.arange(half, dtype=F32) / half))
    ang = jnp.arange(lmax, dtype=F32)[:, None] * inv[None, :]
    cos = jnp.cos(ang) * scale
    sin = jnp.sin(ang) * scale
    return jnp.concatenate([cos, cos], axis=-1), jnp.concatenate([-sin, sin], axis=-1)


def _attn_kernel(lam_ref, q_ref, k_ref, v_ref, g_ref, *rest, tk, nkv, strip, out_scale, aliased):
    if aliased:
        rest = rest[1:]
    o_ref, m_scr, l_scr, acc_scr, s_scr, p_scr, alpha_scr = rest
    dh = DA_HEAD_DIM
    tq = q_ref.shape[0]
    lanes = m_scr.shape[-1]
    m_scr[...] = jnp.full_like(m_scr, -jnp.inf)
    l_scr[...] = jnp.zeros_like(l_scr)
    acc_scr[...] = jnp.zeros_like(acc_scr)

    def body(j, carry):
        off = pl.multiple_of(j * tk, tk)
        for u in range(2):
            s_scr[u] = lax.dot_general(q_ref[:, u * dh:(u + 1) * dh], k_ref[pl.ds(off, tk), u * dh:(u + 1) * dh],
                                       (((1,), (1,)), ((), ())), preferred_element_type=F32)
        for u in range(2):
            for r in range(tq // strip):
                rows = slice(r * strip, (r + 1) * strip)
                s = s_scr[u, rows, :]
                m_prev = m_scr[u, rows, :]
                m_new = jnp.maximum(m_prev, jnp.max(s, axis=-1, keepdims=True))
                alpha = jnp.exp2(m_prev - m_new)
                p = jnp.exp2(s - _lane_repeat(m_new, tk // lanes))
                l_scr[u, rows, :] = alpha * l_scr[u, rows, :] + jnp.sum(p, axis=-1, keepdims=True)
                m_scr[u, rows, :] = m_new
                alpha_scr[u, rows, :] = alpha
                p_scr[u, rows, :] = p.astype(BF16)
            pv = jnp.dot(p_scr[u], v_ref[pl.ds(off, tk), :], preferred_element_type=F32)
            acc_scr[u] = acc_scr[u] * _lane_repeat(alpha_scr[u], acc_scr.shape[-1] // lanes) + pv
        return carry

    lax.fori_loop(0, nkv, body, 0, unroll=2 if nkv % 2 == 0 else 1)
    lam = lam_ref[0]
    rep = acc_scr.shape[-1] // lanes
    o = (acc_scr[0] / _lane_repeat(l_scr[0], rep)
         - lam * (acc_scr[1] / _lane_repeat(l_scr[1], rep)))
    ms = jnp.mean(o * o, axis=-1, keepdims=True)
    o_ref[...] = (o * lax.rsqrt(ms + SUBLN_EPS) * g_ref[...] * out_scale).astype(o_ref.dtype)


def _diff_attention(proj, qcol0, kcol0, vcol0, lam, subln_g, out_scale, groups, prior=None):
    t = proj.shape[0]
    row_start, batch, seqlen = groups
    pw = 2 * DA_HEAD_DIM
    tq = _tile(seqlen, 512)
    tk = _tile(seqlen, 1024)
    strip = 32
    lanes = 128
    nq = seqlen // tq
    assert row_start % seqlen == 0 and vcol0 % pw == 0 and qcol0 % pw == 0 and kcol0 % pw == 0
    qb0 = row_start // tq
    sb0 = row_start // seqlen
    qc0, kc0, vb0 = qcol0 // pw, kcol0 // pw, vcol0 // pw
    aliased = prior is not None
    in_specs = [pl.BlockSpec(memory_space=pltpu.SMEM),
                pl.BlockSpec((tq, pw), lambda b, h, i: (qb0 + b * nq + i, qc0 + h)),
                pl.BlockSpec((seqlen, pw), lambda b, h, i: (sb0 + b, kc0 + h)),
                pl.BlockSpec((seqlen, pw), lambda b, h, i: (sb0 + b, vb0 + h)),
                pl.BlockSpec((1, pw), lambda b, h, i: (0, 0))]
    args = [lam, proj, proj, proj, subln_g]
    if aliased:
        in_specs.append(pl.BlockSpec(memory_space=pl.ANY))
        args.append(prior)
    return pl.pallas_call(
        functools.partial(_attn_kernel, tk=tk, nkv=seqlen // tk, strip=strip, out_scale=out_scale,
                          aliased=aliased),
        grid=(batch, DA_HEADS, nq),
        in_specs=in_specs,
        out_specs=pl.BlockSpec((tq, pw), lambda b, h, i: (qb0 + b * nq + i, h)),
        out_shape=jax.ShapeDtypeStruct((t, DA_V_WIDTH), BF16),
        scratch_shapes=[pltpu.VMEM((2, tq, lanes), F32), pltpu.VMEM((2, tq, lanes), F32),
                        pltpu.VMEM((2, tq, pw), F32), pltpu.VMEM((2, tq, tk), F32), pltpu.VMEM((2, tq, tk), BF16),
                        pltpu.VMEM((2, tq, lanes), F32)],
        input_output_aliases={5: 0} if aliased else {},
        compiler_params=_params("arbitrary", "arbitrary", "arbitrary"),
        name="diff_attention",
    )(*args)


def _outproj_kernel(y1_ref, y2_ref, w1_ref, w2_ref, x_ref, g_ref, o_ref):
    acc = jnp.dot(y1_ref[...], w1_ref[...], preferred_element_type=F32)
    acc = acc + jnp.dot(y2_ref[...], w2_ref[...], preferred_element_type=F32)
    o_ref[...] = x_ref[...] + g_ref[...] * acc


def _outproj_residual(y1, c1, y2, c2, w, x, gate, seq_starts):
    t, d = x.shape
    kh = w.shape[0] // 2
    tm = _row_tile(t, 512, seq_starts)
    tn = _tile(d, 1024)
    return pl.pallas_call(
        _outproj_kernel,
        grid=(d // tn, t // tm),
        in_specs=[pl.BlockSpec((tm, kh), lambda j, i: (i, c1)),
                  pl.BlockSpec((tm, kh), lambda j, i: (i, c2)),
                  pl.BlockSpec((kh, tn), lambda j, i: (0, j)),
                  pl.BlockSpec((kh, tn), lambda j, i: (1, j)),
                  pl.BlockSpec((tm, tn), lambda j, i: (i, j)),
                  pl.BlockSpec((None, 1, tn), lambda j, i: (_seq_index(i * tm, seq_starts), 0, j))],
        out_specs=pl.BlockSpec((tm, tn), lambda j, i: (i, j)),
        out_shape=jax.ShapeDtypeStruct((t, d), F32),
        compiler_params=_params("arbitrary", "arbitrary"),
        name="outproj_residual",
    )(y1, y2, w, w, x, gate)


def _router_kernel(x_ref, a_ref, sh_ref, wr_ref, br_ref, h_ref, lg_ref):
    h = _norm_mod(x_ref[...], a_ref[...], sh_ref[...])
    h_ref[...] = h.astype(h_ref.dtype)
    lg_ref[...] = jnp.dot(h, wr_ref[...], preferred_element_type=F32, precision=HIGHEST) + br_ref[...]


def _norm_mod_router(x, a, sh, wr, br, seq_starts):
    t, d = x.shape
    tm = _row_tile(t, 512, seq_starts)

    def seq_map(i):
        return (_seq_index(i * tm, seq_starts), 0, 0)

    return pl.pallas_call(
        _router_kernel,
        grid=(t // tm,),
        in_specs=[pl.BlockSpec((tm, d), lambda i: (i, 0)),
                  pl.BlockSpec((None, 1, d), seq_map),
                  pl.BlockSpec((None, 1, d), seq_map),
                  pl.BlockSpec((d, ROUTER_PAD), lambda i: (0, 0)),
                  pl.BlockSpec((1, ROUTER_PAD), lambda i: (0, 0))],
        out_specs=[pl.BlockSpec((tm, d), lambda i: (i, 0)),
                   pl.BlockSpec((tm, ROUTER_PAD), lambda i: (i, 0))],
        out_shape=[jax.ShapeDtypeStruct((t, d), BF16), jax.ShapeDtypeStruct((t, ROUTER_PAD), F32)],
        compiler_params=_params("arbitrary"),
        name="norm_mod_router",
    )(x, a, sh, wr, br)


def _moe_ffn_kernel(ib_ref, if_ref, ie_ref, inew_ref, nu_ref, xs_ref, w1_ref, w3_ref, w2_ref, o_ref,
                    w1_scr, w3_scr, w2_scr):
    del ib_ref, if_ref, ie_ref
    s = pl.program_id(0)

    @pl.when(s < nu_ref[0])
    def _():
        @pl.when(inew_ref[s] == 1)
        def _():
            w1_scr[...] = w1_ref[...].astype(BF16)
            w3_scr[...] = w3_ref[...].astype(BF16)
            w2_scr[...] = w2_ref[...].astype(BF16)

        x = xs_ref[...]
        h1 = jnp.dot(x, w1_scr[...], preferred_element_type=F32)
        h3 = jnp.dot(x, w3_scr[...], preferred_element_type=F32)
        hid = (_silu(h1) * h3).astype(BF16)
        o_ref[...] = jnp.dot(hid, w2_scr[...], preferred_element_type=F32).astype(o_ref.dtype)


def _moe_ffn(xs, item_block, item_half, item_expert, item_new, n_items, w1, w3, w2, layer):
    n_rows, d = xs.shape
    ff = w1.shape[3]
    fh = ff // MOE_FF_SPLIT
    blk = MOE_ROW_BLOCK

    def up_map(s, ib, ih, ie, inew, nu):
        return (layer, ie[s], 0, ih[s])

    def down_map(s, ib, ih, ie, inew, nu):
        return (layer, ie[s], ih[s], 0)

    grid_spec = pltpu.PrefetchScalarGridSpec(
        num_scalar_prefetch=5,
        grid=(item_block.shape[0],),
        in_specs=[pl.BlockSpec((blk, d), lambda s, ib, ih, ie, inew, nu: (ib[s], 0)),
                  pl.BlockSpec((None, None, d, fh), up_map),
                  pl.BlockSpec((None, None, d, fh), up_map),
                  pl.BlockSpec((None, None, fh, d), down_map)],
        out_specs=pl.BlockSpec((None, blk, d), lambda s, ib, ih, ie, inew, nu: (ih[s], ib[s], 0)),
        scratch_shapes=[pltpu.VMEM((d, fh), BF16), pltpu.VMEM((d, fh), BF16), pltpu.VMEM((fh, d), BF16)])
    return pl.pallas_call(
        _moe_ffn_kernel,
        grid_spec=grid_spec,
        out_shape=jax.ShapeDtypeStruct((MOE_FF_SPLIT, n_rows, d), BF16),
        compiler_params=_params("arbitrary"),
        name="moe_ffn",
    )(item_block, item_half, item_expert, item_new, n_items, xs, w1, w3, w2)


def _combine_kernel(x_ref, o0a_ref, o0b_ref, o1a_ref, o1b_ref, gt_ref, g_ref, *rest, final):
    if final:
        fg_ref, o_ref = rest
    else:
        (o_ref,) = rest
    gt = gt_ref[...]
    y = (gt[:, 0:1] * (o0a_ref[...].astype(F32) + o0b_ref[...].astype(F32))
         + gt[:, 1:2] * (o1a_ref[...].astype(F32) + o1b_ref[...].astype(F32)))
    x = x_ref[...] + g_ref[...] * y
    if final:
        ms = jnp.mean(x * x, axis=-1, keepdims=True)
        x = x * lax.rsqrt(ms + EPS) * fg_ref[...]
    o_ref[...] = x


def _moe_combine(x, parts, gates, gate_mod, final_g, seq_starts):
    t, d = x.shape
    tm = _row_tile(t, 512, seq_starts)
    final = final_g is not None
    in_specs = ([pl.BlockSpec((tm, d), lambda i: (i, 0))] * (1 + len(parts))
                + [pl.BlockSpec((tm, ROUTER_PAD), lambda i: (i, 0)),
                   pl.BlockSpec((None, 1, d), lambda i: (_seq_index(i * tm, seq_starts), 0, 0))])
    args = [x, *parts, gates, gate_mod]
    if final:
        in_specs.append(pl.BlockSpec((1, d), lambda i: (0, 0)))
        args.append(final_g)
    return pl.pallas_call(
        functools.partial(_combine_kernel, final=final),
        grid=(t // tm,),
        in_specs=in_specs,
        out_specs=pl.BlockSpec((tm, d), lambda i: (i, 0)),
        out_shape=jax.ShapeDtypeStruct((t, d), F32),
        compiler_params=_params("arbitrary"),
        name="moe_combine_residual",
    )(*args)


def _ret_kernel(*refs, reverse, nchunks, seq_starts, seq_ends):
    rc = RET_CHUNK
    if reverse:
        (cd_ref, q_ref, kt_ref, v_ref, inner_ref, cross_ref, kvd_ref, of_ref, gate_ref, o_ref, st_scr) = refs
    else:
        (cd_ref, q_ref, kt_ref, v_ref, inner_ref, cross_ref, kvd_ref, o_ref, st_scr) = refs
    step = pl.program_id(0)
    if reverse:
        fresh = _is_any((nchunks - step) * rc, seq_ends)
    else:
        fresh = _is_any(step * rc, seq_starts)

    @pl.when(fresh)
    def _():
        st_scr[...] = jnp.zeros_like(st_scr)

    dk, dv = RET_QK_DIM, RET_V_DIM
    for h in range(RET_HEADS):
        q = q_ref[:, h * dk:(h + 1) * dk]
        kt = kt_ref[h * dk:(h + 1) * dk, :]
        v = v_ref[:, h * dv:(h + 1) * dv]
        s = jnp.dot(q, kt, preferred_element_type=F32) * inner_ref[h]
        st = st_scr[h]
        o = jnp.dot(s.astype(BF16), v, preferred_element_type=F32)
        o = o + jnp.dot(q, st.astype(BF16), preferred_element_type=F32) * cross_ref[h]
        ktd = (kt.astype(F32) * kvd_ref[h]).astype(BF16)
        st_scr[h] = st * cd_ref[h] + jnp.dot(ktd, v, preferred_element_type=F32)
        if reverse:
            o = o + of_ref[:, h * dv:(h + 1) * dv]
            ms = jnp.mean(o * o, axis=-1, keepdims=True)
            o = o * lax.rsqrt(ms + EPS) * _silu(gate_ref[:, h * dv:(h + 1) * dv].astype(F32))
        o_ref[:, h * dv:(h + 1) * dv] = o.astype(o_ref.dtype)


def _retention_direction(krt, proj, vcol0, gcol0, log_decay, seq_starts, seq_ends, reverse, o_fwd=None):
    t = proj.shape[0]
    rc = _tile(min(s2 - s1 for s1, s2 in zip(seq_starts, seq_ends)), RET_CHUNK)
    assert rc == RET_CHUNK
    nchunks = t // rc
    pos = jnp.arange(rc, dtype=F32)
    diff = pos[:, None] - pos[None, :]
    lg = log_decay.astype(F32)
    if reverse:
        mask = diff < 0
        dist = -diff
        cross = jnp.exp((rc - pos)[None, :, None] * lg[:, None, None])
        kvd = jnp.exp(pos[None, None, :] * lg[:, None, None])
    else:
        mask = diff >= 0
        dist = diff
        cross = jnp.exp((pos + 1.0)[None, :, None] * lg[:, None, None])
        kvd = jnp.exp((rc - 1.0 - pos)[None, None, :] * lg[:, None, None])
    inner = jnp.where(mask[None], jnp.exp(jnp.where(mask, dist, 0.0)[None] * lg[:, None, None]), 0.0)
    cross = jnp.broadcast_to(cross, (RET_HEADS, rc, RET_V_DIM))
    kvd = jnp.broadcast_to(kvd, (RET_HEADS, RET_QK_DIM, rc))
    chunk_decay = jnp.exp(rc * lg)

    def rb(i):
        return nchunks - 1 - i if reverse else i

    vb = vcol0 // RET_V_WIDTH
    assert vcol0 % RET_V_WIDTH == 0 and gcol0 % RET_V_WIDTH == 0
    in_specs = [pl.BlockSpec(memory_space=pltpu.SMEM),
                pl.BlockSpec((rc, RET_QK_WIDTH), lambda i: (rb(i), 0)),
                pl.BlockSpec((RET_QK_WIDTH, rc), lambda i: (0, rb(i))),
                pl.BlockSpec((rc, RET_V_WIDTH), lambda i: (rb(i), vb)),
                pl.BlockSpec((RET_HEADS, rc, rc), lambda i: (0, 0, 0)),
                pl.BlockSpec((RET_HEADS, rc, RET_V_DIM), lambda i: (0, 0, 0)),
                pl.BlockSpec((RET_HEADS, RET_QK_DIM, rc), lambda i: (0, 0, 0))]
    args = [chunk_decay, proj, krt, proj, inner, cross, kvd]
    if reverse:
        gb = gcol0 // RET_V_WIDTH
        in_specs += [pl.BlockSpec((rc, RET_V_WIDTH), lambda i: (rb(i), 0)),
                     pl.BlockSpec((rc, RET_V_WIDTH), lambda i: (rb(i), gb))]
        args += [o_fwd, proj]
    return pl.pallas_call(
        functools.partial(_ret_kernel, reverse=reverse, nchunks=nchunks,
                          seq_starts=seq_starts, seq_ends=seq_ends),
        grid=(nchunks,),
        in_specs=in_specs,
        out_specs=pl.BlockSpec((rc, RET_V_WIDTH), lambda i: (rb(i), 0)),
        out_shape=jax.ShapeDtypeStruct((t, RET_V_WIDTH), BF16 if reverse else F32),
        scratch_shapes=[pltpu.VMEM((RET_HEADS, RET_QK_DIM, RET_V_DIM), F32)],
        compiler_params=_params("arbitrary"),
        name="retention_bwd_norm_gate" if reverse else "retention_fwd",
    )(*args)


def _ssd_diff_layer(x, a1, sh1, g1, p, i, layer, lay):
    seq_starts, seq_ends, groups = lay
    d = x.shape[1]
    w_in = p['ab_w_in'][i]
    o2 = SSD_WIDTH + SSD_XBC
    o3 = o2 + 2 * SSD_HEADS
    w_main = jnp.concatenate([w_in[:, :o2], w_in[:, o3:]], axis=1).astype(BF16)
    w_dt = jnp.pad(w_in[:, o2:o3], ((0, 0), (0, DT_PAD - 2 * SSD_HEADS))).astype(BF16)
    qcol = o2
    kcol = qcol + DA_QK_WIDTH
    vcol = kcol + DA_QK_WIDTH
    lmax = max(g[2] for g in groups)
    tables = (*_rope_tables(lmax, DA_HEAD_DIM, DA_HEAD_DIM ** -0.5 * math.log2(math.e)),
              *_rope_tables(lmax, DA_HEAD_DIM, 1.0))
    tn = 1024
    assert qcol % tn == 0 and kcol % tn == 0 and vcol % tn == 0
    proj, dt = _norm_mod_matmul(x, a1, sh1, w_main, w_dt, seq_starts,
                                ((qcol // tn, kcol // tn), (kcol // tn, vcol // tn)), DA_HEAD_DIM, tables)

    xact = _conv_silu(proj, SSD_WIDTH, p['ssd_conv_w'][i], p['ssd_conv_b'][i], seq_starts, seq_ends)
    bt = xact[:, SSD_WIDTH:SSD_WIDTH + SSD_BC].T
    dtt = dt.T
    expand = (jnp.arange(SSD_WIDTH)[None, :] // SSD_HEAD_DIM == jnp.arange(SSD_HEADS)[:, None]).astype(F32)
    a_f = -jnp.exp(p['ssd_a_log_fwd'][i].astype(F32))
    a_b = -jnp.exp(p['ssd_a_log_bwd'][i].astype(F32))
    yf = _ssd_direction(xact, bt, dt, dtt, p['ssd_dt_bias_fwd'][i].astype(F32), a_f, expand,
                        seq_starts, seq_ends, reverse=False)
    dskip = jnp.repeat(p['ssd_d'][i].astype(F32), SSD_HEAD_DIM).reshape(1, SSD_WIDTH)
    y_ssd = _ssd_direction(xact, bt, dt, dtt, p['ssd_dt_bias_bwd'][i].astype(F32), a_b, expand,
                           seq_starts, seq_ends, reverse=True,
                           final=(proj, yf, dskip, p['ssd_norm_g'][i].astype(F32).reshape(1, SSD_WIDTH)))

    lam_init = 0.8 - 0.6 * math.exp(-0.3 * layer)
    lam = (jnp.exp(jnp.sum(p['da_lambda_q1'][i].astype(F32) * p['da_lambda_k1'][i].astype(F32)))
           - jnp.exp(jnp.sum(p['da_lambda_q2'][i].astype(F32) * p['da_lambda_k2'][i].astype(F32)))
           + lam_init).reshape(1)
    subln = p['da_subln_g'][i].astype(F32).reshape(1, 2 * DA_HEAD_DIM)
    o = None
    for grp in groups:
        o = _diff_attention(proj, qcol, kcol, vcol, lam, subln, 1.0 - lam_init, grp, prior=o)

    w_out = p['ab_w_out'][i].astype(BF16)
    return _outproj_residual(y_ssd, 0, o, 0, w_out, x, g1, seq_starts)


def _retention_layer(x, a1, sh1, g1, p, j, lay):
    seq_starts, seq_ends, groups = lay
    w_in = p['ret_w_in'][j].astype(BF16)
    lmax = max(g[2] for g in groups)
    tables = (*_rope_tables(lmax, RET_QK_DIM, 1.0), *_rope_tables(lmax, RET_QK_DIM, RET_QK_DIM ** -0.5))
    tn = 1024
    kcol = RET_QK_WIDTH
    vcol = 2 * RET_QK_WIDTH
    gcol = vcol + RET_V_WIDTH
    proj = _norm_mod_matmul(x, a1, sh1, w_in, None, seq_starts,
                            ((0, kcol // tn), (kcol // tn, vcol // tn)), RET_QK_DIM, tables)
    krt = proj[:, kcol:vcol].T
    lg_f = -jnp.exp(p['ret_log_decay_fwd'][j].astype(F32))
    lg_b = -jnp.exp(p['ret_log_decay_bwd'][j].astype(F32))
    of = _retention_direction(krt, proj, vcol, gcol, lg_f, seq_starts, seq_ends, reverse=False)
    o = _retention_direction(krt, proj, vcol, gcol, lg_b, seq_starts, seq_ends, reverse=True, o_fwd=of)
    w_out = p['ret_w_out'][j].astype(BF16)
    return _outproj_residual(o, 0, o, 1, w_out, x, g1, seq_starts)


def _route(logits):
    coarse = logits[:, :MOE_GROUPS]
    grp = jnp.argmax(coarse, axis=-1)
    p_grp = jnp.max(jax.nn.softmax(coarse, axis=-1), axis=-1)
    fine = logits[:, MOE_GROUPS:MOE_GROUPS + MOE_EXPERTS].reshape(-1, MOE_GROUPS, MOE_EXPERTS_PER_GROUP)
    fine = jnp.take_along_axis(fine, grp[:, None, None], axis=1)[:, 0]
    top_v, top_i = lax.top_k(fine, MOE_TOP_K)
    gates = jax.nn.softmax(top_v, axis=-1) * p_grp[:, None]
    experts = (grp[:, None] * MOE_EXPERTS_PER_GROUP + top_i).astype(jnp.int32)
    return experts, gates


def _moe_layer(x, a2, sh2, g2, p, layer, lay, final_g):
    seq_starts, _, _ = lay
    t, d = x.shape
    wr = jnp.concatenate([p['moe_w_group'][layer], p['moe_w_expert'][layer]], axis=1).astype(F32)
    br = jnp.concatenate([p['moe_b_group'][layer], p['moe_b_expert'][layer]]).astype(F32)
    nr = wr.shape[1]
    wr = jnp.pad(wr, ((0, 0), (0, ROUTER_PAD - nr)))
    br = jnp.pad(br, (0, ROUTER_PAD - nr)).reshape(1, ROUTER_PAD)
    h, logits = _norm_mod_router(x, a2, sh2, wr, br, seq_starts)
    experts, gates = _route(logits)

    blk = MOE_ROW_BLOCK
    n_assign = t * MOE_TOP_K
    e_flat = experts.reshape(n_assign)
    onehot = (e_flat[:, None] == jnp.arange(MOE_EXPERTS, dtype=jnp.int32)[None, :]).astype(jnp.int32)
    csum = jnp.cumsum(onehot, axis=0)
    counts = csum[-1]
    rank = jnp.sum(csum * onehot, axis=1) - 1
    padded = (counts + blk - 1) // blk * blk
    ends_pad = jnp.cumsum(padded)
    start_pad = ends_pad - padded
    dest = (start_pad[e_flat] + rank).astype(jnp.int32)
    n_rows = n_assign + MOE_EXPERTS * blk
    n_blocks = n_rows // blk
    tok_flat = jnp.repeat(jnp.arange(t, dtype=jnp.int32), MOE_TOP_K)
    row_tok = (jnp.arange(n_rows, dtype=jnp.int32) % t).at[dest].set(tok_flat)
    nblk_e = (padded // blk).astype(jnp.int32)
    ends_blk = jnp.cumsum(nblk_e)
    start_blk = ends_blk - nblk_e
    n_items = (MOE_FF_SPLIT * ends_blk[-1]).astype(jnp.int32)
    item = jnp.minimum(jnp.arange(MOE_FF_SPLIT * n_blocks, dtype=jnp.int32), n_items - 1)
    item_expert = jnp.minimum(jnp.searchsorted(MOE_FF_SPLIT * ends_blk, item, side='right'),
                              MOE_EXPERTS - 1).astype(jnp.int32)
    local = item - MOE_FF_SPLIT * start_blk[item_expert]
    per_half = jnp.maximum(nblk_e[item_expert], 1)
    item_half = (local // per_half).astype(jnp.int32)
    item_block = (start_blk[item_expert] + local % per_half).astype(jnp.int32)
    changed = jnp.logical_or(item_expert[1:] != item_expert[:-1], item_half[1:] != item_half[:-1])
    item_new = jnp.concatenate([jnp.ones((1,), jnp.int32), changed.astype(jnp.int32)])

    xs = jnp.take(h, row_tok, axis=0, mode='clip')
    out = _moe_ffn(xs, item_block, item_half, item_expert, item_new, n_items.reshape(1),
                   p['moe_w1'], p['moe_w3'], p['moe_w2'], layer)
    out = out.reshape(MOE_FF_SPLIT * n_rows, d)
    dest2 = dest.reshape(t, MOE_TOP_K)
    parts = [jnp.take(out, dest2[:, k] + half * n_rows, axis=0, mode='clip')
             for k in range(MOE_TOP_K) for half in range(MOE_FF_SPLIT)]
    gates_pad = jnp.pad(gates, ((0, 0), (0, ROUTER_PAD - MOE_TOP_K)))
    return _moe_combine(x, parts, gates_pad, g2, final_g, seq_starts)


def kernel(x_prompt, x_sample, c_prompt, c_sample, ada_w, ada_b, norm_mix_g, norm_ffn_g, ab_w_in, ssd_conv_w, ssd_conv_b, ssd_a_log_fwd, ssd_a_log_bwd, ssd_dt_bias_fwd, ssd_dt_bias_bwd, ssd_d, ssd_norm_g, da_lambda_q1, da_lambda_k1, da_lambda_q2, da_lambda_k2, da_subln_g, ab_w_out, ret_w_in, ret_log_decay_fwd, ret_log_decay_bwd, ret_w_out, moe_w_group, moe_b_group, moe_w_expert, moe_b_expert, moe_w1, moe_w3, moe_w2, final_norm_g):
    p = dict(ab_w_in=ab_w_in, ssd_conv_w=ssd_conv_w, ssd_conv_b=ssd_conv_b,
             ssd_a_log_fwd=ssd_a_log_fwd, ssd_a_log_bwd=ssd_a_log_bwd,
             ssd_dt_bias_fwd=ssd_dt_bias_fwd, ssd_dt_bias_bwd=ssd_dt_bias_bwd, ssd_d=ssd_d,
             ssd_norm_g=ssd_norm_g, da_lambda_q1=da_lambda_q1, da_lambda_k1=da_lambda_k1,
             da_lambda_q2=da_lambda_q2, da_lambda_k2=da_lambda_k2, da_subln_g=da_subln_g,
             ab_w_out=ab_w_out, ret_w_in=ret_w_in, ret_log_decay_fwd=ret_log_decay_fwd,
             ret_log_decay_bwd=ret_log_decay_bwd, ret_w_out=ret_w_out, moe_w_group=moe_w_group,
             moe_b_group=moe_b_group, moe_w_expert=moe_w_expert, moe_b_expert=moe_b_expert,
             moe_w1=moe_w1, moe_w3=moe_w3, moe_w2=moe_w2)
    bp, lp, d = x_prompt.shape
    bs, ls, _ = x_sample.shape
    depth = ada_w.shape[0]
    groups = ((0, bp, lp), (bp * lp, bs, ls))
    seq_starts = tuple(r0 + b * l for (r0, nb, l) in groups for b in range(nb))
    seq_ends = tuple(r0 + (b + 1) * l for (r0, nb, l) in groups for b in range(nb))
    lay = (seq_starts, seq_ends, groups)
    nseq = len(seq_starts)
    x = jnp.concatenate([x_prompt.reshape(bp * lp, d), x_sample.reshape(bs * ls, d)], axis=0)
    c = jnp.concatenate([c_prompt, c_sample], axis=0).astype(F32)
    c_pad = jnp.pad(c, ((0, -nseq % 8), (0, 0)))
    mod = _ada_modulation(c_pad, ada_w.astype(F32), ada_b.astype(F32))[:, :nseq]

    for layer in range(depth):
        sh1, sc1, g1, sh2, sc2, g2 = [m.reshape(nseq, 1, d) for m in jnp.split(mod[layer], 6, axis=-1)]
        a1 = norm_mix_g[layer].astype(F32)[None, None, :] * (1.0 + sc1)
        a2 = norm_ffn_g[layer].astype(F32)[None, None, :] * (1.0 + sc2)
        if layer % 2 == 0:
            x = _ssd_diff_layer(x, a1, sh1, g1, p, layer // 2, layer, lay)
        else:
            x = _retention_layer(x, a1, sh1, g1, p, layer // 2, lay)
        final_g = final_norm_g.astype(F32).reshape(1, d) if layer == depth - 1 else None
        x = _moe_layer(x, a2, sh2, g2, p, layer, lay, final_g)
    y_prompt = x[:bp * lp].reshape(bp, lp, d)
    y_sample = x[bp * lp:].reshape(bs, ls, d)
    return (y_prompt, y_sample)
```

```python
import functools
import math

import jax
import jax.numpy as jnp
from jax import lax
from jax.experimental import pallas as pl
from jax.experimental.pallas import tpu as pltpu

F32 = jnp.float32
BF16 = jnp.bfloat16
HIGHEST = lax.Precision.HIGHEST

EPS = 1e-6
SUBLN_EPS = 1e-5
ROPE_THETA = 10000.0

SSD_HEADS = 32
SSD_HEAD_DIM = 64
SSD_WIDTH = SSD_HEADS * SSD_HEAD_DIM
SSD_GROUPS = 4
SSD_STATE = 128
SSD_CONV = 5
SSD_BC = SSD_GROUPS * SSD_STATE
SSD_XBC = SSD_WIDTH + 2 * SSD_BC
SSD_CHUNK = 128
DA_HEADS = 8
DA_HEAD_DIM = 128
DA_QK_WIDTH = 2 * DA_HEADS * DA_HEAD_DIM
DA_V_WIDTH = DA_HEADS * 2 * DA_HEAD_DIM
RET_HEADS = 8
RET_QK_DIM = 256
RET_V_DIM = 512
RET_QK_WIDTH = RET_HEADS * RET_QK_DIM
RET_V_WIDTH = RET_HEADS * RET_V_DIM
RET_CHUNK = 256
MOE_GROUPS = 4
MOE_EXPERTS_PER_GROUP = 8
MOE_EXPERTS = MOE_GROUPS * MOE_EXPERTS_PER_GROUP
MOE_TOP_K = 2
MOE_ROW_BLOCK = 512
MOE_FF_SPLIT = 2
ROUTER_PAD = 128
DT_PAD = 128
HALO = 16

VMEM_LIMIT_BYTES = 56 * 1024 * 1024


def _params(*semantics):
    return pltpu.CompilerParams(dimension_semantics=semantics, vmem_limit_bytes=VMEM_LIMIT_BYTES)


def _silu(x):
    return x * jax.nn.sigmoid(x)


def _softplus(x):
    return jnp.maximum(x, 0.0) + jnp.log1p(jnp.exp(-jnp.abs(x)))


def _lane_repeat(x, n):
    return x if n == 1 else jnp.concatenate([x] * n, axis=1)


def _tile(n, pref):
    t = min(n, pref)
    assert n % t == 0, (n, pref)
    return t


def _row_tile(t, pref, seq_starts):
    tm = min(t, pref)
    while t % tm or any(st % tm for st in seq_starts):
        tm //= 2
    return tm


def _seq_index(row, seq_starts):
    s = 0
    for st in seq_starts[1:]:
        s = s + jnp.where(row >= st, 1, 0)
    return s


def _is_any(row, values):
    hit = row == values[0]
    for v in values[1:]:
        hit = jnp.logical_or(hit, row == v)
    return hit


def _ada_kernel(c_ref, w_ref, b_ref, o_ref):
    o_ref[...] = jnp.dot(_silu(c_ref[...]), w_ref[...], preferred_element_type=F32,
                         precision=HIGHEST) + b_ref[...]


def _ada_modulation(c_pad, ada_w, ada_b):
    depth, d, n = ada_w.shape
    rows = c_pad.shape[0]
    tn = _tile(n, 1024)
    return pl.pallas_call(
        _ada_kernel,
        grid=(depth, n // tn),
        in_specs=[pl.BlockSpec((rows, d), lambda l, j: (0, 0)),
                  pl.BlockSpec((None, d, tn), lambda l, j: (l, 0, j)),
                  pl.BlockSpec((None, 1, tn), lambda l, j: (l, 0, j))],
        out_specs=pl.BlockSpec((None, rows, tn), lambda l, j: (l, 0, j)),
        out_shape=jax.ShapeDtypeStruct((depth, rows, n), F32),
        compiler_params=_params("arbitrary", "arbitrary"),
        name="ada_modulation",
    )(c_pad, ada_w, ada_b.reshape(depth, 1, n))


def _norm_mod(x, a, sh):
    ms = jnp.mean(x * x, axis=-1, keepdims=True)
    return (x * lax.rsqrt(ms + EPS)) * a + sh


def _rotate_half_tile(x, cos, sin, head_dim):
    half = head_dim // 2
    outs = []
    for h in range(x.shape[1] // head_dim):
        xh = x[:, h * head_dim:(h + 1) * head_dim]
        if half % 128 == 0:
            rot = jnp.concatenate([xh[:, half:], xh[:, :half]], axis=-1)
        else:
            rot = pltpu.roll(xh, half, 1)
        outs.append(xh * cos + rot * sin)
    return jnp.concatenate(outs, axis=-1)


def _nmm_kernel(*refs, with_dt, rope_tiles, head_dim):
    x_ref, a_ref, sh_ref, w_ref, cq_ref, sq_ref, ck_ref, sk_ref = refs[:8]
    if with_dt:
        wdt_ref, o_ref, odt_ref, h_scr = refs[8:]
    else:
        o_ref, h_scr = refs[8:]
    j = pl.program_id(1)

    @pl.when(j == 0)
    def _():
        hb = _norm_mod(x_ref[...], a_ref[...], sh_ref[...]).astype(BF16)
        h_scr[...] = hb
        if with_dt:
            odt_ref[...] = jnp.dot(hb, wdt_ref[...], preferred_element_type=F32)

    acc = jnp.dot(h_scr[...], w_ref[...], preferred_element_type=F32)
    (q_lo, q_hi), (k_lo, k_hi) = rope_tiles
    is_q = jnp.logical_and(j >= q_lo, j < q_hi)
    is_k = jnp.logical_and(j >= k_lo, j < k_hi)

    @pl.when(is_q)
    def _():
        o_ref[...] = _rotate_half_tile(acc, cq_ref[...], sq_ref[...], head_dim).astype(o_ref.dtype)

    @pl.when(is_k)
    def _():
        o_ref[...] = _rotate_half_tile(acc, ck_ref[...], sk_ref[...], head_dim).astype(o_ref.dtype)

    @pl.when(jnp.logical_not(jnp.logical_or(is_q, is_k)))
    def _():
        o_ref[...] = acc.astype(o_ref.dtype)


def _norm_mod_matmul(x, a, sh, w, wdt, seq_starts, rope_tiles, head_dim, tables):
    t, d = x.shape
    n = w.shape[1]
    tm = _row_tile(t, 1024, seq_starts)
    tn = _tile(n, 1024)
    with_dt = wdt is not None

    def seq_map(i, j):
        return (_seq_index(i * tm, seq_starts), 0, 0)

    def pos_map(i, j):
        r = i * tm
        s = _seq_index(r, seq_starts)
        st = 0
        for k, v in enumerate(seq_starts):
            st = st + jnp.where(s == k, v, 0)
        return ((r - st) // tm, 0)

    in_specs = [pl.BlockSpec((tm, d), lambda i, j: (i, 0)),
                pl.BlockSpec((None, 1, d), seq_map),
                pl.BlockSpec((None, 1, d), seq_map),
                pl.BlockSpec((d, tn), lambda i, j: (0, j))] + [pl.BlockSpec((tm, head_dim), pos_map)] * 4
    out_specs = [pl.BlockSpec((tm, tn), lambda i, j: (i, j))]
    out_shape = [jax.ShapeDtypeStruct((t, n), BF16)]
    args = [x, a, sh, w, *tables]
    if with_dt:
        in_specs.append(pl.BlockSpec((d, DT_PAD), lambda i, j: (0, 0)))
        out_specs.append(pl.BlockSpec((tm, DT_PAD), lambda i, j: (i, 0)))
        out_shape.append(jax.ShapeDtypeStruct((t, DT_PAD), F32))
        args.append(wdt)
    res = pl.pallas_call(
        functools.partial(_nmm_kernel, with_dt=with_dt, rope_tiles=rope_tiles, head_dim=head_dim),
        grid=(t // tm, n // tn),
        in_specs=in_specs, out_specs=out_specs, out_shape=out_shape,
        scratch_shapes=[pltpu.VMEM((tm, d), BF16)],
        compiler_params=_params("arbitrary", "arbitrary"),
        name="norm_mod_matmul",
    )(*args)
    return res if with_dt else res[0]


def _conv_kernel(cur_ref, prev_ref, next_ref, w_ref, b_ref, o_ref, ext_scr, *, tm, seq_starts, seq_ends):
    r0 = pl.program_id(0) * tm
    at_start = _is_any(r0, seq_starts)
    at_end = _is_any(r0 + tm, seq_ends)
    prev = prev_ref[...].astype(F32)[HALO - 8:HALO]
    nxt = next_ref[...].astype(F32)[0:8]
    ext_scr[0:8, :] = jnp.where(at_start, 0.0, prev)
    ext_scr[8:8 + tm, :] = cur_ref[...].astype(F32)
    ext_scr[8 + tm:16 + tm, :] = jnp.where(at_end, 0.0, nxt)
    pad = SSD_CONV // 2
    acc = b_ref[...] + w_ref[0:1, :] * ext_scr[8 - pad:8 - pad + tm, :]
    for k in range(1, SSD_CONV):
        acc = acc + w_ref[k:k + 1, :] * ext_scr[8 - pad + k:8 - pad + k + tm, :]
    o_ref[...] = _silu(acc).astype(o_ref.dtype)


def _conv_silu(proj, col0, conv_w, conv_b, seq_starts, seq_ends):
    t = proj.shape[0]
    tm = _row_tile(t, 512, seq_starts)
    tc = 1024
    assert col0 % tc == 0 and SSD_XBC % tc == 0 and tm % HALO == 0
    cb = col0 // tc
    hb = tm // HALO
    last_halo = t // HALO - 1
    return pl.pallas_call(
        functools.partial(_conv_kernel, tm=tm, seq_starts=seq_starts, seq_ends=seq_ends),
        grid=(t // tm, SSD_XBC // tc),
        in_specs=[pl.BlockSpec((tm, tc), lambda i, j: (i, cb + j)),
                  pl.BlockSpec((HALO, tc), lambda i, j: (jnp.maximum(i * hb - 1, 0), cb + j)),
                  pl.BlockSpec((HALO, tc), lambda i, j: (jnp.minimum((i + 1) * hb, last_halo), cb + j)),
                  pl.BlockSpec((SSD_CONV, tc), lambda i, j: (0, j)),
                  pl.BlockSpec((1, tc), lambda i, j: (0, j))],
        out_specs=pl.BlockSpec((tm, tc), lambda i, j: (i, j)),
        out_shape=jax.ShapeDtypeStruct((t, SSD_XBC), BF16),
        scratch_shapes=[pltpu.VMEM((tm + 16, tc), F32)],
        compiler_params=_params("arbitrary", "arbitrary"),
        name="ssd_conv_silu",
    )(proj, proj, proj, conv_w, conv_b.reshape(1, SSD_XBC))


def _ssd_kernel(*refs, reverse, dcol, nchunks, seq_starts, seq_ends):
    ch = SSD_CHUNK
    if reverse:
        (xact_ref, bt_ref, dt_ref, dtt_ref, bias_ref, a_ref, biast_ref, at_ref, e_ref,
         z_ref, yf_ref, dskip_ref, g_ref, o_ref, h_scr, y_scr) = refs
    else:
        (xact_ref, bt_ref, dt_ref, dtt_ref, bias_ref, a_ref, biast_ref, at_ref, e_ref,
         o_ref, h_scr) = refs
    step = pl.program_id(0)
    if reverse:
        row_end = (nchunks - step) * ch
        fresh = _is_any(row_end, seq_ends)
    else:
        fresh = _is_any(step * ch, seq_starts)

    @pl.when(fresh)
    def _():
        h_scr[...] = jnp.zeros_like(h_scr)

    dt = _softplus(dt_ref[:, dcol:dcol + SSD_HEADS] + bias_ref[...])
    dta = dt * a_ref[...]
    dtt = _softplus(dtt_ref[dcol:dcol + SSD_HEADS, :] + biast_ref[...])
    dtat = dtt * at_ref[...]
    row = lax.broadcasted_iota(jnp.int32, (ch, ch), 0)
    col = lax.broadcasted_iota(jnp.int32, (ch, ch), 1)
    if reverse:
        keep = col >= row
    else:
        keep = col <= row
    tri = jnp.where(keep, 1.0, 0.0).astype(F32)
    trit = jnp.where(keep, 0.0, 1.0).astype(F32) + jnp.where(row == col, 1.0, 0.0).astype(F32)
    cs = jnp.dot(tri, dta, preferred_element_type=F32, precision=HIGHEST)
    cst = jnp.dot(dtat, trit, preferred_element_type=F32, precision=HIGHEST)
    last = 0 if reverse else ch - 1
    ecs = jnp.exp(cs)
    wt = dtt * jnp.exp(cst[:, last:last + 1] - cst)
    cdec_x = jnp.dot(jnp.broadcast_to(ecs[last:last + 1, :], (8, SSD_HEADS)), e_ref[...],
                     preferred_element_type=F32, precision=HIGHEST)[0:1, :]

    lane = lax.broadcasted_iota(jnp.int32, (1, 2 * SSD_HEAD_DIM), 1)
    first = lane < SSD_HEAD_DIM
    gw = SSD_WIDTH // SSD_GROUPS
    hpg = SSD_HEADS // SSD_GROUPS
    pw = 2 * SSD_HEAD_DIM
    ssq = jnp.zeros((ch, pw), F32)
    for g in range(SSD_GROUPS):
        bgt = bt_ref[g * SSD_STATE:(g + 1) * SSD_STATE, :]
        c0 = SSD_WIDTH + SSD_BC + g * SSD_STATE
        cg = xact_ref[:, c0:c0 + SSD_STATE]
        scores = jnp.dot(cg, bgt, preferred_element_type=F32)
        cgf = cg.astype(F32)
        bgtf = bgt.astype(F32)
        for pr in range(hpg // 2):
            lo = g * gw + pr * pw
            hi = lo + pw
            xpair = xact_ref[:, lo:hi].astype(F32)
            hpair = h_scr[g, :, pr * pw:(pr + 1) * pw]
            xsel = (jnp.where(first, xpair, 0.0).astype(BF16), jnp.where(first, 0.0, xpair).astype(BF16))
            hsel = (jnp.where(first, hpair, 0.0).astype(BF16), jnp.where(first, 0.0, hpair).astype(BF16))
            y = None
            bws = []
            for u in range(2):
                h = g * hpg + 2 * pr + u
                csb = jnp.broadcast_to(cs[:, h:h + 1], (ch, ch))
                seg = jnp.where(keep, csb - cst[h:h + 1, :], -jnp.inf)
                m = (scores * jnp.exp(seg) * dtt[h:h + 1, :]).astype(BF16)
                ce = (cgf * jnp.exp(csb)).astype(BF16)
                part = jnp.dot(jnp.concatenate([m, ce], axis=1), jnp.concatenate([xsel[u], hsel[u]], axis=0),
                               preferred_element_type=F32)
                y = part if y is None else y + part
                bws.append((bgtf * wt[h:h + 1, :]).astype(BF16))
            h_scr[g, :, pr * pw:(pr + 1) * pw] = hpair * cdec_x[:, lo:hi] + jnp.dot(
                jnp.concatenate(bws, axis=1), jnp.concatenate(xsel, axis=0), preferred_element_type=F32)
            if reverse:
                y = y + yf_ref[:, lo:hi] + dskip_ref[:, lo:hi] * xpair
                y = y * _silu(z_ref[:, lo:hi].astype(F32))
                ssq = ssq + y * y
                y_scr[:, lo:hi] = y
            else:
                o_ref[:, lo:hi] = y
    if reverse:
        inv = lax.rsqrt(jnp.sum(ssq, axis=-1, keepdims=True) * (1.0 / SSD_WIDTH) + EPS)
        o_ref[...] = (y_scr[...] * inv * g_ref[...]).astype(o_ref.dtype)


def _ssd_direction(xact, bt, dt, dtt, bias, a, expand, seq_starts, seq_ends, reverse, final=None):
    t = xact.shape[0]
    ch = SSD_CHUNK
    nchunks = t // ch
    dcol = SSD_HEADS if reverse else 0

    def rb(i):
        return nchunks - 1 - i if reverse else i

    in_specs = [pl.BlockSpec((ch, SSD_XBC), lambda i: (rb(i), 0)),
                pl.BlockSpec((SSD_BC, ch), lambda i: (0, rb(i))),
                pl.BlockSpec((ch, DT_PAD), lambda i: (rb(i), 0)),
                pl.BlockSpec((DT_PAD, ch), lambda i: (0, rb(i))),
                pl.BlockSpec((1, SSD_HEADS), lambda i: (0, 0)),
                pl.BlockSpec((1, SSD_HEADS), lambda i: (0, 0)),
                pl.BlockSpec((SSD_HEADS, 1), lambda i: (0, 0)),
                pl.BlockSpec((SSD_HEADS, 1), lambda i: (0, 0)),
                pl.BlockSpec((SSD_HEADS, SSD_WIDTH), lambda i: (0, 0))]
    args = [xact, bt, dt, dtt, bias.reshape(1, -1), a.reshape(1, -1), bias.reshape(-1, 1), a.reshape(-1, 1), expand]
    scratch = [pltpu.VMEM((SSD_GROUPS, SSD_STATE, SSD_WIDTH // SSD_GROUPS), F32)]
    if reverse:
        proj, yf, dskip, g = final
        in_specs += [pl.BlockSpec((ch, SSD_WIDTH), lambda i: (rb(i), 0)),
                     pl.BlockSpec((ch, SSD_WIDTH), lambda i: (rb(i), 0)),
                     pl.BlockSpec((1, SSD_WIDTH), lambda i: (0, 0)),
                     pl.BlockSpec((1, SSD_WIDTH), lambda i: (0, 0))]
        args += [proj, yf, dskip, g]
        scratch.append(pltpu.VMEM((ch, SSD_WIDTH), F32))
    return pl.pallas_call(
        functools.partial(_ssd_kernel, reverse=reverse, dcol=dcol, nchunks=nchunks,
                          seq_starts=seq_starts, seq_ends=seq_ends),
        grid=(nchunks,),
        in_specs=in_specs,
        out_specs=pl.BlockSpec((ch, SSD_WIDTH), lambda i: (rb(i), 0)),
        out_shape=jax.ShapeDtypeStruct((t, SSD_WIDTH), BF16 if reverse else F32),
        scratch_shapes=scratch,
        compiler_params=_params("arbitrary"),
        name="ssd_bwd_gate_norm" if reverse else "ssd_fwd",
    )(*args)


def _rope_tables(lmax, head_dim, scale):
    half = head_dim // 2
    inv = 1.0 / (ROPE_THETA ** (jnp.arange(half, dtype=F32) / half))
    ang = jnp.arange(lmax, dtype=F32)[:, None] * inv[None, :]
    cos = jnp.cos(ang) * scale
    sin = jnp.sin(ang) * scale
    return jnp.concatenate([cos, cos], axis=-1), jnp.concatenate([-sin, sin], axis=-1)


def _seq_lookup(row, seq_starts, values):
    s = _seq_index(row, seq_starts)
    out = 0
    for k, v in enumerate(values):
        out = out + jnp.where(s == k, v, 0)
    return out


def _attn_kernel(lam_ref, q_ref, k_ref, v_ref, g_ref, o_ref, m_scr, l_scr, acc_scr, s_scr, p_scr, alpha_scr, *,
                 tk, nkv, strip, out_scale, seq_starts, seq_kv_blocks):
    dh = DA_HEAD_DIM
    tq = q_ref.shape[0]
    lanes = m_scr.shape[-1]
    kv_step = pl.program_id(2)
    n_valid = _seq_lookup(pl.program_id(1) * tq, seq_starts, seq_kv_blocks)

    @pl.when(kv_step == 0)
    def _():
        m_scr[...] = jnp.full_like(m_scr, -jnp.inf)
        l_scr[...] = jnp.zeros_like(l_scr)
        acc_scr[...] = jnp.zeros_like(acc_scr)

    def body(j, carry):
        off = pl.multiple_of(j * tk, tk)
        for u in range(2):
            s_scr[u] = lax.dot_general(q_ref[:, u * dh:(u + 1) * dh], k_ref[pl.ds(off, tk), u * dh:(u + 1) * dh],
                                       (((1,), (1,)), ((), ())), preferred_element_type=F32)
        for u in range(2):
            for r in range(tq // strip):
                rows = slice(r * strip, (r + 1) * strip)
                s = s_scr[u, rows, :]
                m_prev = m_scr[u, rows, :]
                m_new = jnp.maximum(m_prev, jnp.max(s, axis=-1, keepdims=True))
                alpha = jnp.exp2(m_prev - m_new)
                p = jnp.exp2(s - _lane_repeat(m_new, tk // lanes))
                l_scr[u, rows, :] = alpha * l_scr[u, rows, :] + jnp.sum(p, axis=-1, keepdims=True)
                m_scr[u, rows, :] = m_new
                alpha_scr[u, rows, :] = alpha
                p_scr[u, rows, :] = p.astype(BF16)
            pv = jnp.dot(p_scr[u], v_ref[pl.ds(off, tk), :], preferred_element_type=F32)
            acc_scr[u] = acc_scr[u] * _lane_repeat(alpha_scr[u], acc_scr.shape[-1] // lanes) + pv
        return carry

    @pl.when(kv_step < n_valid)
    def _():
        lax.fori_loop(0, nkv, body, 0, unroll=2 if nkv % 2 == 0 else 1)

    @pl.when(kv_step == n_valid - 1)
    def _():
        lam = lam_ref[0]
        rep = acc_scr.shape[-1] // lanes
        o = (acc_scr[0] / _lane_repeat(l_scr[0], rep)
             - lam * (acc_scr[1] / _lane_repeat(l_scr[1], rep)))
        ms = jnp.mean(o * o, axis=-1, keepdims=True)
        o_ref[...] = (o * lax.rsqrt(ms + SUBLN_EPS) * g_ref[...] * out_scale).astype(o_ref.dtype)


def _diff_attention(proj, qcol0, kcol0, vcol0, lam, subln_g, out_scale, seq_starts, seq_ends):
    t = proj.shape[0]
    pw = 2 * DA_HEAD_DIM
    lens = tuple(e - s for s, e in zip(seq_starts, seq_ends))
    kvb = min(lens)
    assert all(n % kvb == 0 for n in lens) and all(s % kvb == 0 for s in seq_starts)
    tq = _tile(kvb, 512)
    tk = _tile(kvb, 1024)
    strip = 32
    lanes = 128
    assert vcol0 % pw == 0 and qcol0 % pw == 0 and kcol0 % pw == 0
    qc0, kc0, vc0 = qcol0 // pw, kcol0 // pw, vcol0 // pw
    seq_kv_blocks = tuple(n // kvb for n in lens)
    seq_first_block = tuple(s // kvb for s in seq_starts)

    def kv_row_block(i, c):
        row = i * tq
        n_valid = _seq_lookup(row, seq_starts, seq_kv_blocks)
        return _seq_lookup(row, seq_starts, seq_first_block) + jnp.minimum(c, n_valid - 1)

    return pl.pallas_call(
        functools.partial(_attn_kernel, tk=tk, nkv=kvb // tk, strip=strip, out_scale=out_scale,
                          seq_starts=seq_starts, seq_kv_blocks=seq_kv_blocks),
        grid=(DA_HEADS, t // tq, max(seq_kv_blocks)),
        in_specs=[pl.BlockSpec(memory_space=pltpu.SMEM),
                  pl.BlockSpec((tq, pw), lambda h, i, c: (i, qc0 + h)),
                  pl.BlockSpec((kvb, pw), lambda h, i, c: (kv_row_block(i, c), kc0 + h)),
                  pl.BlockSpec((kvb, pw), lambda h, i, c: (kv_row_block(i, c), vc0 + h)),
                  pl.BlockSpec((1, pw), lambda h, i, c: (0, 0))],
        out_specs=pl.BlockSpec((tq, pw), lambda h, i, c: (i, h)),
        out_shape=jax.ShapeDtypeStruct((t, DA_V_WIDTH), BF16),
        scratch_shapes=[pltpu.VMEM((2, tq, lanes), F32), pltpu.VMEM((2, tq, lanes), F32),
                        pltpu.VMEM((2, tq, pw), F32), pltpu.VMEM((2, tq, tk), F32), pltpu.VMEM((2, tq, tk), BF16),
                        pltpu.VMEM((2, tq, lanes), F32)],
        compiler_params=_params("arbitrary", "arbitrary", "arbitrary"),
        name="diff_attention",
    )(lam, proj, proj, proj, subln_g)


def _outproj_kernel(y1_ref, y2_ref, w1_ref, w2_ref, x_ref, g_ref, o_ref):
    acc = jnp.dot(y1_ref[...], w1_ref[...], preferred_element_type=F32)
    acc = acc + jnp.dot(y2_ref[...], w2_ref[...], preferred_element_type=F32)
    o_ref[...] = x_ref[...] + g_ref[...] * acc


def _outproj_residual(y1, c1, y2, c2, w, x, gate, seq_starts):
    t, d = x.shape
    kh = w.shape[0] // 2
    tm = _row_tile(t, 512, seq_starts)
    tn = _tile(d, 1024)
    return pl.pallas_call(
        _outproj_kernel,
        grid=(d // tn, t // tm),
        in_specs=[pl.BlockSpec((tm, kh), lambda j, i: (i, c1)),
                  pl.BlockSpec((tm, kh), lambda j, i: (i, c2)),
                  pl.BlockSpec((kh, tn), lambda j, i: (0, j)),
                  pl.BlockSpec((kh, tn), lambda j, i: (1, j)),
                  pl.BlockSpec((tm, tn), lambda j, i: (i, j)),
                  pl.BlockSpec((None, 1, tn), lambda j, i: (_seq_index(i * tm, seq_starts), 0, j))],
        out_specs=pl.BlockSpec((tm, tn), lambda j, i: (i, j)),
        out_shape=jax.ShapeDtypeStruct((t, d), F32),
        compiler_params=_params("arbitrary", "arbitrary"),
        name="outproj_residual",
    )(y1, y2, w, w, x, gate)


def _router_kernel(x_ref, a_ref, sh_ref, wr_ref, br_ref, h_ref, lg_ref):
    h = _norm_mod(x_ref[...], a_ref[...], sh_ref[...])
    h_ref[...] = h.astype(h_ref.dtype)
    lg_ref[...] = jnp.dot(h, wr_ref[...], preferred_element_type=F32, precision=HIGHEST) + br_ref[...]


def _norm_mod_router(x, a, sh, wr, br, seq_starts):
    t, d = x.shape
    tm = _row_tile(t, 512, seq_starts)

    def seq_map(i):
        return (_seq_index(i * tm, seq_starts), 0, 0)

    return pl.pallas_call(
        _router_kernel,
        grid=(t // tm,),
        in_specs=[pl.BlockSpec((tm, d), lambda i: (i, 0)),
                  pl.BlockSpec((None, 1, d), seq_map),
                  pl.BlockSpec((None, 1, d), seq_map),
                  pl.BlockSpec((d, ROUTER_PAD), lambda i: (0, 0)),
                  pl.BlockSpec((1, ROUTER_PAD), lambda i: (0, 0))],
        out_specs=[pl.BlockSpec((tm, d), lambda i: (i, 0)),
                   pl.BlockSpec((tm, ROUTER_PAD), lambda i: (i, 0))],
        out_shape=[jax.ShapeDtypeStruct((t, d), BF16), jax.ShapeDtypeStruct((t, ROUTER_PAD), F32)],
        compiler_params=_params("arbitrary"),
        name="norm_mod_router",
    )(x, a, sh, wr, br)


def _moe_ffn_kernel(ib_ref, ob_ref, oh_ref, ih_ref, ie_ref, inew_ref, nxe_ref, nxh_ref, hasnx_ref, nu_ref,
                    xs_ref, w1_hbm, w3_hbm, w2_hbm, o_ref,
                    w1_stage, w3_stage, w2_stage, w1_scr, w3_scr, w2_scr, sems, *, layer, fh):
    del ib_ref, ob_ref, oh_ref
    s = pl.program_id(0)

    @pl.when(s >= nu_ref[0])
    def _():
        o_ref[...] = jnp.zeros_like(o_ref)

    def slab_copies(e, half):
        c0 = pl.multiple_of(half * fh, fh)
        return (pltpu.make_async_copy(w1_hbm.at[layer, e, :, pl.ds(c0, fh)], w1_stage, sems.at[0]),
                pltpu.make_async_copy(w3_hbm.at[layer, e, :, pl.ds(c0, fh)], w3_stage, sems.at[1]),
                pltpu.make_async_copy(w2_hbm.at[layer, e, pl.ds(c0, fh), :], w2_stage, sems.at[2]))

    @pl.when(s == 0)
    def _():
        for c in slab_copies(ie_ref[0], ih_ref[0]):
            c.start()

    @pl.when(s < nu_ref[0])
    def _():
        @pl.when(inew_ref[s] == 1)
        def _():
            for c in slab_copies(ie_ref[s], ih_ref[s]):
                c.wait()
            w1_scr[...] = w1_stage[...].astype(BF16)
            w3_scr[...] = w3_stage[...].astype(BF16)
            w2_scr[...] = w2_stage[...].astype(BF16)

            @pl.when(hasnx_ref[s] == 1)
            def _():
                for c in slab_copies(nxe_ref[s], nxh_ref[s]):
                    c.start()

        x = xs_ref[...]
        h1 = jnp.dot(x, w1_scr[...], preferred_element_type=F32)
        h3 = jnp.dot(x, w3_scr[...], preferred_element_type=F32)
        hid = (_silu(h1) * h3).astype(BF16)
        o_ref[...] = jnp.dot(hid, w2_scr[...], preferred_element_type=F32).astype(o_ref.dtype)


def _moe_ffn(xs, tables, w1, w3, w2, layer):
    n_rows, d = xs.shape
    ff = w1.shape[3]
    fh = ff // MOE_FF_SPLIT
    blk = MOE_ROW_BLOCK
    n_tab = len(tables)

    grid_spec = pltpu.PrefetchScalarGridSpec(
        num_scalar_prefetch=n_tab,
        grid=(tables[0].shape[0],),
        in_specs=[pl.BlockSpec((blk, d), lambda s, ib, *_: (ib[s], 0)),
                  pl.BlockSpec(memory_space=pl.ANY),
                  pl.BlockSpec(memory_space=pl.ANY),
                  pl.BlockSpec(memory_space=pl.ANY)],
        out_specs=pl.BlockSpec((blk, d), lambda s, ib, ob, oh, *_: (ob[s], oh[s])),
        scratch_shapes=[pltpu.VMEM((d, fh), F32), pltpu.VMEM((d, fh), F32), pltpu.VMEM((fh, d), F32),
                        pltpu.VMEM((d, fh), BF16), pltpu.VMEM((d, fh), BF16), pltpu.VMEM((fh, d), BF16),
                        pltpu.SemaphoreType.DMA((3,))])
    return pl.pallas_call(
        functools.partial(_moe_ffn_kernel, layer=layer, fh=fh),
        grid_spec=grid_spec,
        out_shape=jax.ShapeDtypeStruct((n_rows, MOE_FF_SPLIT * d), BF16),
        compiler_params=_params("arbitrary"),
        name="moe_ffn",
    )(*tables, xs, w1, w3, w2)


def _combine_kernel(x_ref, o0_ref, o1_ref, gt_ref, g_ref, *rest, final):
    if final:
        fg_ref, o_ref = rest
    else:
        (o_ref,) = rest
    d = x_ref.shape[1]
    gt = gt_ref[...]
    y = (gt[:, 0:1] * (o0_ref[:, :d].astype(F32) + o0_ref[:, d:].astype(F32))
         + gt[:, 1:2] * (o1_ref[:, :d].astype(F32) + o1_ref[:, d:].astype(F32)))
    x = x_ref[...] + g_ref[...] * y
    if final:
        ms = jnp.mean(x * x, axis=-1, keepdims=True)
        x = x * lax.rsqrt(ms + EPS) * fg_ref[...]
    o_ref[...] = x


def _moe_combine(x, parts, gates, gate_mod, final_g, seq_starts):
    t, d = x.shape
    tm = _row_tile(t, 512, seq_starts)
    final = final_g is not None
    in_specs = ([pl.BlockSpec((tm, d), lambda i: (i, 0))]
                + [pl.BlockSpec((tm, MOE_FF_SPLIT * d), lambda i: (i, 0))] * len(parts)
                + [pl.BlockSpec((tm, ROUTER_PAD), lambda i: (i, 0)),
                   pl.BlockSpec((None, 1, d), lambda i: (_seq_index(i * tm, seq_starts), 0, 0))])
    args = [x, *parts, gates, gate_mod]
    if final:
        in_specs.append(pl.BlockSpec((1, d), lambda i: (0, 0)))
        args.append(final_g)
    return pl.pallas_call(
        functools.partial(_combine_kernel, final=final),
        grid=(t // tm,),
        in_specs=in_specs,
        out_specs=pl.BlockSpec((tm, d), lambda i: (i, 0)),
        out_shape=jax.ShapeDtypeStruct((t, d), F32),
        compiler_params=_params("arbitrary"),
        name="moe_combine_residual",
    )(*args)


def _ret_kernel(*refs, reverse, nchunks, seq_starts, seq_ends):
    rc = RET_CHUNK
    if reverse:
        (cd_ref, q_ref, kt_ref, v_ref, inner_ref, cross_ref, kvd_ref, of_ref, gate_ref, o_ref, st_scr) = refs
    else:
        (cd_ref, q_ref, kt_ref, v_ref, inner_ref, cross_ref, kvd_ref, o_ref, st_scr) = refs
    step = pl.program_id(0)
    if reverse:
        fresh = _is_any((nchunks - step) * rc, seq_ends)
    else:
        fresh = _is_any(step * rc, seq_starts)

    @pl.when(fresh)
    def _():
        st_scr[...] = jnp.zeros_like(st_scr)

    dk, dv = RET_QK_DIM, RET_V_DIM
    for h in range(RET_HEADS):
        q = q_ref[:, h * dk:(h + 1) * dk]
        kt = kt_ref[h * dk:(h + 1) * dk, :]
        v = v_ref[:, h * dv:(h + 1) * dv]
        s = jnp.dot(q, kt, preferred_element_type=F32) * inner_ref[h]
        st = st_scr[h]
        o = jnp.dot(s.astype(BF16), v, preferred_element_type=F32)
        o = o + jnp.dot(q, st.astype(BF16), preferred_element_type=F32) * cross_ref[h]
        ktd = (kt.astype(F32) * kvd_ref[h]).astype(BF16)
        st_scr[h] = st * cd_ref[h] + jnp.dot(ktd, v, preferred_element_type=F32)
        if reverse:
            o = o + of_ref[:, h * dv:(h + 1) * dv]
            ms = jnp.mean(o * o, axis=-1, keepdims=True)
            o = o * lax.rsqrt(ms + EPS) * _silu(gate_ref[:, h * dv:(h + 1) * dv].astype(F32))
        o_ref[:, h * dv:(h + 1) * dv] = o.astype(o_ref.dtype)


def _retention_direction(krt, proj, vcol0, gcol0, log_decay, seq_starts, seq_ends, reverse, o_fwd=None):
    t = proj.shape[0]
    rc = _tile(min(s2 - s1 for s1, s2 in zip(seq_starts, seq_ends)), RET_CHUNK)
    assert rc == RET_CHUNK
    nchunks = t // rc
    pos = jnp.arange(rc, dtype=F32)
    diff = pos[:, None] - pos[None, :]
    lg = log_decay.astype(F32)
    if reverse:
        mask = diff < 0
        dist = -diff
        cross = jnp.exp((rc - pos)[None, :, None] * lg[:, None, None])
        kvd = jnp.exp(pos[None, None, :] * lg[:, None, None])
    else:
        mask = diff >= 0
        dist = diff
        cross = jnp.exp((pos + 1.0)[None, :, None] * lg[:, None, None])
        kvd = jnp.exp((rc - 1.0 - pos)[None, None, :] * lg[:, None, None])
    inner = jnp.where(mask[None], jnp.exp(jnp.where(mask, dist, 0.0)[None] * lg[:, None, None]), 0.0)
    cross = jnp.broadcast_to(cross, (RET_HEADS, rc, RET_V_DIM))
    kvd = jnp.broadcast_to(kvd, (RET_HEADS, RET_QK_DIM, rc))
    chunk_decay = jnp.exp(rc * lg)

    def rb(i):
        return nchunks - 1 - i if reverse else i

    vb = vcol0 // RET_V_WIDTH
    assert vcol0 % RET_V_WIDTH == 0 and gcol0 % RET_V_WIDTH == 0
    in_specs = [pl.BlockSpec(memory_space=pltpu.SMEM),
                pl.BlockSpec((rc, RET_QK_WIDTH), lambda i: (rb(i), 0)),
                pl.BlockSpec((RET_QK_WIDTH, rc), lambda i: (0, rb(i))),
                pl.BlockSpec((rc, RET_V_WIDTH), lambda i: (rb(i), vb)),
                pl.BlockSpec((RET_HEADS, rc, rc), lambda i: (0, 0, 0)),
                pl.BlockSpec((RET_HEADS, rc, RET_V_DIM), lambda i: (0, 0, 0)),
                pl.BlockSpec((RET_HEADS, RET_QK_DIM, rc), lambda i: (0, 0, 0))]
    args = [chunk_decay, proj, krt, proj, inner, cross, kvd]
    if reverse:
        gb = gcol0 // RET_V_WIDTH
        in_specs += [pl.BlockSpec((rc, RET_V_WIDTH), lambda i: (rb(i), 0)),
                     pl.BlockSpec((rc, RET_V_WIDTH), lambda i: (rb(i), gb))]
        args += [o_fwd, proj]
    return pl.pallas_call(
        functools.partial(_ret_kernel, reverse=reverse, nchunks=nchunks,
                          seq_starts=seq_starts, seq_ends=seq_ends),
        grid=(nchunks,),
        in_specs=in_specs,
        out_specs=pl.BlockSpec((rc, RET_V_WIDTH), lambda i: (rb(i), 0)),
        out_shape=jax.ShapeDtypeStruct((t, RET_V_WIDTH), BF16 if reverse else F32),
        scratch_shapes=[pltpu.VMEM((RET_HEADS, RET_QK_DIM, RET_V_DIM), F32)],
        compiler_params=_params("arbitrary"),
        name="retention_bwd_norm_gate" if reverse else "retention_fwd",
    )(*args)


def _ssd_diff_layer(x, a1, sh1, g1, p, i, layer, lay):
    seq_starts, seq_ends, groups = lay
    d = x.shape[1]
    w_in = p['ab_w_in'][i]
    o2 = SSD_WIDTH + SSD_XBC
    o3 = o2 + 2 * SSD_HEADS
    w_main = jnp.concatenate([w_in[:, :o2], w_in[:, o3:]], axis=1).astype(BF16)
    w_dt = jnp.pad(w_in[:, o2:o3], ((0, 0), (0, DT_PAD - 2 * SSD_HEADS))).astype(BF16)
    qcol = o2
    kcol = qcol + DA_QK_WIDTH
    vcol = kcol + DA_QK_WIDTH
    lmax = max(g[2] for g in groups)
    tables = (*_rope_tables(lmax, DA_HEAD_DIM, DA_HEAD_DIM ** -0.5 * math.log2(math.e)),
              *_rope_tables(lmax, DA_HEAD_DIM, 1.0))
    tn = 1024
    assert qcol % tn == 0 and kcol % tn == 0 and vcol % tn == 0
    proj, dt = _norm_mod_matmul(x, a1, sh1, w_main, w_dt, seq_starts,
                                ((qcol // tn, kcol // tn), (kcol // tn, vcol // tn)), DA_HEAD_DIM, tables)

    xact = _conv_silu(proj, SSD_WIDTH, p['ssd_conv_w'][i], p['ssd_conv_b'][i], seq_starts, seq_ends)
    bt = xact[:, SSD_WIDTH:SSD_WIDTH + SSD_BC].T
    dtt = dt.T
    expand = (jnp.arange(SSD_WIDTH)[None, :] // SSD_HEAD_DIM == jnp.arange(SSD_HEADS)[:, None]).astype(F32)
    a_f = -jnp.exp(p['ssd_a_log_fwd'][i].astype(F32))
    a_b = -jnp.exp(p['ssd_a_log_bwd'][i].astype(F32))
    yf = _ssd_direction(xact, bt, dt, dtt, p['ssd_dt_bias_fwd'][i].astype(F32), a_f, expand,
                        seq_starts, seq_ends, reverse=False)
    dskip = jnp.repeat(p['ssd_d'][i].astype(F32), SSD_HEAD_DIM).reshape(1, SSD_WIDTH)
    y_ssd = _ssd_direction(xact, bt, dt, dtt, p['ssd_dt_bias_bwd'][i].astype(F32), a_b, expand,
                           seq_starts, seq_ends, reverse=True,
                           final=(proj, yf, dskip, p['ssd_norm_g'][i].astype(F32).reshape(1, SSD_WIDTH)))

    lam_init = 0.8 - 0.6 * math.exp(-0.3 * layer)
    lam = (jnp.exp(jnp.sum(p['da_lambda_q1'][i].astype(F32) * p['da_lambda_k1'][i].astype(F32)))
           - jnp.exp(jnp.sum(p['da_lambda_q2'][i].astype(F32) * p['da_lambda_k2'][i].astype(F32)))
           + lam_init).reshape(1)
    subln = p['da_subln_g'][i].astype(F32).reshape(1, 2 * DA_HEAD_DIM)
    o = _diff_attention(proj, qcol, kcol, vcol, lam, subln, 1.0 - lam_init, seq_starts, seq_ends)

    w_out = p['ab_w_out'][i].astype(BF16)
    return _outproj_residual(y_ssd, 0, o, 0, w_out, x, g1, seq_starts)


def _retention_layer(x, a1, sh1, g1, p, j, lay):
    seq_starts, seq_ends, groups = lay
    w_in = p['ret_w_in'][j].astype(BF16)
    lmax = max(g[2] for g in groups)
    tables = (*_rope_tables(lmax, RET_QK_DIM, 1.0), *_rope_tables(lmax, RET_QK_DIM, RET_QK_DIM ** -0.5))
    tn = 1024
    kcol = RET_QK_WIDTH
    vcol = 2 * RET_QK_WIDTH
    gcol = vcol + RET_V_WIDTH
    proj = _norm_mod_matmul(x, a1, sh1, w_in, None, seq_starts,
                            ((0, kcol // tn), (kcol // tn, vcol // tn)), RET_QK_DIM, tables)
    krt = proj[:, kcol:vcol].T
    lg_f = -jnp.exp(p['ret_log_decay_fwd'][j].astype(F32))
    lg_b = -jnp.exp(p['ret_log_decay_bwd'][j].astype(F32))
    of = _retention_direction(krt, proj, vcol, gcol, lg_f, seq_starts, seq_ends, reverse=False)
    o = _retention_direction(krt, proj, vcol, gcol, lg_b, seq_starts, seq_ends, reverse=True, o_fwd=of)
    w_out = p['ret_w_out'][j].astype(BF16)
    return _outproj_residual(o, 0, o, 1, w_out, x, g1, seq_starts)


def _route(logits):
    coarse = logits[:, :MOE_GROUPS]
    grp = jnp.argmax(coarse, axis=-1)
    p_grp = jnp.max(jax.nn.softmax(coarse, axis=-1), axis=-1)
    fine = logits[:, MOE_GROUPS:MOE_GROUPS + MOE_EXPERTS].reshape(-1, MOE_GROUPS, MOE_EXPERTS_PER_GROUP)
    fine = jnp.take_along_axis(fine, grp[:, None, None], axis=1)[:, 0]
    top_v, top_i = lax.top_k(fine, MOE_TOP_K)
    gates = jax.nn.softmax(top_v, axis=-1) * p_grp[:, None]
    experts = (grp[:, None] * MOE_EXPERTS_PER_GROUP + top_i).astype(jnp.int32)
    return experts, gates


def _moe_layer(x, a2, sh2, g2, p, layer, lay, final_g):
    seq_starts, _, _ = lay
    t, d = x.shape
    wr = jnp.concatenate([p['moe_w_group'][layer], p['moe_w_expert'][layer]], axis=1).astype(F32)
    br = jnp.concatenate([p['moe_b_group'][layer], p['moe_b_expert'][layer]]).astype(F32)
    nr = wr.shape[1]
    wr = jnp.pad(wr, ((0, 0), (0, ROUTER_PAD - nr)))
    br = jnp.pad(br, (0, ROUTER_PAD - nr)).reshape(1, ROUTER_PAD)
    h, logits = _norm_mod_router(x, a2, sh2, wr, br, seq_starts)
    experts, gates = _route(logits)

    blk = MOE_ROW_BLOCK
    n_assign = t * MOE_TOP_K
    e_flat = experts.reshape(n_assign)
    onehot = (e_flat[:, None] == jnp.arange(MOE_EXPERTS, dtype=jnp.int32)[None, :]).astype(jnp.int32)
    csum = jnp.cumsum(onehot, axis=0)
    counts = csum[-1]
    rank = jnp.sum(csum * onehot, axis=1) - 1
    padded = (counts + blk - 1) // blk * blk
    ends_pad = jnp.cumsum(padded)
    start_pad = ends_pad - padded
    dest = (start_pad[e_flat] + rank).astype(jnp.int32)
    n_rows = n_assign + MOE_EXPERTS * blk
    n_blocks = n_rows // blk
    tok_flat = jnp.repeat(jnp.arange(t, dtype=jnp.int32), MOE_TOP_K)
    row_tok = (jnp.arange(n_rows, dtype=jnp.int32) % t).at[dest].set(tok_flat)
    nblk_e = (padded // blk).astype(jnp.int32)
    ends_blk = jnp.cumsum(nblk_e)
    start_blk = ends_blk - nblk_e
    n_items = (MOE_FF_SPLIT * ends_blk[-1]).astype(jnp.int32)
    item = jnp.minimum(jnp.arange(MOE_FF_SPLIT * n_blocks, dtype=jnp.int32), n_items - 1)
    item_expert = jnp.minimum(jnp.sum((MOE_FF_SPLIT * ends_blk)[None, :] <= item[:, None], axis=1),
                              MOE_EXPERTS - 1).astype(jnp.int32)
    local = item - MOE_FF_SPLIT * start_blk[item_expert]
    per_half = jnp.maximum(nblk_e[item_expert], 1)
    item_half = (local // per_half).astype(jnp.int32)
    item_block = (start_blk[item_expert] + local % per_half).astype(jnp.int32)
    changed = jnp.logical_or(item_expert[1:] != item_expert[:-1], item_half[1:] != item_half[:-1])
    item_new = jnp.concatenate([jnp.ones((1,), jnp.int32), changed.astype(jnp.int32)])
    n_slots = item.shape[0]
    pos = jnp.arange(n_slots, dtype=jnp.int32)
    run_start = jnp.where(item_new == 1, pos, n_slots)
    next_start = lax.cummin(jnp.concatenate([run_start[1:], jnp.full((1,), n_slots, jnp.int32)]), reverse=True)
    has_next = (next_start < n_slots).astype(jnp.int32)
    next_idx = jnp.minimum(next_start, n_slots - 1)
    spare = jnp.maximum(pos - n_items, 0)
    out_block = jnp.where(pos < n_items, item_block, ends_blk[-1] + spare // MOE_FF_SPLIT).astype(jnp.int32)
    out_half = jnp.where(pos < n_items, item_half, spare % MOE_FF_SPLIT).astype(jnp.int32)
    tables = (item_block, out_block, out_half, item_half, item_expert, item_new,
              item_expert[next_idx], item_half[next_idx], has_next, n_items.reshape(1))

    xs = jnp.take(h, row_tok, axis=0, mode='clip')
    out = _moe_ffn(xs, tables, p['moe_w1'], p['moe_w3'], p['moe_w2'], layer)
    dest2 = dest.reshape(t, MOE_TOP_K)
    parts = [jnp.take(out, dest2[:, k], axis=0, mode='clip') for k in range(MOE_TOP_K)]
    gates_pad = jnp.pad(gates, ((0, 0), (0, ROUTER_PAD - MOE_TOP_K)))
    return _moe_combine(x, parts, gates_pad, g2, final_g, seq_starts)


def kernel(x_prompt, x_sample, c_prompt, c_sample, ada_w, ada_b, norm_mix_g, norm_ffn_g, ab_w_in, ssd_conv_w, ssd_conv_b, ssd_a_log_fwd, ssd_a_log_bwd, ssd_dt_bias_fwd, ssd_dt_bias_bwd, ssd_d, ssd_norm_g, da_lambda_q1, da_lambda_k1, da_lambda_q2, da_lambda_k2, da_subln_g, ab_w_out, ret_w_in, ret_log_decay_fwd, ret_log_decay_bwd, ret_w_out, moe_w_group, moe_b_group, moe_w_expert, moe_b_expert, moe_w1, moe_w3, moe_w2, final_norm_g):
    p = dict(ab_w_in=ab_w_in, ssd_conv_w=ssd_conv_w, ssd_conv_b=ssd_conv_b,
             ssd_a_log_fwd=ssd_a_log_fwd, ssd_a_log_bwd=ssd_a_log_bwd,
             ssd_dt_bias_fwd=ssd_dt_bias_fwd, ssd_dt_bias_bwd=ssd_dt_bias_bwd, ssd_d=ssd_d,
             ssd_norm_g=ssd_norm_g, da_lambda_q1=da_lambda_q1, da_lambda_k1=da_lambda_k1,
             da_lambda_q2=da_lambda_q2, da_lambda_k2=da_lambda_k2, da_subln_g=da_subln_g,
             ab_w_out=ab_w_out, ret_w_in=ret_w_in, ret_log_decay_fwd=ret_log_decay_fwd,
             ret_log_decay_bwd=ret_log_decay_bwd, ret_w_out=ret_w_out, moe_w_group=moe_w_group,
             moe_b_group=moe_b_group, moe_w_expert=moe_w_expert, moe_b_expert=moe_b_expert,
             moe_w1=moe_w1, moe_w3=moe_w3, moe_w2=moe_w2)
    bp, lp, d = x_prompt.shape
    bs, ls, _ = x_sample.shape
    depth = ada_w.shape[0]
    groups = ((0, bp, lp), (bp * lp, bs, ls))
    seq_starts = tuple(r0 + b * l for (r0, nb, l) in groups for b in range(nb))
    seq_ends = tuple(r0 + (b + 1) * l for (r0, nb, l) in groups for b in range(nb))
    lay = (seq_starts, seq_ends, groups)
    nseq = len(seq_starts)
    x = jnp.concatenate([x_prompt.reshape(bp * lp, d), x_sample.reshape(bs * ls, d)], axis=0)
    c = jnp.concatenate([c_prompt, c_sample], axis=0).astype(F32)
    c_pad = jnp.pad(c, ((0, -nseq % 8), (0, 0)))
    mod = _ada_modulation(c_pad, ada_w.astype(F32), ada_b.astype(F32))[:, :nseq]

    for layer in range(depth):
        sh1, sc1, g1, sh2, sc2, g2 = [m.reshape(nseq, 1, d) for m in jnp.split(mod[layer], 6, axis=-1)]
        a1 = norm_mix_g[layer].astype(F32)[None, None, :] * (1.0 + sc1)
        a2 = norm_ffn_g[layer].astype(F32)[None, None, :] * (1.0 + sc2)
        if layer % 2 == 0:
            x = _ssd_diff_layer(x, a1, sh1, g1, p, layer // 2, layer, lay)
        else:
            x = _retention_layer(x, a1, sh1, g1, p, layer // 2, lay)
        final_g = final_norm_g.astype(F32).reshape(1, d) if layer == depth - 1 else None
        x = _moe_layer(x, a2, sh2, g2, p, layer, lay, final_g)
    y_prompt = x[:bp * lp].reshape(bp, lp, d)
    y_sample = x[bp * lp:].reshape(bs, ls, d)
    return (y_prompt, y_sample)
```

```python
import functools
import math

import jax
import jax.numpy as jnp
from jax import lax
from jax.experimental import pallas as pl
from jax.experimental.pallas import tpu as pltpu

F32 = jnp.float32
BF16 = jnp.bfloat16
HIGHEST = lax.Precision.HIGHEST

EPS = 1e-6
SUBLN_EPS = 1e-5
ROPE_THETA = 10000.0

SSD_HEADS = 32
SSD_HEAD_DIM = 64
SSD_WIDTH = SSD_HEADS * SSD_HEAD_DIM
SSD_GROUPS = 4
SSD_STATE = 128
SSD_CONV = 5
SSD_BC = SSD_GROUPS * SSD_STATE
SSD_XBC = SSD_WIDTH + 2 * SSD_BC
SSD_CHUNK = 128
DA_HEADS = 8
DA_HEAD_DIM = 128
DA_QK_WIDTH = 2 * DA_HEADS * DA_HEAD_DIM
DA_V_WIDTH = DA_HEADS * 2 * DA_HEAD_DIM
RET_HEADS = 8
RET_QK_DIM = 256
RET_V_DIM = 512
RET_QK_WIDTH = RET_HEADS * RET_QK_DIM
RET_V_WIDTH = RET_HEADS * RET_V_DIM
RET_CHUNK = 256
MOE_GROUPS = 4
MOE_EXPERTS_PER_GROUP = 8
MOE_EXPERTS = MOE_GROUPS * MOE_EXPERTS_PER_GROUP
MOE_TOP_K = 2
MOE_ROW_BLOCK = 512
MOE_FF_SPLIT = 2
ROUTER_PAD = 128
DT_PAD = 128
HALO = 16

VMEM_LIMIT_BYTES = 56 * 1024 * 1024


def _params(*semantics):
    return pltpu.CompilerParams(dimension_semantics=semantics, vmem_limit_bytes=VMEM_LIMIT_BYTES)


def _silu(x):
    return x * jax.nn.sigmoid(x)


def _softplus(x):
    return jnp.maximum(x, 0.0) + jnp.log1p(jnp.exp(-jnp.abs(x)))


def _lane_repeat(x, n):
    return x if n == 1 else jnp.concatenate([x] * n, axis=1)


def _tile(n, pref):
    t = min(n, pref)
    assert n % t == 0, (n, pref)
    return t


def _row_tile(t, pref, seq_starts):
    tm = min(t, pref)
    while t % tm or any(st % tm for st in seq_starts):
        tm //= 2
    return tm


def _seq_index(row, seq_starts):
    s = 0
    for st in seq_starts[1:]:
        s = s + jnp.where(row >= st, 1, 0)
    return s


def _is_any(row, values):
    hit = row == values[0]
    for v in values[1:]:
        hit = jnp.logical_or(hit, row == v)
    return hit


def _ada_kernel(c_ref, w_ref, b_ref, o_ref):
    o_ref[...] = jnp.dot(_silu(c_ref[...]), w_ref[...], preferred_element_type=F32,
                         precision=HIGHEST) + b_ref[...]


def _ada_modulation(c_pad, ada_w, ada_b):
    depth, d, n = ada_w.shape
    rows = c_pad.shape[0]
    tn = _tile(n, 1024)
    return pl.pallas_call(
        _ada_kernel,
        grid=(depth, n // tn),
        in_specs=[pl.BlockSpec((rows, d), lambda l, j: (0, 0)),
                  pl.BlockSpec((None, d, tn), lambda l, j: (l, 0, j)),
                  pl.BlockSpec((None, 1, tn), lambda l, j: (l, 0, j))],
        out_specs=pl.BlockSpec((None, rows, tn), lambda l, j: (l, 0, j)),
        out_shape=jax.ShapeDtypeStruct((depth, rows, n), F32),
        compiler_params=_params("arbitrary", "arbitrary"),
        name="ada_modulation",
    )(c_pad, ada_w, ada_b.reshape(depth, 1, n))


def _norm_mod(x, a, sh):
    ms = jnp.mean(x * x, axis=-1, keepdims=True)
    return (x * lax.rsqrt(ms + EPS)) * a + sh


def _rotate_half_tile(x, cos, sin, head_dim):
    half = head_dim // 2
    outs = []
    for h in range(x.shape[1] // head_dim):
        xh = x[:, h * head_dim:(h + 1) * head_dim]
        if half % 128 == 0:
            rot = jnp.concatenate([xh[:, half:], xh[:, :half]], axis=-1)
        else:
            rot = pltpu.roll(xh, half, 1)
        outs.append(xh * cos + rot * sin)
    return jnp.concatenate(outs, axis=-1)


def _row_part_specs(parts, tm, width, row_of, col_of, single_buffer=False):
    specs, bounds = [], [0]
    mode = dict(pipeline_mode=pl.Buffered(1)) if single_buffer else {}
    for arr in parts:
        b0, nb = bounds[-1], arr.shape[0] // tm
        assert arr.shape[0] % tm == 0
        specs.append(pl.BlockSpec(
            (tm, width), lambda *g, b0=b0, nb=nb: (jnp.clip(row_of(*g) - b0, 0, nb - 1), col_of(*g)), **mode))
        bounds.append(b0 + nb)
    return specs, tuple(bounds)


def _for_row_part(i, bounds, fn):
    if len(bounds) == 2:
        fn(0)
        return
    for k in range(len(bounds) - 1):
        pl.when(jnp.logical_and(i >= bounds[k], i < bounds[k + 1]))(functools.partial(fn, k))


def _nmm_kernel(*refs, with_dt, rope_tiles, head_dim, bounds):
    n_parts = len(bounds) - 1
    x_refs = refs[:n_parts]
    a_ref, sh_ref, w_ref, cq_ref, sq_ref, ck_ref, sk_ref = refs[n_parts:n_parts + 7]
    if with_dt:
        wdt_ref, o_ref, odt_ref, h_scr = refs[n_parts + 7:]
    else:
        o_ref, h_scr = refs[n_parts + 7:]
    j = pl.program_id(1)

    def normalize(k):
        hb = _norm_mod(x_refs[k][...], a_ref[...], sh_ref[...]).astype(BF16)
        h_scr[...] = hb
        if with_dt:
            odt_ref[...] = jnp.dot(hb, wdt_ref[...], preferred_element_type=F32)

    @pl.when(j == 0)
    def _():
        _for_row_part(pl.program_id(0), bounds, normalize)

    acc = jnp.dot(h_scr[...], w_ref[...], preferred_element_type=F32)
    (q_lo, q_hi), (k_lo, k_hi) = rope_tiles
    is_q = jnp.logical_and(j >= q_lo, j < q_hi)
    is_k = jnp.logical_and(j >= k_lo, j < k_hi)

    @pl.when(is_q)
    def _():
        o_ref[...] = _rotate_half_tile(acc, cq_ref[...], sq_ref[...], head_dim).astype(o_ref.dtype)

    @pl.when(is_k)
    def _():
        o_ref[...] = _rotate_half_tile(acc, ck_ref[...], sk_ref[...], head_dim).astype(o_ref.dtype)

    @pl.when(jnp.logical_not(jnp.logical_or(is_q, is_k)))
    def _():
        o_ref[...] = acc.astype(o_ref.dtype)


def _norm_mod_matmul(x_parts, a, sh, w, wdt, seq_starts, rope_tiles, head_dim, tables):
    t = sum(xp.shape[0] for xp in x_parts)
    d = x_parts[0].shape[1]
    n = w.shape[1]
    tm = _row_tile(t, 1024, seq_starts)
    tn = _tile(n, 1024)
    with_dt = wdt is not None

    def seq_map(i, j):
        return (_seq_index(i * tm, seq_starts), 0, 0)

    def pos_map(i, j):
        r = i * tm
        s = _seq_index(r, seq_starts)
        st = 0
        for k, v in enumerate(seq_starts):
            st = st + jnp.where(s == k, v, 0)
        return ((r - st) // tm, 0)

    x_specs, bounds = _row_part_specs(x_parts, tm, d, lambda i, j: i, lambda i, j: 0,
                                      single_buffer=len(x_parts) > 1)
    in_specs = x_specs + [pl.BlockSpec((None, 1, d), seq_map),
                          pl.BlockSpec((None, 1, d), seq_map),
                          pl.BlockSpec((d, tn), lambda i, j: (0, j))] + [pl.BlockSpec((tm, head_dim), pos_map)] * 4
    out_specs = [pl.BlockSpec((tm, tn), lambda i, j: (i, j))]
    out_shape = [jax.ShapeDtypeStruct((t, n), BF16)]
    args = [*x_parts, a, sh, w, *tables]
    if with_dt:
        in_specs.append(pl.BlockSpec((d, DT_PAD), lambda i, j: (0, 0)))
        out_specs.append(pl.BlockSpec((tm, DT_PAD), lambda i, j: (i, 0)))
        out_shape.append(jax.ShapeDtypeStruct((t, DT_PAD), F32))
        args.append(wdt)
    res = pl.pallas_call(
        functools.partial(_nmm_kernel, with_dt=with_dt, rope_tiles=rope_tiles, head_dim=head_dim, bounds=bounds),
        grid=(t // tm, n // tn),
        in_specs=in_specs, out_specs=out_specs, out_shape=out_shape,
        scratch_shapes=[pltpu.VMEM((tm, d), BF16)],
        compiler_params=_params("arbitrary", "arbitrary"),
        name="norm_mod_matmul",
    )(*args)
    return res if with_dt else res[0]


def _conv_kernel(cur_ref, prev_ref, next_ref, w_ref, b_ref, o_ref, ext_scr, *, tm, seq_starts, seq_ends):
    r0 = pl.program_id(0) * tm
    at_start = _is_any(r0, seq_starts)
    at_end = _is_any(r0 + tm, seq_ends)
    prev = prev_ref[...].astype(F32)[HALO - 8:HALO]
    nxt = next_ref[...].astype(F32)[0:8]
    ext_scr[0:8, :] = jnp.where(at_start, 0.0, prev)
    ext_scr[8:8 + tm, :] = cur_ref[...].astype(F32)
    ext_scr[8 + tm:16 + tm, :] = jnp.where(at_end, 0.0, nxt)
    pad = SSD_CONV // 2
    acc = b_ref[...] + w_ref[0:1, :] * ext_scr[8 - pad:8 - pad + tm, :]
    for k in range(1, SSD_CONV):
        acc = acc + w_ref[k:k + 1, :] * ext_scr[8 - pad + k:8 - pad + k + tm, :]
    o_ref[...] = _silu(acc).astype(o_ref.dtype)


def _conv_silu(proj, col0, conv_w, conv_b, seq_starts, seq_ends):
    t = proj.shape[0]
    tm = _row_tile(t, 512, seq_starts)
    tc = 1024
    assert col0 % tc == 0 and SSD_XBC % tc == 0 and tm % HALO == 0
    cb = col0 // tc
    hb = tm // HALO
    last_halo = t // HALO - 1
    return pl.pallas_call(
        functools.partial(_conv_kernel, tm=tm, seq_starts=seq_starts, seq_ends=seq_ends),
        grid=(t // tm, SSD_XBC // tc),
        in_specs=[pl.BlockSpec((tm, tc), lambda i, j: (i, cb + j)),
                  pl.BlockSpec((HALO, tc), lambda i, j: (jnp.maximum(i * hb - 1, 0), cb + j)),
                  pl.BlockSpec((HALO, tc), lambda i, j: (jnp.minimum((i + 1) * hb, last_halo), cb + j)),
                  pl.BlockSpec((SSD_CONV, tc), lambda i, j: (0, j)),
                  pl.BlockSpec((1, tc), lambda i, j: (0, j))],
        out_specs=pl.BlockSpec((tm, tc), lambda i, j: (i, j)),
        out_shape=jax.ShapeDtypeStruct((t, SSD_XBC), BF16),
        scratch_shapes=[pltpu.VMEM((tm + 16, tc), F32)],
        compiler_params=_params("arbitrary", "arbitrary"),
        name="ssd_conv_silu",
    )(proj, proj, proj, conv_w, conv_b.reshape(1, SSD_XBC))


def _ssd_kernel(*refs, reverse, dcol, nchunks, seq_starts, seq_ends):
    ch = SSD_CHUNK
    if reverse:
        (xact_ref, bt_ref, dt_ref, dtt_ref, bias_ref, a_ref, biast_ref, at_ref, e_ref,
         z_ref, yf_ref, dskip_ref, g_ref, o_ref, h_scr, y_scr) = refs
    else:
        (xact_ref, bt_ref, dt_ref, dtt_ref, bias_ref, a_ref, biast_ref, at_ref, e_ref,
         o_ref, h_scr) = refs
    step = pl.program_id(0)
    if reverse:
        row_end = (nchunks - step) * ch
        fresh = _is_any(row_end, seq_ends)
    else:
        fresh = _is_any(step * ch, seq_starts)

    @pl.when(fresh)
    def _():
        h_scr[...] = jnp.zeros_like(h_scr)

    dt = _softplus(dt_ref[:, dcol:dcol + SSD_HEADS] + bias_ref[...])
    dta = dt * a_ref[...]
    dtt = _softplus(dtt_ref[dcol:dcol + SSD_HEADS, :] + biast_ref[...])
    dtat = dtt * at_ref[...]
    row = lax.broadcasted_iota(jnp.int32, (ch, ch), 0)
    col = lax.broadcasted_iota(jnp.int32, (ch, ch), 1)
    if reverse:
        keep = col >= row
    else:
        keep = col <= row
    tri = jnp.where(keep, 1.0, 0.0).astype(F32)
    trit = jnp.where(keep, 0.0, 1.0).astype(F32) + jnp.where(row == col, 1.0, 0.0).astype(F32)
    cs = jnp.dot(tri, dta, preferred_element_type=F32, precision=HIGHEST)
    cst = jnp.dot(dtat, trit, preferred_element_type=F32, precision=HIGHEST)
    last = 0 if reverse else ch - 1
    ecs = jnp.exp(cs)
    wt = dtt * jnp.exp(cst[:, last:last + 1] - cst)
    cdec_x = jnp.dot(jnp.broadcast_to(ecs[last:last + 1, :], (8, SSD_HEADS)), e_ref[...],
                     preferred_element_type=F32, precision=HIGHEST)[0:1, :]

    lane = lax.broadcasted_iota(jnp.int32, (1, 2 * SSD_HEAD_DIM), 1)
    first = lane < SSD_HEAD_DIM
    gw = SSD_WIDTH // SSD_GROUPS
    hpg = SSD_HEADS // SSD_GROUPS
    pw = 2 * SSD_HEAD_DIM
    ssq = jnp.zeros((ch, pw), F32)
    for g in range(SSD_GROUPS):
        bgt = bt_ref[g * SSD_STATE:(g + 1) * SSD_STATE, :]
        c0 = SSD_WIDTH + SSD_BC + g * SSD_STATE
        cg = xact_ref[:, c0:c0 + SSD_STATE]
        scores = jnp.dot(cg, bgt, preferred_element_type=F32)
        cgf = cg.astype(F32)
        bgtf = bgt.astype(F32)
        for pr in range(hpg // 2):
            lo = g * gw + pr * pw
            hi = lo + pw
            xpair = xact_ref[:, lo:hi].astype(F32)
            hpair = h_scr[g, :, pr * pw:(pr + 1) * pw]
            xsel = (jnp.where(first, xpair, 0.0).astype(BF16), jnp.where(first, 0.0, xpair).astype(BF16))
            hsel = (jnp.where(first, hpair, 0.0).astype(BF16), jnp.where(first, 0.0, hpair).astype(BF16))
            y = None
            bws = []
            for u in range(2):
                h = g * hpg + 2 * pr + u
                csb = jnp.broadcast_to(cs[:, h:h + 1], (ch, ch))
                seg = jnp.where(keep, csb - cst[h:h + 1, :], -jnp.inf)
                m = (scores * jnp.exp(seg) * dtt[h:h + 1, :]).astype(BF16)
                ce = (cgf * jnp.exp(csb)).astype(BF16)
                part = jnp.dot(jnp.concatenate([m, ce], axis=1), jnp.concatenate([xsel[u], hsel[u]], axis=0),
                               preferred_element_type=F32)
                y = part if y is None else y + part
                bws.append((bgtf * wt[h:h + 1, :]).astype(BF16))
            h_scr[g, :, pr * pw:(pr + 1) * pw] = hpair * cdec_x[:, lo:hi] + jnp.dot(
                jnp.concatenate(bws, axis=1), jnp.concatenate(xsel, axis=0), preferred_element_type=F32)
            if reverse:
                y = y + yf_ref[:, lo:hi].astype(F32) + dskip_ref[:, lo:hi] * xpair
                y = y * _silu(z_ref[:, lo:hi].astype(F32))
                ssq = ssq + y * y
                y_scr[:, lo:hi] = y
            else:
                o_ref[:, lo:hi] = y.astype(o_ref.dtype)
    if reverse:
        inv = lax.rsqrt(jnp.sum(ssq, axis=-1, keepdims=True) * (1.0 / SSD_WIDTH) + EPS)
        o_ref[...] = (y_scr[...] * inv * g_ref[...]).astype(o_ref.dtype)


def _ssd_direction(xact, bt, dt, dtt, bias, a, expand, seq_starts, seq_ends, reverse, final=None):
    t = xact.shape[0]
    ch = SSD_CHUNK
    nchunks = t // ch
    dcol = SSD_HEADS if reverse else 0

    def rb(i):
        return nchunks - 1 - i if reverse else i

    in_specs = [pl.BlockSpec((ch, SSD_XBC), lambda i: (rb(i), 0)),
                pl.BlockSpec((SSD_BC, ch), lambda i: (0, rb(i))),
                pl.BlockSpec((ch, DT_PAD), lambda i: (rb(i), 0)),
                pl.BlockSpec((DT_PAD, ch), lambda i: (0, rb(i))),
                pl.BlockSpec((1, SSD_HEADS), lambda i: (0, 0)),
                pl.BlockSpec((1, SSD_HEADS), lambda i: (0, 0)),
                pl.BlockSpec((SSD_HEADS, 1), lambda i: (0, 0)),
                pl.BlockSpec((SSD_HEADS, 1), lambda i: (0, 0)),
                pl.BlockSpec((SSD_HEADS, SSD_WIDTH), lambda i: (0, 0))]
    args = [xact, bt, dt, dtt, bias.reshape(1, -1), a.reshape(1, -1), bias.reshape(-1, 1), a.reshape(-1, 1), expand]
    scratch = [pltpu.VMEM((SSD_GROUPS, SSD_STATE, SSD_WIDTH // SSD_GROUPS), F32)]
    if reverse:
        proj, yf, dskip, g = final
        in_specs += [pl.BlockSpec((ch, SSD_WIDTH), lambda i: (rb(i), 0)),
                     pl.BlockSpec((ch, SSD_WIDTH), lambda i: (rb(i), 0)),
                     pl.BlockSpec((1, SSD_WIDTH), lambda i: (0, 0)),
                     pl.BlockSpec((1, SSD_WIDTH), lambda i: (0, 0))]
        args += [proj, yf, dskip, g]
        scratch.append(pltpu.VMEM((ch, SSD_WIDTH), F32))
    return pl.pallas_call(
        functools.partial(_ssd_kernel, reverse=reverse, dcol=dcol, nchunks=nchunks,
                          seq_starts=seq_starts, seq_ends=seq_ends),
        grid=(nchunks,),
        in_specs=in_specs,
        out_specs=pl.BlockSpec((ch, SSD_WIDTH), lambda i: (rb(i), 0)),
        out_shape=jax.ShapeDtypeStruct((t, SSD_WIDTH), BF16),
        scratch_shapes=scratch,
        compiler_params=_params("arbitrary"),
        name="ssd_bwd_gate_norm" if reverse else "ssd_fwd",
    )(*args)


def _rope_tables(lmax, head_dim, scale):
    half = head_dim // 2
    inv = 1.0 / (ROPE_THETA ** (jnp.arange(half, dtype=F32) / half))
    ang = jnp.arange(lmax, dtype=F32)[:, None] * inv[None, :]
    cos = jnp.cos(ang) * scale
    sin = jnp.sin(ang) * scale
    return jnp.concatenate([cos, cos], axis=-1), jnp.concatenate([-sin, sin], axis=-1)


def _seq_lookup(row, seq_starts, values):
    s = _seq_index(row, seq_starts)
    out = 0
    for k, v in enumerate(values):
        out = out + jnp.where(s == k, v, 0)
    return out


def _attn_kernel(lam_ref, q_ref, k_ref, v_ref, g_ref, o_ref, m_scr, l_scr, acc_scr, s_scr, p_scr, alpha_scr, *,
                 tk, nkv, unroll, strip, out_scale, seq_starts, seq_kv_blocks, seq_sub_block):
    dh = DA_HEAD_DIM
    tq = q_ref.shape[0]
    lanes = m_scr.shape[-1]
    row0 = pl.program_id(1) * tq
    n_valid = _seq_lookup(row0, seq_starts, seq_kv_blocks)
    sub0 = _seq_lookup(row0, seq_starts, seq_sub_block)
    m_scr[...] = jnp.full_like(m_scr, -jnp.inf)
    l_scr[...] = jnp.zeros_like(l_scr)
    acc_scr[...] = jnp.zeros_like(acc_scr)

    def chunk(c):
        sub = sub0 + c // nkv
        off = pl.multiple_of((c % nkv) * tk, tk)
        for u in range(2):
            s_scr[u] = lax.dot_general(q_ref[:, u * dh:(u + 1) * dh],
                                       k_ref[sub, pl.ds(off, tk), u * dh:(u + 1) * dh],
                                       (((1,), (1,)), ((), ())), preferred_element_type=F32)
        for u in range(2):
            for r in range(tq // strip):
                rows = slice(r * strip, (r + 1) * strip)
                s = s_scr[u, rows, :]
                m_prev = m_scr[u, rows, :]
                m_new = jnp.maximum(m_prev, jnp.max(s, axis=-1, keepdims=True))
                alpha = jnp.exp2(m_prev - m_new)
                p = jnp.exp2(s - _lane_repeat(m_new, tk // lanes))
                l_scr[u, rows, :] = alpha * l_scr[u, rows, :] + jnp.sum(p, axis=-1, keepdims=True)
                m_scr[u, rows, :] = m_new
                alpha_scr[u, rows, :] = alpha
                p_scr[u, rows, :] = p.astype(BF16)
            pv = jnp.dot(p_scr[u], v_ref[sub, pl.ds(off, tk), :], preferred_element_type=F32)
            acc_scr[u] = acc_scr[u] * _lane_repeat(alpha_scr[u], acc_scr.shape[-1] // lanes) + pv

    def body(j, carry):
        for w in range(unroll):
            chunk(j * unroll + w)
        return carry

    lax.fori_loop(0, n_valid * (nkv // unroll), body, 0)
    lam = lam_ref[0]
    rep = acc_scr.shape[-1] // lanes
    o = (acc_scr[0] / _lane_repeat(l_scr[0], rep)
         - lam * (acc_scr[1] / _lane_repeat(l_scr[1], rep)))
    ms = jnp.mean(o * o, axis=-1, keepdims=True)
    o_ref[...] = (o * lax.rsqrt(ms + SUBLN_EPS) * g_ref[...] * out_scale).astype(o_ref.dtype)


def _diff_attention(proj, qcol0, kcol0, vcol0, lam, subln_g, out_scale, seq_starts, seq_ends):
    t = proj.shape[0]
    pw = 2 * DA_HEAD_DIM
    lens = tuple(e - s for s, e in zip(seq_starts, seq_ends))
    kvb = min(lens)
    group = max(lens) // kvb
    assert all(n % kvb == 0 for n in lens) and all(s % kvb == 0 for s in seq_starts) and (t // kvb) % group == 0
    tq = _tile(kvb, 1024)
    tk = _tile(kvb, 1024)
    nkv = kvb // tk
    strip = 32
    lanes = 128
    assert vcol0 % pw == 0 and qcol0 % pw == 0 and kcol0 % pw == 0
    qc0, kc0, vc0 = qcol0 // pw, kcol0 // pw, vcol0 // pw
    seq_kv_blocks = tuple(n // kvb for n in lens)
    seq_group = tuple((s // kvb) // group for s in seq_starts)
    seq_sub_block = tuple((s // kvb) % group for s in seq_starts)
    assert all(sb + nb <= group for sb, nb in zip(seq_sub_block, seq_kv_blocks))
    kv_view = proj.reshape(t // kvb, kvb, proj.shape[1])

    def kv_map(col0):
        return lambda h, i: (_seq_lookup(i * tq, seq_starts, seq_group), 0, col0 + h)

    return pl.pallas_call(
        functools.partial(_attn_kernel, tk=tk, nkv=nkv, unroll=2 if nkv % 2 == 0 else 1, strip=strip,
                          out_scale=out_scale, seq_starts=seq_starts, seq_kv_blocks=seq_kv_blocks,
                          seq_sub_block=seq_sub_block),
        grid=(DA_HEADS, t // tq),
        in_specs=[pl.BlockSpec(memory_space=pltpu.SMEM),
                  pl.BlockSpec((tq, pw), lambda h, i: (i, qc0 + h)),
                  pl.BlockSpec((group, kvb, pw), kv_map(kc0)),
                  pl.BlockSpec((group, kvb, pw), kv_map(vc0)),
                  pl.BlockSpec((1, pw), lambda h, i: (0, 0))],
        out_specs=pl.BlockSpec((tq, pw), lambda h, i: (i, h)),
        out_shape=jax.ShapeDtypeStruct((t, DA_V_WIDTH), BF16),
        scratch_shapes=[pltpu.VMEM((2, tq, lanes), F32), pltpu.VMEM((2, tq, lanes), F32),
                        pltpu.VMEM((2, tq, pw), F32), pltpu.VMEM((2, tq, tk), F32), pltpu.VMEM((2, tq, tk), BF16),
                        pltpu.VMEM((2, tq, lanes), F32)],
        compiler_params=_params("arbitrary", "arbitrary"),
        name="diff_attention",
    )(lam, proj, kv_view, kv_view, subln_g)


def _outproj_kernel(y1_ref, y2_ref, w1_ref, w2_ref, g_ref, *rest, bounds):
    x_refs, o_ref = rest[:-1], rest[-1]
    acc = jnp.dot(y1_ref[...], w1_ref[...], preferred_element_type=F32)
    acc = acc + jnp.dot(y2_ref[...], w2_ref[...], preferred_element_type=F32)
    upd = g_ref[...] * acc

    def residual(k):
        o_ref[...] = x_refs[k][...] + upd

    _for_row_part(pl.program_id(1), bounds, residual)


def _outproj_residual(y1, c1, y2, c2, w, x_parts, gate, seq_starts):
    t = sum(xp.shape[0] for xp in x_parts)
    d = x_parts[0].shape[1]
    kh = w.shape[0] // 2
    tm = _row_tile(t, 512, seq_starts)
    tn = _tile(d, 1024)
    x_specs, bounds = _row_part_specs(x_parts, tm, tn, lambda j, i: i, lambda j, i: j)
    return pl.pallas_call(
        functools.partial(_outproj_kernel, bounds=bounds),
        grid=(d // tn, t // tm),
        in_specs=[pl.BlockSpec((tm, kh), lambda j, i: (i, c1)),
                  pl.BlockSpec((tm, kh), lambda j, i: (i, c2)),
                  pl.BlockSpec((kh, tn), lambda j, i: (0, j)),
                  pl.BlockSpec((kh, tn), lambda j, i: (1, j)),
                  pl.BlockSpec((None, 1, tn), lambda j, i: (_seq_index(i * tm, seq_starts), 0, j))] + x_specs,
        out_specs=pl.BlockSpec((tm, tn), lambda j, i: (i, j)),
        out_shape=jax.ShapeDtypeStruct((t, d), F32),
        compiler_params=_params("arbitrary", "arbitrary"),
        name="outproj_residual",
    )(y1, y2, w, w, gate, *x_parts)


def _router_kernel(x_ref, a_ref, sh_ref, wr_ref, br_ref, h_ref, lg_ref):
    h = _norm_mod(x_ref[...], a_ref[...], sh_ref[...])
    h_ref[...] = h.astype(h_ref.dtype)
    lg_ref[...] = jnp.dot(h, wr_ref[...], preferred_element_type=F32, precision=HIGHEST) + br_ref[...]


def _norm_mod_router(x, a, sh, wr, br, seq_starts):
    t, d = x.shape
    tm = _row_tile(t, 512, seq_starts)

    def seq_map(i):
        return (_seq_index(i * tm, seq_starts), 0, 0)

    return pl.pallas_call(
        _router_kernel,
        grid=(t // tm,),
        in_specs=[pl.BlockSpec((tm, d), lambda i: (i, 0)),
                  pl.BlockSpec((None, 1, d), seq_map),
                  pl.BlockSpec((None, 1, d), seq_map),
                  pl.BlockSpec((d, ROUTER_PAD), lambda i: (0, 0)),
                  pl.BlockSpec((1, ROUTER_PAD), lambda i: (0, 0))],
        out_specs=[pl.BlockSpec((tm, d), lambda i: (i, 0)),
                   pl.BlockSpec((tm, ROUTER_PAD), lambda i: (i, 0))],
        out_shape=[jax.ShapeDtypeStruct((t, d), BF16), jax.ShapeDtypeStruct((t, ROUTER_PAD), F32)],
        compiler_params=_params("arbitrary"),
        name="norm_mod_router",
    )(x, a, sh, wr, br)


def _moe_ffn_kernel(ib_ref, ob_ref, oh_ref, ih_ref, ie_ref, inew_ref, nxe_ref, nxh_ref, hasnx_ref, nu_ref,
                    xs_ref, w1_hbm, w3_hbm, w2_hbm, o_ref,
                    w1_stage, w3_stage, w2_stage, w1_scr, w3_scr, w2_scr, sems, *, layer, fh):
    del ib_ref, ob_ref, oh_ref
    s = pl.program_id(0)

    @pl.when(s >= nu_ref[0])
    def _():
        o_ref[...] = jnp.zeros_like(o_ref)

    def slab_copies(e, half):
        c0 = pl.multiple_of(half * fh, fh)
        return (pltpu.make_async_copy(w1_hbm.at[layer, e, :, pl.ds(c0, fh)], w1_stage, sems.at[0]),
                pltpu.make_async_copy(w3_hbm.at[layer, e, :, pl.ds(c0, fh)], w3_stage, sems.at[1]),
                pltpu.make_async_copy(w2_hbm.at[layer, e, pl.ds(c0, fh), :], w2_stage, sems.at[2]))

    @pl.when(s == 0)
    def _():
        for c in slab_copies(ie_ref[0], ih_ref[0]):
            c.start()

    @pl.when(s < nu_ref[0])
    def _():
        @pl.when(inew_ref[s] == 1)
        def _():
            for c in slab_copies(ie_ref[s], ih_ref[s]):
                c.wait()
            w1_scr[...] = w1_stage[...].astype(BF16)
            w3_scr[...] = w3_stage[...].astype(BF16)
            w2_scr[...] = w2_stage[...].astype(BF16)

            @pl.when(hasnx_ref[s] == 1)
            def _():
                for c in slab_copies(nxe_ref[s], nxh_ref[s]):
                    c.start()

        x = xs_ref[...]
        h1 = jnp.dot(x, w1_scr[...], preferred_element_type=F32)
        h3 = jnp.dot(x, w3_scr[...], preferred_element_type=F32)
        hid = (_silu(h1) * h3).astype(BF16)
        o_ref[...] = jnp.dot(hid, w2_scr[...], preferred_element_type=F32).astype(o_ref.dtype)


def _moe_ffn(xs, tables, w1, w3, w2, layer):
    n_rows, d = xs.shape
    ff = w1.shape[3]
    fh = ff // MOE_FF_SPLIT
    blk = MOE_ROW_BLOCK
    n_tab = len(tables)

    grid_spec = pltpu.PrefetchScalarGridSpec(
        num_scalar_prefetch=n_tab,
        grid=(tables[0].shape[0],),
        in_specs=[pl.BlockSpec((blk, d), lambda s, ib, *_: (ib[s], 0)),
                  pl.BlockSpec(memory_space=pl.ANY),
                  pl.BlockSpec(memory_space=pl.ANY),
                  pl.BlockSpec(memory_space=pl.ANY)],
        out_specs=pl.BlockSpec((blk, d), lambda s, ib, ob, oh, *_: (ob[s], oh[s])),
        scratch_shapes=[pltpu.VMEM((d, fh), F32), pltpu.VMEM((d, fh), F32), pltpu.VMEM((fh, d), F32),
                        pltpu.VMEM((d, fh), BF16), pltpu.VMEM((d, fh), BF16), pltpu.VMEM((fh, d), BF16),
                        pltpu.SemaphoreType.DMA((3,))])
    return pl.pallas_call(
        functools.partial(_moe_ffn_kernel, layer=layer, fh=fh),
        grid_spec=grid_spec,
        out_shape=jax.ShapeDtypeStruct((n_rows, MOE_FF_SPLIT * d), BF16),
        compiler_params=_params("arbitrary"),
        name="moe_ffn",
    )(*tables, xs, w1, w3, w2)


def _combine_kernel(x_ref, o0_ref, o1_ref, gt_ref, g_ref, *rest, final, bounds):
    if final:
        fg_ref, o_refs = rest[0], rest[1:]
    else:
        o_refs = rest
    d = x_ref.shape[1]
    gt = gt_ref[...]
    y = (gt[:, 0:1] * (o0_ref[:, :d].astype(F32) + o0_ref[:, d:].astype(F32))
         + gt[:, 1:2] * (o1_ref[:, :d].astype(F32) + o1_ref[:, d:].astype(F32)))
    x = x_ref[...] + g_ref[...] * y
    if final:
        ms = jnp.mean(x * x, axis=-1, keepdims=True)
        x = x * lax.rsqrt(ms + EPS) * fg_ref[...]

    def write(k):
        o_refs[k][...] = x

    _for_row_part(pl.program_id(0), bounds, write)


def _moe_combine(x, parts, gates, gate_mod, final_g, seq_starts, out_rows):
    t, d = x.shape
    tm = _row_tile(t, 512, seq_starts)
    final = final_g is not None
    out_shape = [jax.ShapeDtypeStruct((r, d), F32) for r in out_rows]
    out_specs, bounds = _row_part_specs(out_shape, tm, d, lambda i: i, lambda i: 0)
    in_specs = ([pl.BlockSpec((tm, d), lambda i: (i, 0))]
                + [pl.BlockSpec((tm, MOE_FF_SPLIT * d), lambda i: (i, 0))] * len(parts)
                + [pl.BlockSpec((tm, ROUTER_PAD), lambda i: (i, 0)),
                   pl.BlockSpec((None, 1, d), lambda i: (_seq_index(i * tm, seq_starts), 0, 0))])
    args = [x, *parts, gates, gate_mod]
    if final:
        in_specs.append(pl.BlockSpec((1, d), lambda i: (0, 0)))
        args.append(final_g)
    return pl.pallas_call(
        functools.partial(_combine_kernel, final=final, bounds=bounds),
        grid=(t // tm,),
        in_specs=in_specs,
        out_specs=out_specs,
        out_shape=out_shape,
        compiler_params=_params("arbitrary"),
        name="moe_combine_residual",
    )(*args)


def _ret_kernel(*refs, reverse, nchunks, seq_starts, seq_ends):
    rc = RET_CHUNK
    if reverse:
        (cd_ref, q_ref, kt_ref, v_ref, inner_ref, cross_ref, kvd_ref, of_ref, gate_ref, o_ref, st_scr) = refs
    else:
        (cd_ref, q_ref, kt_ref, v_ref, inner_ref, cross_ref, kvd_ref, o_ref, st_scr) = refs
    step = pl.program_id(0)
    if reverse:
        fresh = _is_any((nchunks - step) * rc, seq_ends)
    else:
        fresh = _is_any(step * rc, seq_starts)

    @pl.when(fresh)
    def _():
        st_scr[...] = jnp.zeros_like(st_scr)

    dk, dv = RET_QK_DIM, RET_V_DIM
    for h in range(RET_HEADS):
        q = q_ref[:, h * dk:(h + 1) * dk]
        kt = kt_ref[h * dk:(h + 1) * dk, :]
        v = v_ref[:, h * dv:(h + 1) * dv]
        s = jnp.dot(q, kt, preferred_element_type=F32) * inner_ref[h]
        st = st_scr[h]
        o = jnp.dot(s.astype(BF16), v, preferred_element_type=F32)
        o = o + jnp.dot(q, st.astype(BF16), preferred_element_type=F32) * cross_ref[h]
        ktd = (kt.astype(F32) * kvd_ref[h]).astype(BF16)
        st_scr[h] = st * cd_ref[h] + jnp.dot(ktd, v, preferred_element_type=F32)
        if reverse:
            o = o + of_ref[:, h * dv:(h + 1) * dv].astype(F32)
            ms = jnp.mean(o * o, axis=-1, keepdims=True)
            o = o * lax.rsqrt(ms + EPS) * _silu(gate_ref[:, h * dv:(h + 1) * dv].astype(F32))
        o_ref[:, h * dv:(h + 1) * dv] = o.astype(o_ref.dtype)


def _retention_direction(krt, proj, vcol0, gcol0, log_decay, seq_starts, seq_ends, reverse, o_fwd=None):
    t = proj.shape[0]
    rc = _tile(min(s2 - s1 for s1, s2 in zip(seq_starts, seq_ends)), RET_CHUNK)
    assert rc == RET_CHUNK
    nchunks = t // rc
    pos = jnp.arange(rc, dtype=F32)
    diff = pos[:, None] - pos[None, :]
    lg = log_decay.astype(F32)
    if reverse:
        mask = diff < 0
        dist = -diff
        cross = jnp.exp((rc - pos)[None, :, None] * lg[:, None, None])
        kvd = jnp.exp(pos[None, None, :] * lg[:, None, None])
    else:
        mask = diff >= 0
        dist = diff
        cross = jnp.exp((pos + 1.0)[None, :, None] * lg[:, None, None])
        kvd = jnp.exp((rc - 1.0 - pos)[None, None, :] * lg[:, None, None])
    inner = jnp.where(mask[None], jnp.exp(jnp.where(mask, dist, 0.0)[None] * lg[:, None, None]), 0.0)
    cross = jnp.broadcast_to(cross, (RET_HEADS, rc, RET_V_DIM))
    kvd = jnp.broadcast_to(kvd, (RET_HEADS, RET_QK_DIM, rc))
    chunk_decay = jnp.exp(rc * lg)

    def rb(i):
        return nchunks - 1 - i if reverse else i

    vb = vcol0 // RET_V_WIDTH
    assert vcol0 % RET_V_WIDTH == 0 and gcol0 % RET_V_WIDTH == 0
    in_specs = [pl.BlockSpec(memory_space=pltpu.SMEM),
                pl.BlockSpec((rc, RET_QK_WIDTH), lambda i: (rb(i), 0)),
                pl.BlockSpec((RET_QK_WIDTH, rc), lambda i: (0, rb(i))),
                pl.BlockSpec((rc, RET_V_WIDTH), lambda i: (rb(i), vb)),
                pl.BlockSpec((RET_HEADS, rc, rc), lambda i: (0, 0, 0)),
                pl.BlockSpec((RET_HEADS, rc, RET_V_DIM), lambda i: (0, 0, 0)),
                pl.BlockSpec((RET_HEADS, RET_QK_DIM, rc), lambda i: (0, 0, 0))]
    args = [chunk_decay, proj, krt, proj, inner, cross, kvd]
    if reverse:
        gb = gcol0 // RET_V_WIDTH
        in_specs += [pl.BlockSpec((rc, RET_V_WIDTH), lambda i: (rb(i), 0)),
                     pl.BlockSpec((rc, RET_V_WIDTH), lambda i: (rb(i), gb))]
        args += [o_fwd, proj]
    return pl.pallas_call(
        functools.partial(_ret_kernel, reverse=reverse, nchunks=nchunks,
                          seq_starts=seq_starts, seq_ends=seq_ends),
        grid=(nchunks,),
        in_specs=in_specs,
        out_specs=pl.BlockSpec((rc, RET_V_WIDTH), lambda i: (rb(i), 0)),
        out_shape=jax.ShapeDtypeStruct((t, RET_V_WIDTH), BF16),
        scratch_shapes=[pltpu.VMEM((RET_HEADS, RET_QK_DIM, RET_V_DIM), F32)],
        compiler_params=_params("arbitrary"),
        name="retention_bwd_norm_gate" if reverse else "retention_fwd",
    )(*args)


def _ssd_diff_layer(x, a1, sh1, p, i, layer, lay):
    seq_starts, seq_ends, groups = lay
    w_in = p['ab_w_in'][i]
    o2 = SSD_WIDTH + SSD_XBC
    o3 = o2 + 2 * SSD_HEADS
    w_main = jnp.concatenate([w_in[:, :o2], w_in[:, o3:]], axis=1).astype(BF16)
    w_dt = jnp.pad(w_in[:, o2:o3], ((0, 0), (0, DT_PAD - 2 * SSD_HEADS))).astype(BF16)
    qcol = o2
    kcol = qcol + DA_QK_WIDTH
    vcol = kcol + DA_QK_WIDTH
    lmax = max(g[2] for g in groups)
    tables = (*_rope_tables(lmax, DA_HEAD_DIM, DA_HEAD_DIM ** -0.5 * math.log2(math.e)),
              *_rope_tables(lmax, DA_HEAD_DIM, 1.0))
    tn = 1024
    assert qcol % tn == 0 and kcol % tn == 0 and vcol % tn == 0
    proj, dt = _norm_mod_matmul(x, a1, sh1, w_main, w_dt, seq_starts,
                                ((qcol // tn, kcol // tn), (kcol // tn, vcol // tn)), DA_HEAD_DIM, tables)

    xact = _conv_silu(proj, SSD_WIDTH, p['ssd_conv_w'][i], p['ssd_conv_b'][i], seq_starts, seq_ends)
    bt = xact[:, SSD_WIDTH:SSD_WIDTH + SSD_BC].T
    dtt = dt.T
    expand = (jnp.arange(SSD_WIDTH)[None, :] // SSD_HEAD_DIM == jnp.arange(SSD_HEADS)[:, None]).astype(F32)
    a_f = -jnp.exp(p['ssd_a_log_fwd'][i].astype(F32))
    a_b = -jnp.exp(p['ssd_a_log_bwd'][i].astype(F32))
    yf = _ssd_direction(xact, bt, dt, dtt, p['ssd_dt_bias_fwd'][i].astype(F32), a_f, expand,
                        seq_starts, seq_ends, reverse=False)
    dskip = jnp.repeat(p['ssd_d'][i].astype(F32), SSD_HEAD_DIM).reshape(1, SSD_WIDTH)
    y_ssd = _ssd_direction(xact, bt, dt, dtt, p['ssd_dt_bias_bwd'][i].astype(F32), a_b, expand,
                           seq_starts, seq_ends, reverse=True,
                           final=(proj, yf, dskip, p['ssd_norm_g'][i].astype(F32).reshape(1, SSD_WIDTH)))

    lam_init = 0.8 - 0.6 * math.exp(-0.3 * layer)
    lam = (jnp.exp(jnp.sum(p['da_lambda_q1'][i].astype(F32) * p['da_lambda_k1'][i].astype(F32)))
           - jnp.exp(jnp.sum(p['da_lambda_q2'][i].astype(F32) * p['da_lambda_k2'][i].astype(F32)))
           + lam_init).reshape(1)
    subln = p['da_subln_g'][i].astype(F32).reshape(1, 2 * DA_HEAD_DIM)
    o = _diff_attention(proj, qcol, kcol, vcol, lam, subln, 1.0 - lam_init, seq_starts, seq_ends)

    w_out = p['ab_w_out'][i].astype(BF16)
    return (y_ssd, 0, o, 0, w_out)


def _retention_layer(x, a1, sh1, p, j, lay):
    seq_starts, seq_ends, groups = lay
    w_in = p['ret_w_in'][j].astype(BF16)
    lmax = max(g[2] for g in groups)
    tables = (*_rope_tables(lmax, RET_QK_DIM, 1.0), *_rope_tables(lmax, RET_QK_DIM, RET_QK_DIM ** -0.5))
    tn = 1024
    kcol = RET_QK_WIDTH
    vcol = 2 * RET_QK_WIDTH
    gcol = vcol + RET_V_WIDTH
    proj = _norm_mod_matmul(x, a1, sh1, w_in, None, seq_starts,
                            ((0, kcol // tn), (kcol // tn, vcol // tn)), RET_QK_DIM, tables)
    krt = proj[:, kcol:vcol].T
    lg_f = -jnp.exp(p['ret_log_decay_fwd'][j].astype(F32))
    lg_b = -jnp.exp(p['ret_log_decay_bwd'][j].astype(F32))
    of = _retention_direction(krt, proj, vcol, gcol, lg_f, seq_starts, seq_ends, reverse=False)
    o = _retention_direction(krt, proj, vcol, gcol, lg_b, seq_starts, seq_ends, reverse=True, o_fwd=of)
    w_out = p['ret_w_out'][j].astype(BF16)
    return (o, 0, o, 1, w_out)


def _route(logits):
    coarse = logits[:, :MOE_GROUPS]
    grp = jnp.argmax(coarse, axis=-1)
    p_grp = jnp.max(jax.nn.softmax(coarse, axis=-1), axis=-1)
    fine = logits[:, MOE_GROUPS:MOE_GROUPS + MOE_EXPERTS].reshape(-1, MOE_GROUPS, MOE_EXPERTS_PER_GROUP)
    fine = jnp.take_along_axis(fine, grp[:, None, None], axis=1)[:, 0]
    top_v, top_i = lax.top_k(fine, MOE_TOP_K)
    gates = jax.nn.softmax(top_v, axis=-1) * p_grp[:, None]
    experts = (grp[:, None] * MOE_EXPERTS_PER_GROUP + top_i).astype(jnp.int32)
    return experts, gates


def _moe_layer(x_parts, mix, g1, a2, sh2, g2, p, layer, lay, final_g, out_rows):
    seq_starts, _, _ = lay
    t = sum(xp.shape[0] for xp in x_parts)
    d = x_parts[0].shape[1]
    wr = jnp.concatenate([p['moe_w_group'][layer], p['moe_w_expert'][layer]], axis=1).astype(F32)
    br = jnp.concatenate([p['moe_b_group'][layer], p['moe_b_expert'][layer]]).astype(F32)
    nr = wr.shape[1]
    wr = jnp.pad(wr, ((0, 0), (0, ROUTER_PAD - nr)))
    br = jnp.pad(br, (0, ROUTER_PAD - nr)).reshape(1, ROUTER_PAD)
    x = _outproj_residual(*mix, x_parts, g1, seq_starts)
    h, logits = _norm_mod_router(x, a2, sh2, wr, br, seq_starts)
    experts, gates = _route(logits)

    blk = MOE_ROW_BLOCK
    n_assign = t * MOE_TOP_K
    e_flat = experts.reshape(n_assign)
    onehot = (e_flat[:, None] == jnp.arange(MOE_EXPERTS, dtype=jnp.int32)[None, :]).astype(jnp.int32)
    csum = jnp.cumsum(onehot, axis=0)
    counts = csum[-1]
    rank = jnp.sum(csum * onehot, axis=1) - 1
    padded = (counts + blk - 1) // blk * blk
    ends_pad = jnp.cumsum(padded)
    start_pad = ends_pad - padded
    dest = (start_pad[e_flat] + rank).astype(jnp.int32)
    n_rows = n_assign + MOE_EXPERTS * blk
    n_blocks = n_rows // blk
    tok_flat = jnp.repeat(jnp.arange(t, dtype=jnp.int32), MOE_TOP_K)
    row_tok = (jnp.arange(n_rows, dtype=jnp.int32) % t).at[dest].set(tok_flat)
    nblk_e = (padded // blk).astype(jnp.int32)
    ends_blk = jnp.cumsum(nblk_e)
    start_blk = ends_blk - nblk_e
    n_items = (MOE_FF_SPLIT * ends_blk[-1]).astype(jnp.int32)
    item = jnp.minimum(jnp.arange(MOE_FF_SPLIT * n_blocks, dtype=jnp.int32), n_items - 1)
    item_expert = jnp.minimum(jnp.sum((MOE_FF_SPLIT * ends_blk)[None, :] <= item[:, None], axis=1),
                              MOE_EXPERTS - 1).astype(jnp.int32)
    local = item - MOE_FF_SPLIT * start_blk[item_expert]
    per_half = jnp.maximum(nblk_e[item_expert], 1)
    item_half = (local // per_half).astype(jnp.int32)
    item_block = (start_blk[item_expert] + local % per_half).astype(jnp.int32)
    changed = jnp.logical_or(item_expert[1:] != item_expert[:-1], item_half[1:] != item_half[:-1])
    item_new = jnp.concatenate([jnp.ones((1,), jnp.int32), changed.astype(jnp.int32)])
    n_slots = item.shape[0]
    pos = jnp.arange(n_slots, dtype=jnp.int32)
    run_start = jnp.where(item_new == 1, pos, n_slots)
    next_start = lax.cummin(jnp.concatenate([run_start[1:], jnp.full((1,), n_slots, jnp.int32)]), reverse=True)
    has_next = (next_start < n_slots).astype(jnp.int32)
    next_idx = jnp.minimum(next_start, n_slots - 1)
    spare = jnp.maximum(pos - n_items, 0)
    out_block = jnp.where(pos < n_items, item_block, ends_blk[-1] + spare // MOE_FF_SPLIT).astype(jnp.int32)
    out_half = jnp.where(pos < n_items, item_half, spare % MOE_FF_SPLIT).astype(jnp.int32)
    tables = (item_block, out_block, out_half, item_half, item_expert, item_new,
              item_expert[next_idx], item_half[next_idx], has_next, n_items.reshape(1))

    xs = jnp.take(h, row_tok, axis=0, mode='clip')
    out = _moe_ffn(xs, tables, p['moe_w1'], p['moe_w3'], p['moe_w2'], layer)
    dest2 = dest.reshape(t, MOE_TOP_K)
    parts = [jnp.take(out, dest2[:, k], axis=0, mode='clip') for k in range(MOE_TOP_K)]
    gates_pad = jnp.pad(gates, ((0, 0), (0, ROUTER_PAD - MOE_TOP_K)))
    return tuple(_moe_combine(x, parts, gates_pad, g2, final_g, seq_starts, out_rows))


def kernel(x_prompt, x_sample, c_prompt, c_sample, ada_w, ada_b, norm_mix_g, norm_ffn_g, ab_w_in, ssd_conv_w, ssd_conv_b, ssd_a_log_fwd, ssd_a_log_bwd, ssd_dt_bias_fwd, ssd_dt_bias_bwd, ssd_d, ssd_norm_g, da_lambda_q1, da_lambda_k1, da_lambda_q2, da_lambda_k2, da_subln_g, ab_w_out, ret_w_in, ret_log_decay_fwd, ret_log_decay_bwd, ret_w_out, moe_w_group, moe_b_group, moe_w_expert, moe_b_expert, moe_w1, moe_w3, moe_w2, final_norm_g):
    p = dict(ab_w_in=ab_w_in, ssd_conv_w=ssd_conv_w, ssd_conv_b=ssd_conv_b,
             ssd_a_log_fwd=ssd_a_log_fwd, ssd_a_log_bwd=ssd_a_log_bwd,
             ssd_dt_bias_fwd=ssd_dt_bias_fwd, ssd_dt_bias_bwd=ssd_dt_bias_bwd, ssd_d=ssd_d,
             ssd_norm_g=ssd_norm_g, da_lambda_q1=da_lambda_q1, da_lambda_k1=da_lambda_k1,
             da_lambda_q2=da_lambda_q2, da_lambda_k2=da_lambda_k2, da_subln_g=da_subln_g,
             ab_w_out=ab_w_out, ret_w_in=ret_w_in, ret_log_decay_fwd=ret_log_decay_fwd,
             ret_log_decay_bwd=ret_log_decay_bwd, ret_w_out=ret_w_out, moe_w_group=moe_w_group,
             moe_b_group=moe_b_group, moe_w_expert=moe_w_expert, moe_b_expert=moe_b_expert,
             moe_w1=moe_w1, moe_w3=moe_w3, moe_w2=moe_w2)
    bp, lp, d = x_prompt.shape
    bs, ls, _ = x_sample.shape
    depth = ada_w.shape[0]
    groups = ((0, bp, lp), (bp * lp, bs, ls))
    seq_starts = tuple(r0 + b * l for (r0, nb, l) in groups for b in range(nb))
    seq_ends = tuple(r0 + (b + 1) * l for (r0, nb, l) in groups for b in range(nb))
    lay = (seq_starts, seq_ends, groups)
    nseq = len(seq_starts)
    group_rows = (bp * lp, bs * ls)
    x = (x_prompt.reshape(bp * lp, d), x_sample.reshape(bs * ls, d))
    c = jnp.concatenate([c_prompt, c_sample], axis=0).astype(F32)
    c_pad = jnp.pad(c, ((0, -nseq % 8), (0, 0)))
    mod = _ada_modulation(c_pad, ada_w.astype(F32), ada_b.astype(F32))[:, :nseq]

    for layer in range(depth):
        sh1, sc1, g1, sh2, sc2, g2 = [m.reshape(nseq, 1, d) for m in jnp.split(mod[layer], 6, axis=-1)]
        a1 = norm_mix_g[layer].astype(F32)[None, None, :] * (1.0 + sc1)
        a2 = norm_ffn_g[layer].astype(F32)[None, None, :] * (1.0 + sc2)
        if layer % 2 == 0:
            mix = _ssd_diff_layer(x, a1, sh1, p, layer // 2, layer, lay)
        else:
            mix = _retention_layer(x, a1, sh1, p, layer // 2, lay)
        last = layer == depth - 1
        final_g = final_norm_g.astype(F32).reshape(1, d) if last else None
        x = _moe_layer(x, mix, g1, a2, sh2, g2, p, layer, lay, final_g,
                       group_rows if last else (sum(group_rows),))
    return (x[0].reshape(bp, lp, d), x[1].reshape(bs, ls, d))
```

```python
import functools
import math

import jax
import jax.numpy as jnp
from jax import lax
from jax.experimental import pallas as pl
from jax.experimental.pallas import tpu as pltpu

F32 = jnp.float32
BF16 = jnp.bfloat16
HIGHEST = lax.Precision.HIGHEST

EPS = 1e-6
SUBLN_EPS = 1e-5
ROPE_THETA = 10000.0

SSD_HEADS = 32
SSD_HEAD_DIM = 64
SSD_WIDTH = SSD_HEADS * SSD_HEAD_DIM
SSD_GROUPS = 4
SSD_STATE = 128
SSD_CONV = 5
SSD_BC = SSD_GROUPS * SSD_STATE
SSD_XBC = SSD_WIDTH + 2 * SSD_BC
SSD_CHUNK = 128
DA_HEADS = 8
DA_HEAD_DIM = 128
DA_QK_WIDTH = 2 * DA_HEADS * DA_HEAD_DIM
DA_V_WIDTH = DA_HEADS * 2 * DA_HEAD_DIM
RET_HEADS = 8
RET_QK_DIM = 256
RET_V_DIM = 512
RET_QK_WIDTH = RET_HEADS * RET_QK_DIM
RET_V_WIDTH = RET_HEADS * RET_V_DIM
RET_CHUNK = 256
MOE_GROUPS = 4
MOE_EXPERTS_PER_GROUP = 8
MOE_EXPERTS = MOE_GROUPS * MOE_EXPERTS_PER_GROUP
MOE_TOP_K = 2
MOE_ROW_BLOCK = 512
MOE_FF_CHUNKS = 2
ROUTER_PAD = 128
DT_PAD = 128
HALO = 16

VMEM_LIMIT_BYTES = 56 * 1024 * 1024
MOE_VMEM_LIMIT_BYTES = 62 * 1024 * 1024


def _params(*semantics, vmem_limit_bytes=VMEM_LIMIT_BYTES):
    return pltpu.CompilerParams(dimension_semantics=semantics, vmem_limit_bytes=vmem_limit_bytes)


def _silu(x):
    return x * jax.nn.sigmoid(x)


def _softplus(x):
    return jnp.maximum(x, 0.0) + jnp.log1p(jnp.exp(-jnp.abs(x)))


def _lane_repeat(x, n):
    return x if n == 1 else jnp.concatenate([x] * n, axis=1)


def _tile(n, pref):
    t = min(n, pref)
    assert n % t == 0, (n, pref)
    return t


def _row_tile(t, pref, seq_starts):
    tm = min(t, pref)
    while t % tm or any(st % tm for st in seq_starts):
        tm //= 2
    return tm


def _seq_index(row, seq_starts):
    s = 0
    for st in seq_starts[1:]:
        s = s + jnp.where(row >= st, 1, 0)
    return s


def _is_any(row, values):
    hit = row == values[0]
    for v in values[1:]:
        hit = jnp.logical_or(hit, row == v)
    return hit


def _ada_kernel(c_ref, w_ref, b_ref, o_ref):
    o_ref[...] = jnp.dot(_silu(c_ref[...]), w_ref[...], preferred_element_type=F32,
                         precision=HIGHEST) + b_ref[...]


def _ada_modulation(c_pad, ada_w, ada_b):
    depth, d, n = ada_w.shape
    rows = c_pad.shape[0]
    tn = _tile(n, 1024)
    return pl.pallas_call(
        _ada_kernel,
        grid=(depth, n // tn),
        in_specs=[pl.BlockSpec((rows, d), lambda l, j: (0, 0)),
                  pl.BlockSpec((None, d, tn), lambda l, j: (l, 0, j)),
                  pl.BlockSpec((None, 1, tn), lambda l, j: (l, 0, j))],
        out_specs=pl.BlockSpec((None, rows, tn), lambda l, j: (l, 0, j)),
        out_shape=jax.ShapeDtypeStruct((depth, rows, n), F32),
        compiler_params=_params("arbitrary", "arbitrary"),
        name="ada_modulation",
    )(c_pad, ada_w, ada_b.reshape(depth, 1, n))


def _norm_mod(x, a, sh):
    ms = jnp.mean(x * x, axis=-1, keepdims=True)
    return (x * lax.rsqrt(ms + EPS)) * a + sh


def _rotate_half_tile(x, cos, sin, head_dim):
    half = head_dim // 2
    outs = []
    for h in range(x.shape[1] // head_dim):
        xh = x[:, h * head_dim:(h + 1) * head_dim]
        if half % 128 == 0:
            rot = jnp.concatenate([xh[:, half:], xh[:, :half]], axis=-1)
        else:
            rot = pltpu.roll(xh, half, 1)
        outs.append(xh * cos + rot * sin)
    return jnp.concatenate(outs, axis=-1)


def _row_part_specs(parts, tm, width, row_of, col_of, single_buffer=False):
    specs, bounds = [], [0]
    mode = dict(pipeline_mode=pl.Buffered(1)) if single_buffer else {}
    for arr in parts:
        b0, nb = bounds[-1], arr.shape[0] // tm
        assert arr.shape[0] % tm == 0
        specs.append(pl.BlockSpec(
            (tm, width), lambda *g, b0=b0, nb=nb: (jnp.clip(row_of(*g) - b0, 0, nb - 1), col_of(*g)), **mode))
        bounds.append(b0 + nb)
    return specs, tuple(bounds)


def _for_row_part(i, bounds, fn):
    if len(bounds) == 2:
        fn(0)
        return
    for k in range(len(bounds) - 1):
        pl.when(jnp.logical_and(i >= bounds[k], i < bounds[k + 1]))(functools.partial(fn, k))


def _nmm_kernel(*refs, with_dt, rope_tiles, head_dim, bounds):
    n_parts = len(bounds) - 1
    x_refs = refs[:n_parts]
    a_ref, sh_ref, w_ref, cq_ref, sq_ref, ck_ref, sk_ref = refs[n_parts:n_parts + 7]
    if with_dt:
        wdt_ref, o_ref, odt_ref, h_scr = refs[n_parts + 7:]
    else:
        o_ref, h_scr = refs[n_parts + 7:]
    j = pl.program_id(1)

    def normalize(k):
        hb = _norm_mod(x_refs[k][...], a_ref[...], sh_ref[...]).astype(BF16)
        h_scr[...] = hb
        if with_dt:
            odt_ref[...] = jnp.dot(hb, wdt_ref[...], preferred_element_type=F32)

    @pl.when(j == 0)
    def _():
        _for_row_part(pl.program_id(0), bounds, normalize)

    acc = jnp.dot(h_scr[...], w_ref[...], preferred_element_type=F32)
    (q_lo, q_hi), (k_lo, k_hi) = rope_tiles
    is_q = jnp.logical_and(j >= q_lo, j < q_hi)
    is_k = jnp.logical_and(j >= k_lo, j < k_hi)

    @pl.when(is_q)
    def _():
        o_ref[...] = _rotate_half_tile(acc, cq_ref[...], sq_ref[...], head_dim).astype(o_ref.dtype)

    @pl.when(is_k)
    def _():
        o_ref[...] = _rotate_half_tile(acc, ck_ref[...], sk_ref[...], head_dim).astype(o_ref.dtype)

    @pl.when(jnp.logical_not(jnp.logical_or(is_q, is_k)))
    def _():
        o_ref[...] = acc.astype(o_ref.dtype)


def _norm_mod_matmul(x_parts, a, sh, w, wdt, seq_starts, rope_tiles, head_dim, tables):
    t = sum(xp.shape[0] for xp in x_parts)
    d = x_parts[0].shape[1]
    n = w.shape[1]
    tm = _row_tile(t, 1024, seq_starts)
    tn = _tile(n, 1024)
    with_dt = wdt is not None

    def seq_map(i, j):
        return (_seq_index(i * tm, seq_starts), 0, 0)

    def pos_map(i, j):
        r = i * tm
        s = _seq_index(r, seq_starts)
        st = 0
        for k, v in enumerate(seq_starts):
            st = st + jnp.where(s == k, v, 0)
        return ((r - st) // tm, 0)

    x_specs, bounds = _row_part_specs(x_parts, tm, d, lambda i, j: i, lambda i, j: 0,
                                      single_buffer=len(x_parts) > 1)
    in_specs = x_specs + [pl.BlockSpec((None, 1, d), seq_map),
                          pl.BlockSpec((None, 1, d), seq_map),
                          pl.BlockSpec((d, tn), lambda i, j: (0, j))] + [pl.BlockSpec((tm, head_dim), pos_map)] * 4
    out_specs = [pl.BlockSpec((tm, tn), lambda i, j: (i, j))]
    out_shape = [jax.ShapeDtypeStruct((t, n), BF16)]
    args = [*x_parts, a, sh, w, *tables]
    if with_dt:
        in_specs.append(pl.BlockSpec((d, DT_PAD), lambda i, j: (0, 0)))
        out_specs.append(pl.BlockSpec((tm, DT_PAD), lambda i, j: (i, 0)))
        out_shape.append(jax.ShapeDtypeStruct((t, DT_PAD), F32))
        args.append(wdt)
    res = pl.pallas_call(
        functools.partial(_nmm_kernel, with_dt=with_dt, rope_tiles=rope_tiles, head_dim=head_dim, bounds=bounds),
        grid=(t // tm, n // tn),
        in_specs=in_specs, out_specs=out_specs, out_shape=out_shape,
        scratch_shapes=[pltpu.VMEM((tm, d), BF16)],
        compiler_params=_params("arbitrary", "arbitrary"),
        name="norm_mod_matmul",
    )(*args)
    return res if with_dt else res[0]


def _conv_kernel(cur_ref, prev_ref, next_ref, w_ref, b_ref, o_ref, ext_scr, *, tm, seq_starts, seq_ends):
    r0 = pl.program_id(0) * tm
    at_start = _is_any(r0, seq_starts)
    at_end = _is_any(r0 + tm, seq_ends)
    prev = prev_ref[...].astype(F32)[HALO - 8:HALO]
    nxt = next_ref[...].astype(F32)[0:8]
    ext_scr[0:8, :] = jnp.where(at_start, 0.0, prev)
    ext_scr[8:8 + tm, :] = cur_ref[...].astype(F32)
    ext_scr[8 + tm:16 + tm, :] = jnp.where(at_end, 0.0, nxt)
    pad = SSD_CONV // 2
    acc = b_ref[...] + w_ref[0:1, :] * ext_scr[8 - pad:8 - pad + tm, :]
    for k in range(1, SSD_CONV):
        acc = acc + w_ref[k:k + 1, :] * ext_scr[8 - pad + k:8 - pad + k + tm, :]
    o_ref[...] = _silu(acc).astype(o_ref.dtype)


def _conv_silu(proj, col0, conv_w, conv_b, seq_starts, seq_ends):
    t = proj.shape[0]
    tm = _row_tile(t, 512, seq_starts)
    tc = 1024
    assert col0 % tc == 0 and SSD_XBC % tc == 0 and tm % HALO == 0
    cb = col0 // tc
    hb = tm // HALO
    last_halo = t // HALO - 1
    return pl.pallas_call(
        functools.partial(_conv_kernel, tm=tm, seq_starts=seq_starts, seq_ends=seq_ends),
        grid=(t // tm, SSD_XBC // tc),
        in_specs=[pl.BlockSpec((tm, tc), lambda i, j: (i, cb + j)),
                  pl.BlockSpec((HALO, tc), lambda i, j: (jnp.maximum(i * hb - 1, 0), cb + j)),
                  pl.BlockSpec((HALO, tc), lambda i, j: (jnp.minimum((i + 1) * hb, last_halo), cb + j)),
                  pl.BlockSpec((SSD_CONV, tc), lambda i, j: (0, j)),
                  pl.BlockSpec((1, tc), lambda i, j: (0, j))],
        out_specs=pl.BlockSpec((tm, tc), lambda i, j: (i, j)),
        out_shape=jax.ShapeDtypeStruct((t, SSD_XBC), BF16),
        scratch_shapes=[pltpu.VMEM((tm + 16, tc), F32)],
        compiler_params=_params("arbitrary", "arbitrary"),
        name="ssd_conv_silu",
    )(proj, proj, proj, conv_w, conv_b.reshape(1, SSD_XBC))


def _ssd_kernel(*refs, reverse, dcol, nchunks, seq_starts, seq_ends):
    ch = SSD_CHUNK
    if reverse:
        (xact_ref, bt_ref, dt_ref, dtt_ref, bias_ref, a_ref, biast_ref, at_ref, e_ref,
         z_ref, yf_ref, dskip_ref, g_ref, o_ref, h_scr, y_scr) = refs
    else:
        (xact_ref, bt_ref, dt_ref, dtt_ref, bias_ref, a_ref, biast_ref, at_ref, e_ref,
         o_ref, h_scr) = refs
    step = pl.program_id(0)
    if reverse:
        row_end = (nchunks - step) * ch
        fresh = _is_any(row_end, seq_ends)
    else:
        fresh = _is_any(step * ch, seq_starts)

    @pl.when(fresh)
    def _():
        h_scr[...] = jnp.zeros_like(h_scr)

    dt = _softplus(dt_ref[:, dcol:dcol + SSD_HEADS] + bias_ref[...])
    dta = dt * a_ref[...]
    dtt = _softplus(dtt_ref[dcol:dcol + SSD_HEADS, :] + biast_ref[...])
    dtat = dtt * at_ref[...]
    row = lax.broadcasted_iota(jnp.int32, (ch, ch), 0)
    col = lax.broadcasted_iota(jnp.int32, (ch, ch), 1)
    if reverse:
        keep = col >= row
    else:
        keep = col <= row
    tri = jnp.where(keep, 1.0, 0.0).astype(F32)
    trit = jnp.where(keep, 0.0, 1.0).astype(F32) + jnp.where(row == col, 1.0, 0.0).astype(F32)
    cs = jnp.dot(tri, dta, preferred_element_type=F32, precision=HIGHEST)
    cst = jnp.dot(dtat, trit, preferred_element_type=F32, precision=HIGHEST)
    last = 0 if reverse else ch - 1
    ecs = jnp.exp(cs)
    wt = dtt * jnp.exp(cst[:, last:last + 1] - cst)
    cdec_x = jnp.dot(jnp.broadcast_to(ecs[last:last + 1, :], (8, SSD_HEADS)), e_ref[...],
                     preferred_element_type=F32, precision=HIGHEST)[0:1, :]

    lane = lax.broadcasted_iota(jnp.int32, (1, 2 * SSD_HEAD_DIM), 1)
    first = lane < SSD_HEAD_DIM
    gw = SSD_WIDTH // SSD_GROUPS
    hpg = SSD_HEADS // SSD_GROUPS
    pw = 2 * SSD_HEAD_DIM
    ssq = jnp.zeros((ch, pw), F32)
    for g in range(SSD_GROUPS):
        bgt = bt_ref[g * SSD_STATE:(g + 1) * SSD_STATE, :]
        c0 = SSD_WIDTH + SSD_BC + g * SSD_STATE
        cg = xact_ref[:, c0:c0 + SSD_STATE]
        scores = jnp.dot(cg, bgt, preferred_element_type=F32)
        cgf = cg.astype(F32)
        bgtf = bgt.astype(F32)
        for pr in range(hpg // 2):
            lo = g * gw + pr * pw
            hi = lo + pw
            xpair = xact_ref[:, lo:hi].astype(F32)
            hpair = h_scr[g, :, pr * pw:(pr + 1) * pw]
            xsel = (jnp.where(first, xpair, 0.0).astype(BF16), jnp.where(first, 0.0, xpair).astype(BF16))
            hsel = (jnp.where(first, hpair, 0.0).astype(BF16), jnp.where(first, 0.0, hpair).astype(BF16))
            y = None
            bws = []
            for u in range(2):
                h = g * hpg + 2 * pr + u
                csb = jnp.broadcast_to(cs[:, h:h + 1], (ch, ch))
                seg = jnp.where(keep, csb - cst[h:h + 1, :], -jnp.inf)
                m = (scores * jnp.exp(seg) * dtt[h:h + 1, :]).astype(BF16)
                ce = (cgf * jnp.exp(csb)).astype(BF16)
                part = jnp.dot(jnp.concatenate([m, ce], axis=1), jnp.concatenate([xsel[u], hsel[u]], axis=0),
                               preferred_element_type=F32)
                y = part if y is None else y + part
                bws.append((bgtf * wt[h:h + 1, :]).astype(BF16))
            h_scr[g, :, pr * pw:(pr + 1) * pw] = hpair * cdec_x[:, lo:hi] + jnp.dot(
                jnp.concatenate(bws, axis=1), jnp.concatenate(xsel, axis=0), preferred_element_type=F32)
            if reverse:
                y = y + yf_ref[:, lo:hi].astype(F32) + dskip_ref[:, lo:hi] * xpair
                y = y * _silu(z_ref[:, lo:hi].astype(F32))
                ssq = ssq + y * y
                y_scr[:, lo:hi] = y
            else:
                o_ref[:, lo:hi] = y.astype(o_ref.dtype)
    if reverse:
        inv = lax.rsqrt(jnp.sum(ssq, axis=-1, keepdims=True) * (1.0 / SSD_WIDTH) + EPS)
        o_ref[...] = (y_scr[...] * inv * g_ref[...]).astype(o_ref.dtype)


def _ssd_direction(xact, bt, dt, dtt, bias, a, expand, seq_starts, seq_ends, reverse, final=None):
    t = xact.shape[0]
    ch = SSD_CHUNK
    nchunks = t // ch
    dcol = SSD_HEADS if reverse else 0

    def rb(i):
        return nchunks - 1 - i if reverse else i

    in_specs = [pl.BlockSpec((ch, SSD_XBC), lambda i: (rb(i), 0)),
                pl.BlockSpec((SSD_BC, ch), lambda i: (0, rb(i))),
                pl.BlockSpec((ch, DT_PAD), lambda i: (rb(i), 0)),
                pl.BlockSpec((DT_PAD, ch), lambda i: (0, rb(i))),
                pl.BlockSpec((1, SSD_HEADS), lambda i: (0, 0)),
                pl.BlockSpec((1, SSD_HEADS), lambda i: (0, 0)),
                pl.BlockSpec((SSD_HEADS, 1), lambda i: (0, 0)),
                pl.BlockSpec((SSD_HEADS, 1), lambda i: (0, 0)),
                pl.BlockSpec((SSD_HEADS, SSD_WIDTH), lambda i: (0, 0))]
    args = [xact, bt, dt, dtt, bias.reshape(1, -1), a.reshape(1, -1), bias.reshape(-1, 1), a.reshape(-1, 1), expand]
    scratch = [pltpu.VMEM((SSD_GROUPS, SSD_STATE, SSD_WIDTH // SSD_GROUPS), F32)]
    if reverse:
        proj, yf, dskip, g = final
        in_specs += [pl.BlockSpec((ch, SSD_WIDTH), lambda i: (rb(i), 0)),
                     pl.BlockSpec((ch, SSD_WIDTH), lambda i: (rb(i), 0)),
                     pl.BlockSpec((1, SSD_WIDTH), lambda i: (0, 0)),
                     pl.BlockSpec((1, SSD_WIDTH), lambda i: (0, 0))]
        args += [proj, yf, dskip, g]
        scratch.append(pltpu.VMEM((ch, SSD_WIDTH), F32))
    return pl.pallas_call(
        functools.partial(_ssd_kernel, reverse=reverse, dcol=dcol, nchunks=nchunks,
                          seq_starts=seq_starts, seq_ends=seq_ends),
        grid=(nchunks,),
        in_specs=in_specs,
        out_specs=pl.BlockSpec((ch, SSD_WIDTH), lambda i: (rb(i), 0)),
        out_shape=jax.ShapeDtypeStruct((t, SSD_WIDTH), BF16),
        scratch_shapes=scratch,
        compiler_params=_params("arbitrary"),
        name="ssd_bwd_gate_norm" if reverse else "ssd_fwd",
    )(*args)


def _rope_tables(lmax, head_dim, scale):
    half = head_dim // 2
    inv = 1.0 / (ROPE_THETA ** (jnp.arange(half, dtype=F32) / half))
    ang = jnp.arange(lmax, dtype=F32)[:, None] * inv[None, :]
    cos = jnp.cos(ang) * scale
    sin = jnp.sin(ang) * scale
    return jnp.concatenate([cos, cos], axis=-1), jnp.concatenate([-sin, sin], axis=-1)


def _seq_lookup(row, seq_starts, values):
    s = _seq_index(row, seq_starts)
    out = 0
    for k, v in enumerate(values):
        out = out + jnp.where(s == k, v, 0)
    return out


def _attn_kernel(lam_ref, q_ref, k_ref, v_ref, g_ref, o_ref, m_scr, l_scr, acc_scr, s_scr, p_scr, alpha_scr, *,
                 tk, nkv, unroll, strip, out_scale, seq_starts, seq_kv_blocks, seq_sub_block):
    dh = DA_HEAD_DIM
    tq = q_ref.shape[0]
    lanes = m_scr.shape[-1]
    row0 = pl.program_id(1) * tq
    n_valid = _seq_lookup(row0, seq_starts, seq_kv_blocks)
    sub0 = _seq_lookup(row0, seq_starts, seq_sub_block)
    m_scr[...] = jnp.full_like(m_scr, -jnp.inf)
    l_scr[...] = jnp.zeros_like(l_scr)
    acc_scr[...] = jnp.zeros_like(acc_scr)

    def chunk(c):
        sub = sub0 + c // nkv
        off = pl.multiple_of((c % nkv) * tk, tk)
        for u in range(2):
            s_scr[u] = lax.dot_general(q_ref[:, u * dh:(u + 1) * dh],
                                       k_ref[sub, pl.ds(off, tk), u * dh:(u + 1) * dh],
                                       (((1,), (1,)), ((), ())), preferred_element_type=F32)
        for u in range(2):
            for r in range(tq // strip):
                rows = slice(r * strip, (r + 1) * strip)
                s = s_scr[u, rows, :]
                m_prev = m_scr[u, rows, :]
                m_new = jnp.maximum(m_prev, jnp.max(s, axis=-1, keepdims=True))
                alpha = jnp.exp2(m_prev - m_new)
                p = jnp.exp2(s - _lane_repeat(m_new, tk // lanes))
                l_scr[u, rows, :] = alpha * l_scr[u, rows, :] + jnp.sum(p, axis=-1, keepdims=True)
                m_scr[u, rows, :] = m_new
                alpha_scr[u, rows, :] = alpha
                p_scr[u, rows, :] = p.astype(BF16)
            pv = jnp.dot(p_scr[u], v_ref[sub, pl.ds(off, tk), :], preferred_element_type=F32)
            acc_scr[u] = acc_scr[u] * _lane_repeat(alpha_scr[u], acc_scr.shape[-1] // lanes) + pv

    def body(j, carry):
        for w in range(unroll):
            chunk(j * unroll + w)
        return carry

    lax.fori_loop(0, n_valid * (nkv // unroll), body, 0)
    lam = lam_ref[0]
    rep = acc_scr.shape[-1] // lanes
    o = (acc_scr[0] / _lane_repeat(l_scr[0], rep)
         - lam * (acc_scr[1] / _lane_repeat(l_scr[1], rep)))
    ms = jnp.mean(o * o, axis=-1, keepdims=True)
    o_ref[...] = (o * lax.rsqrt(ms + SUBLN_EPS) * g_ref[...] * out_scale).astype(o_ref.dtype)


def _diff_attention(proj, qcol0, kcol0, vcol0, lam, subln_g, out_scale, seq_starts, seq_ends):
    t = proj.shape[0]
    pw = 2 * DA_HEAD_DIM
    lens = tuple(e - s for s, e in zip(seq_starts, seq_ends))
    kvb = min(lens)
    group = max(lens) // kvb
    assert all(n % kvb == 0 for n in lens) and all(s % kvb == 0 for s in seq_starts) and (t // kvb) % group == 0
    tq = _tile(kvb, 1024)
    tk = _tile(kvb, 1024)
    nkv = kvb // tk
    strip = 32
    lanes = 128
    assert vcol0 % pw == 0 and qcol0 % pw == 0 and kcol0 % pw == 0
    qc0, kc0, vc0 = qcol0 // pw, kcol0 // pw, vcol0 // pw
    seq_kv_blocks = tuple(n // kvb for n in lens)
    seq_group = tuple((s // kvb) // group for s in seq_starts)
    seq_sub_block = tuple((s // kvb) % group for s in seq_starts)
    assert all(sb + nb <= group for sb, nb in zip(seq_sub_block, seq_kv_blocks))
    kv_view = proj.reshape(t // kvb, kvb, proj.shape[1])

    def kv_map(col0):
        return lambda h, i: (_seq_lookup(i * tq, seq_starts, seq_group), 0, col0 + h)

    return pl.pallas_call(
        functools.partial(_attn_kernel, tk=tk, nkv=nkv, unroll=2 if nkv % 2 == 0 else 1, strip=strip,
                          out_scale=out_scale, seq_starts=seq_starts, seq_kv_blocks=seq_kv_blocks,
                          seq_sub_block=seq_sub_block),
        grid=(DA_HEADS, t // tq),
        in_specs=[pl.BlockSpec(memory_space=pltpu.SMEM),
                  pl.BlockSpec((tq, pw), lambda h, i: (i, qc0 + h)),
                  pl.BlockSpec((group, kvb, pw), kv_map(kc0)),
                  pl.BlockSpec((group, kvb, pw), kv_map(vc0)),
                  pl.BlockSpec((1, pw), lambda h, i: (0, 0))],
        out_specs=pl.BlockSpec((tq, pw), lambda h, i: (i, h)),
        out_shape=jax.ShapeDtypeStruct((t, DA_V_WIDTH), BF16),
        scratch_shapes=[pltpu.VMEM((2, tq, lanes), F32), pltpu.VMEM((2, tq, lanes), F32),
                        pltpu.VMEM((2, tq, pw), F32), pltpu.VMEM((2, tq, tk), F32), pltpu.VMEM((2, tq, tk), BF16),
                        pltpu.VMEM((2, tq, lanes), F32)],
        compiler_params=_params("arbitrary", "arbitrary"),
        name="diff_attention",
    )(lam, proj, kv_view, kv_view, subln_g)


def _outproj_kernel(y1_ref, y2_ref, w1_ref, w2_ref, g_ref, *rest, bounds):
    x_refs, o_ref = rest[:-1], rest[-1]
    acc = jnp.dot(y1_ref[...], w1_ref[...], preferred_element_type=F32)
    acc = acc + jnp.dot(y2_ref[...], w2_ref[...], preferred_element_type=F32)
    upd = g_ref[...] * acc

    def residual(k):
        o_ref[...] = x_refs[k][...] + upd

    _for_row_part(pl.program_id(1), bounds, residual)


def _outproj_residual(y1, c1, y2, c2, w, x_parts, gate, seq_starts):
    t = sum(xp.shape[0] for xp in x_parts)
    d = x_parts[0].shape[1]
    kh = w.shape[0] // 2
    tm = _row_tile(t, 512, seq_starts)
    tn = _tile(d, 1024)
    x_specs, bounds = _row_part_specs(x_parts, tm, tn, lambda j, i: i, lambda j, i: j)
    return pl.pallas_call(
        functools.partial(_outproj_kernel, bounds=bounds),
        grid=(d // tn, t // tm),
        in_specs=[pl.BlockSpec((tm, kh), lambda j, i: (i, c1)),
                  pl.BlockSpec((tm, kh), lambda j, i: (i, c2)),
                  pl.BlockSpec((kh, tn), lambda j, i: (0, j)),
                  pl.BlockSpec((kh, tn), lambda j, i: (1, j)),
                  pl.BlockSpec((None, 1, tn), lambda j, i: (_seq_index(i * tm, seq_starts), 0, j))] + x_specs,
        out_specs=pl.BlockSpec((tm, tn), lambda j, i: (i, j)),
        out_shape=jax.ShapeDtypeStruct((t, d), F32),
        compiler_params=_params("arbitrary", "arbitrary"),
        name="outproj_residual",
    )(y1, y2, w, w, gate, *x_parts)


def _router_kernel(x_ref, a_ref, sh_ref, wr_ref, br_ref, h_ref, rt_ref, cnt_ref, carry_scr):
    @pl.when(pl.program_id(0) == 0)
    def _():
        carry_scr[...] = jnp.zeros_like(carry_scr)

    h = _norm_mod(x_ref[...], a_ref[...], sh_ref[...])
    h_ref[...] = h.astype(h_ref.dtype)
    lg = jnp.dot(h, wr_ref[...], preferred_element_type=F32, precision=HIGHEST) + br_ref[...]
    tm, lanes = lg.shape
    lane = lax.broadcasted_iota(jnp.int32, (tm, lanes), 1).astype(F32)
    none = float(lanes)

    def top1(mask):
        v = jnp.max(jnp.where(mask, lg, -jnp.inf), axis=-1, keepdims=True)
        idx = jnp.min(jnp.where(mask, jnp.where(lg == v, lane, none), none), axis=-1, keepdims=True)
        return v, idx

    is_group = lane < MOE_GROUPS
    cm, grp = top1(is_group)
    p_grp = 1.0 / jnp.sum(jnp.where(is_group, jnp.exp(lg - cm), 0.0), axis=-1, keepdims=True)
    lo = MOE_GROUPS + grp * MOE_EXPERTS_PER_GROUP
    in_grp = jnp.logical_and(lane >= lo, lane < lo + MOE_EXPERTS_PER_GROUP)
    v1, i1 = top1(in_grp)
    v2, i2 = top1(jnp.logical_and(in_grp, lane != i1))
    e12 = jnp.exp(v2 - v1)
    g1 = p_grp / (1.0 + e12)
    g2 = g1 * e12

    onehot = jnp.where(lane == i1, 1.0, 0.0) + jnp.where(lane == i2, 1.0, 0.0)
    row = lax.broadcasted_iota(jnp.int32, (tm, tm), 0)
    col = lax.broadcasted_iota(jnp.int32, (tm, tm), 1)
    before = jnp.where(col < row, 1.0, 0.0).astype(BF16)
    base = carry_scr[0:1, :] + jnp.dot(before, onehot.astype(BF16), preferred_element_type=F32)
    r1 = jnp.sum(jnp.where(lane == i1, base, 0.0), axis=-1, keepdims=True)
    r2 = jnp.sum(jnp.where(lane == i2, base, 0.0), axis=-1, keepdims=True)
    carry_scr[...] = carry_scr[...] + jnp.sum(onehot, axis=0, keepdims=True)
    cnt_ref[...] = carry_scr[...]

    fields = (i1 - MOE_GROUPS, i2 - MOE_GROUPS, r1, r2, g1, g2)
    rt = jnp.zeros((tm, lanes), F32)
    for k, f in enumerate(fields):
        rt = jnp.where(lane == k, f, rt)
    rt_ref[...] = rt


def _norm_mod_router(x, a, sh, wr, br, seq_starts):
    t, d = x.shape
    tm = _row_tile(t, 512, seq_starts)

    def seq_map(i):
        return (_seq_index(i * tm, seq_starts), 0, 0)

    return pl.pallas_call(
        _router_kernel,
        grid=(t // tm,),
        in_specs=[pl.BlockSpec((tm, d), lambda i: (i, 0)),
                  pl.BlockSpec((None, 1, d), seq_map),
                  pl.BlockSpec((None, 1, d), seq_map),
                  pl.BlockSpec((d, ROUTER_PAD), lambda i: (0, 0)),
                  pl.BlockSpec((1, ROUTER_PAD), lambda i: (0, 0))],
        out_specs=[pl.BlockSpec((tm, d), lambda i: (i, 0)),
                   pl.BlockSpec((tm, ROUTER_PAD), lambda i: (i, 0)),
                   pl.BlockSpec((8, ROUTER_PAD), lambda i: (0, 0))],
        out_shape=[jax.ShapeDtypeStruct((t, d), BF16), jax.ShapeDtypeStruct((t, ROUTER_PAD), F32),
                   jax.ShapeDtypeStruct((8, ROUTER_PAD), F32)],
        scratch_shapes=[pltpu.VMEM((8, ROUTER_PAD), F32)],
        compiler_params=_params("arbitrary"),
        name="norm_mod_router",
    )(x, a, sh, wr, br)


def _moe_ffn_kernel(ib_ref, ie_ref, inew_ref, nxe_ref, hasnx_ref, nu_ref,
                    xs_ref, w1_hbm, w3_hbm, w2_hbm, o_ref,
                    w1_stage, w3_stage, w2_stage, w1_scr, w3_scr, w2_scr, sems, *, layer, ff_chunks):
    del ib_ref
    s = pl.program_id(0)

    @pl.when(s >= nu_ref[0])
    def _():
        o_ref[...] = jnp.zeros_like(o_ref)

    def weight_copies(e):
        return (pltpu.make_async_copy(w1_hbm.at[layer, e], w1_stage, sems.at[0]),
                pltpu.make_async_copy(w3_hbm.at[layer, e], w3_stage, sems.at[1]),
                pltpu.make_async_copy(w2_hbm.at[layer, e], w2_stage, sems.at[2]))

    @pl.when(s == 0)
    def _():
        for c in weight_copies(ie_ref[0]):
            c.start()

    @pl.when(s < nu_ref[0])
    def _():
        @pl.when(inew_ref[s] == 1)
        def _():
            for c in weight_copies(ie_ref[s]):
                c.wait()
            w1_scr[...] = w1_stage[...].astype(BF16)
            w3_scr[...] = w3_stage[...].astype(BF16)
            w2_scr[...] = w2_stage[...].astype(BF16)

            @pl.when(hasnx_ref[s] == 1)
            def _():
                for c in weight_copies(nxe_ref[s]):
                    c.start()

        x = xs_ref[...]
        fc = w1_scr.shape[1] // ff_chunks
        acc = None
        for k in range(ff_chunks):
            h1 = jnp.dot(x, w1_scr[:, k * fc:(k + 1) * fc], preferred_element_type=F32)
            h3 = jnp.dot(x, w3_scr[:, k * fc:(k + 1) * fc], preferred_element_type=F32)
            hid = (_silu(h1) * h3).astype(BF16)
            part = jnp.dot(hid, w2_scr[k * fc:(k + 1) * fc, :], preferred_element_type=F32)
            acc = part if acc is None else acc + part
        o_ref[...] = acc.astype(o_ref.dtype)


def _moe_ffn(xs, tables, w1, w3, w2, layer):
    n_rows, d = xs.shape
    ff = w1.shape[3]
    blk = MOE_ROW_BLOCK
    n_tab = len(tables)

    grid_spec = pltpu.PrefetchScalarGridSpec(
        num_scalar_prefetch=n_tab,
        grid=(n_rows // blk,),
        in_specs=[pl.BlockSpec((blk, d), lambda s, ib, *_: (ib[s], 0)),
                  pl.BlockSpec(memory_space=pl.ANY),
                  pl.BlockSpec(memory_space=pl.ANY),
                  pl.BlockSpec(memory_space=pl.ANY)],
        out_specs=pl.BlockSpec((blk, d), lambda s, *_: (s, 0)),
        scratch_shapes=[pltpu.VMEM((d, ff), F32), pltpu.VMEM((d, ff), F32), pltpu.VMEM((ff, d), F32),
                        pltpu.VMEM((d, ff), BF16), pltpu.VMEM((d, ff), BF16), pltpu.VMEM((ff, d), BF16),
                        pltpu.SemaphoreType.DMA((3,))])
    return pl.pallas_call(
        functools.partial(_moe_ffn_kernel, layer=layer, ff_chunks=MOE_FF_CHUNKS),
        grid_spec=grid_spec,
        out_shape=jax.ShapeDtypeStruct((n_rows, d), BF16),
        compiler_params=_params("arbitrary", vmem_limit_bytes=MOE_VMEM_LIMIT_BYTES),
        name="moe_ffn",
    )(*tables, xs, w1, w3, w2)


def _combine_kernel(x_ref, o0_ref, o1_ref, gt_ref, g_ref, *rest, final, bounds):
    if final:
        fg_ref, o_refs = rest[0], rest[1:]
    else:
        o_refs = rest
    gt = gt_ref[...]
    g0 = 2 * MOE_TOP_K
    y = gt[:, g0:g0 + 1] * o0_ref[...].astype(F32) + gt[:, g0 + 1:g0 + 2] * o1_ref[...].astype(F32)
    x = x_ref[...] + g_ref[...] * y
    if final:
        ms = jnp.mean(x * x, axis=-1, keepdims=True)
        x = x * lax.rsqrt(ms + EPS) * fg_ref[...]

    def write(k):
        o_refs[k][...] = x

    _for_row_part(pl.program_id(0), bounds, write)


def _moe_combine(x, parts, gates, gate_mod, final_g, seq_starts, out_rows):
    t, d = x.shape
    tm = _row_tile(t, 512, seq_starts)
    final = final_g is not None
    out_shape = [jax.ShapeDtypeStruct((r, d), F32) for r in out_rows]
    out_specs, bounds = _row_part_specs(out_shape, tm, d, lambda i: i, lambda i: 0)
    in_specs = ([pl.BlockSpec((tm, d), lambda i: (i, 0))]
                + [pl.BlockSpec((tm, d), lambda i: (i, 0))] * len(parts)
                + [pl.BlockSpec((tm, ROUTER_PAD), lambda i: (i, 0)),
                   pl.BlockSpec((None, 1, d), lambda i: (_seq_index(i * tm, seq_starts), 0, 0))])
    args = [x, *parts, gates, gate_mod]
    if final:
        in_specs.append(pl.BlockSpec((1, d), lambda i: (0, 0)))
        args.append(final_g)
    return pl.pallas_call(
        functools.partial(_combine_kernel, final=final, bounds=bounds),
        grid=(t // tm,),
        in_specs=in_specs,
        out_specs=out_specs,
        out_shape=out_shape,
        compiler_params=_params("arbitrary"),
        name="moe_combine_residual",
    )(*args)


def _ret_kernel(*refs, reverse, nchunks, seq_starts, seq_ends):
    rc = RET_CHUNK
    if reverse:
        (cd_ref, q_ref, kt_ref, v_ref, inner_ref, cross_ref, kvd_ref, of_ref, gate_ref, o_ref, st_scr) = refs
    else:
        (cd_ref, q_ref, kt_ref, v_ref, inner_ref, cross_ref, kvd_ref, o_ref, st_scr) = refs
    step = pl.program_id(0)
    if reverse:
        fresh = _is_any((nchunks - step) * rc, seq_ends)
    else:
        fresh = _is_any(step * rc, seq_starts)

    @pl.when(fresh)
    def _():
        st_scr[...] = jnp.zeros_like(st_scr)

    dk, dv = RET_QK_DIM, RET_V_DIM
    for h in range(RET_HEADS):
        q = q_ref[:, h * dk:(h + 1) * dk]
        kt = kt_ref[h * dk:(h + 1) * dk, :]
        v = v_ref[:, h * dv:(h + 1) * dv]
        s = jnp.dot(q, kt, preferred_element_type=F32) * inner_ref[h]
        st = st_scr[h]
        o = jnp.dot(s.astype(BF16), v, preferred_element_type=F32)
        o = o + jnp.dot(q, st.astype(BF16), preferred_element_type=F32) * cross_ref[h]
        ktd = (kt.astype(F32) * kvd_ref[h]).astype(BF16)
        st_scr[h] = st * cd_ref[h] + jnp.dot(ktd, v, preferred_element_type=F32)
        if reverse:
            o = o + of_ref[:, h * dv:(h + 1) * dv].astype(F32)
            ms = jnp.mean(o * o, axis=-1, keepdims=True)
            o = o * lax.rsqrt(ms + EPS) * _silu(gate_ref[:, h * dv:(h + 1) * dv].astype(F32))
        o_ref[:, h * dv:(h + 1) * dv] = o.astype(o_ref.dtype)


def _retention_direction(krt, proj, vcol0, gcol0, log_decay, seq_starts, seq_ends, reverse, o_fwd=None):
    t = proj.shape[0]
    rc = _tile(min(s2 - s1 for s1, s2 in zip(seq_starts, seq_ends)), RET_CHUNK)
    assert rc == RET_CHUNK
    nchunks = t // rc
    pos = jnp.arange(rc, dtype=F32)
    diff = pos[:, None] - pos[None, :]
    lg = log_decay.astype(F32)
    if reverse:
        mask = diff < 0
        dist = -diff
        cross = jnp.exp((rc - pos)[None, :, None] * lg[:, None, None])
        kvd = jnp.exp(pos[None, None, :] * lg[:, None, None])
    else:
        mask = diff >= 0
        dist = diff
        cross = jnp.exp((pos + 1.0)[None, :, None] * lg[:, None, None])
        kvd = jnp.exp((rc - 1.0 - pos)[None, None, :] * lg[:, None, None])
    inner = jnp.where(mask[None], jnp.exp(jnp.where(mask, dist, 0.0)[None] * lg[:, None, None]), 0.0)
    cross = jnp.broadcast_to(cross, (RET_HEADS, rc, RET_V_DIM))
    kvd = jnp.broadcast_to(kvd, (RET_HEADS, RET_QK_DIM, rc))
    chunk_decay = jnp.exp(rc * lg)

    def rb(i):
        return nchunks - 1 - i if reverse else i

    vb = vcol0 // RET_V_WIDTH
    assert vcol0 % RET_V_WIDTH == 0 and gcol0 % RET_V_WIDTH == 0
    in_specs = [pl.BlockSpec(memory_space=pltpu.SMEM),
                pl.BlockSpec((rc, RET_QK_WIDTH), lambda i: (rb(i), 0)),
                pl.BlockSpec((RET_QK_WIDTH, rc), lambda i: (0, rb(i))),
                pl.BlockSpec((rc, RET_V_WIDTH), lambda i: (rb(i), vb)),
                pl.BlockSpec((RET_HEADS, rc, rc), lambda i: (0, 0, 0)),
                pl.BlockSpec((RET_HEADS, rc, RET_V_DIM), lambda i: (0, 0, 0)),
                pl.BlockSpec((RET_HEADS, RET_QK_DIM, rc), lambda i: (0, 0, 0))]
    args = [chunk_decay, proj, krt, proj, inner, cross, kvd]
    if reverse:
        gb = gcol0 // RET_V_WIDTH
        in_specs += [pl.BlockSpec((rc, RET_V_WIDTH), lambda i: (rb(i), 0)),
                     pl.BlockSpec((rc, RET_V_WIDTH), lambda i: (rb(i), gb))]
        args += [o_fwd, proj]
    return pl.pallas_call(
        functools.partial(_ret_kernel, reverse=reverse, nchunks=nchunks,
                          seq_starts=seq_starts, seq_ends=seq_ends),
        grid=(nchunks,),
        in_specs=in_specs,
        out_specs=pl.BlockSpec((rc, RET_V_WIDTH), lambda i: (rb(i), 0)),
        out_shape=jax.ShapeDtypeStruct((t, RET_V_WIDTH), BF16),
        scratch_shapes=[pltpu.VMEM((RET_HEADS, RET_QK_DIM, RET_V_DIM), F32)],
        compiler_params=_params("arbitrary"),
        name="retention_bwd_norm_gate" if reverse else "retention_fwd",
    )(*args)


def _ssd_diff_layer(x, a1, sh1, p, i, layer, lay):
    seq_starts, seq_ends, groups = lay
    w_in = p['ab_w_in'][i]
    o2 = SSD_WIDTH + SSD_XBC
    o3 = o2 + 2 * SSD_HEADS
    w_main = jnp.concatenate([w_in[:, :o2], w_in[:, o3:]], axis=1).astype(BF16)
    w_dt = jnp.pad(w_in[:, o2:o3], ((0, 0), (0, DT_PAD - 2 * SSD_HEADS))).astype(BF16)
    qcol = o2
    kcol = qcol + DA_QK_WIDTH
    vcol = kcol + DA_QK_WIDTH
    lmax = max(g[2] for g in groups)
    tables = (*_rope_tables(lmax, DA_HEAD_DIM, DA_HEAD_DIM ** -0.5 * math.log2(math.e)),
              *_rope_tables(lmax, DA_HEAD_DIM, 1.0))
    tn = 1024
    assert qcol % tn == 0 and kcol % tn == 0 and vcol % tn == 0
    proj, dt = _norm_mod_matmul(x, a1, sh1, w_main, w_dt, seq_starts,
                                ((qcol // tn, kcol // tn), (kcol // tn, vcol // tn)), DA_HEAD_DIM, tables)

    xact = _conv_silu(proj, SSD_WIDTH, p['ssd_conv_w'][i], p['ssd_conv_b'][i], seq_starts, seq_ends)
    bt = xact[:, SSD_WIDTH:SSD_WIDTH + SSD_BC].T
    dtt = dt.T
    expand = (jnp.arange(SSD_WIDTH)[None, :] // SSD_HEAD_DIM == jnp.arange(SSD_HEADS)[:, None]).astype(F32)
    a_f = -jnp.exp(p['ssd_a_log_fwd'][i].astype(F32))
    a_b = -jnp.exp(p['ssd_a_log_bwd'][i].astype(F32))
    yf = _ssd_direction(xact, bt, dt, dtt, p['ssd_dt_bias_fwd'][i].astype(F32), a_f, expand,
                        seq_starts, seq_ends, reverse=False)
    dskip = jnp.repeat(p['ssd_d'][i].astype(F32), SSD_HEAD_DIM).reshape(1, SSD_WIDTH)
    y_ssd = _ssd_direction(xact, bt, dt, dtt, p['ssd_dt_bias_bwd'][i].astype(F32), a_b, expand,
                           seq_starts, seq_ends, reverse=True,
                           final=(proj, yf, dskip, p['ssd_norm_g'][i].astype(F32).reshape(1, SSD_WIDTH)))

    lam_init = 0.8 - 0.6 * math.exp(-0.3 * layer)
    lam = (jnp.exp(jnp.sum(p['da_lambda_q1'][i].astype(F32) * p['da_lambda_k1'][i].astype(F32)))
           - jnp.exp(jnp.sum(p['da_lambda_q2'][i].astype(F32) * p['da_lambda_k2'][i].astype(F32)))
           + lam_init).reshape(1)
    subln = p['da_subln_g'][i].astype(F32).reshape(1, 2 * DA_HEAD_DIM)
    o = _diff_attention(proj, qcol, kcol, vcol, lam, subln, 1.0 - lam_init, seq_starts, seq_ends)

    w_out = p['ab_w_out'][i].astype(BF16)
    return (y_ssd, 0, o, 0, w_out)


def _retention_layer(x, a1, sh1, p, j, lay):
    seq_starts, seq_ends, groups = lay
    w_in = p['ret_w_in'][j].astype(BF16)
    lmax = max(g[2] for g in groups)
    tables = (*_rope_tables(lmax, RET_QK_DIM, 1.0), *_rope_tables(lmax, RET_QK_DIM, RET_QK_DIM ** -0.5))
    tn = 1024
    kcol = RET_QK_WIDTH
    vcol = 2 * RET_QK_WIDTH
    gcol = vcol + RET_V_WIDTH
    proj = _norm_mod_matmul(x, a1, sh1, w_in, None, seq_starts,
                            ((0, kcol // tn), (kcol // tn, vcol // tn)), RET_QK_DIM, tables)
    krt = proj[:, kcol:vcol].T
    lg_f = -jnp.exp(p['ret_log_decay_fwd'][j].astype(F32))
    lg_b = -jnp.exp(p['ret_log_decay_bwd'][j].astype(F32))
    of = _retention_direction(krt, proj, vcol, gcol, lg_f, seq_starts, seq_ends, reverse=False)
    o = _retention_direction(krt, proj, vcol, gcol, lg_b, seq_starts, seq_ends, reverse=True, o_fwd=of)
    w_out = p['ret_w_out'][j].astype(BF16)
    return (o, 0, o, 1, w_out)


def _moe_layer(x_parts, mix, g1, a2, sh2, g2, p, layer, lay, final_g, out_rows):
    seq_starts, _, _ = lay
    t = sum(xp.shape[0] for xp in x_parts)
    d = x_parts[0].shape[1]
    wr = jnp.concatenate([p['moe_w_group'][layer], p['moe_w_expert'][layer]], axis=1).astype(F32)
    br = jnp.concatenate([p['moe_b_group'][layer], p['moe_b_expert'][layer]]).astype(F32)
    nr = wr.shape[1]
    wr = jnp.pad(wr, ((0, 0), (0, ROUTER_PAD - nr)))
    br = jnp.pad(br, (0, ROUTER_PAD - nr)).reshape(1, ROUTER_PAD)
    x = _outproj_residual(*mix, x_parts, g1, seq_starts)
    h, route, cnt = _norm_mod_router(x, a2, sh2, wr, br, seq_starts)

    blk = MOE_ROW_BLOCK
    n_assign = t * MOE_TOP_K
    e_flat = route[:, 0:MOE_TOP_K].astype(jnp.int32).reshape(n_assign)
    rank = route[:, MOE_TOP_K:2 * MOE_TOP_K].astype(jnp.int32).reshape(n_assign)
    counts = cnt[0, MOE_GROUPS:MOE_GROUPS + MOE_EXPERTS].astype(jnp.int32)
    padded = (counts + blk - 1) // blk * blk
    ends_pad = jnp.cumsum(padded)
    start_pad = ends_pad - padded
    dest = (start_pad[e_flat] + rank).astype(jnp.int32)
    n_rows = n_assign + MOE_EXPERTS * blk
    n_blocks = n_rows // blk
    tok_flat = jnp.repeat(jnp.arange(t, dtype=jnp.int32), MOE_TOP_K)
    row_tok = (jnp.arange(n_rows, dtype=jnp.int32) % t).at[dest].set(tok_flat)
    ends_blk = jnp.cumsum(padded // blk).astype(jnp.int32)
    n_used = ends_blk[-1]
    pos = jnp.arange(n_blocks, dtype=jnp.int32)
    block = jnp.minimum(pos, n_used - 1)
    block_expert = jnp.minimum(jnp.sum(ends_blk[None, :] <= block[:, None], axis=1),
                               MOE_EXPERTS - 1).astype(jnp.int32)
    block_new = jnp.concatenate([jnp.ones((1,), jnp.int32),
                                 (block_expert[1:] != block_expert[:-1]).astype(jnp.int32)])
    run_start = jnp.where(block_new == 1, pos, n_blocks)
    next_start = lax.cummin(jnp.concatenate([run_start[1:], jnp.full((1,), n_blocks, jnp.int32)]), reverse=True)
    has_next = (next_start < n_blocks).astype(jnp.int32)
    next_expert = block_expert[jnp.minimum(next_start, n_blocks - 1)]
    tables = (block, block_expert, block_new, next_expert, has_next, n_used.reshape(1))

    xs = jnp.take(h, row_tok, axis=0, mode='clip')
    out = _moe_ffn(xs, tables, p['moe_w1'], p['moe_w3'], p['moe_w2'], layer)
    dest2 = dest.reshape(t, MOE_TOP_K)
    parts = [jnp.take(out, dest2[:, k], axis=0, mode='clip') for k in range(MOE_TOP_K)]
    return tuple(_moe_combine(x, parts, route, g2, final_g, seq_starts, out_rows))


def kernel(x_prompt, x_sample, c_prompt, c_sample, ada_w, ada_b, norm_mix_g, norm_ffn_g, ab_w_in, ssd_conv_w, ssd_conv_b, ssd_a_log_fwd, ssd_a_log_bwd, ssd_dt_bias_fwd, ssd_dt_bias_bwd, ssd_d, ssd_norm_g, da_lambda_q1, da_lambda_k1, da_lambda_q2, da_lambda_k2, da_subln_g, ab_w_out, ret_w_in, ret_log_decay_fwd, ret_log_decay_bwd, ret_w_out, moe_w_group, moe_b_group, moe_w_expert, moe_b_expert, moe_w1, moe_w3, moe_w2, final_norm_g):
    p = dict(ab_w_in=ab_w_in, ssd_conv_w=ssd_conv_w, ssd_conv_b=ssd_conv_b,
             ssd_a_log_fwd=ssd_a_log_fwd, ssd_a_log_bwd=ssd_a_log_bwd,
             ssd_dt_bias_fwd=ssd_dt_bias_fwd, ssd_dt_bias_bwd=ssd_dt_bias_bwd, ssd_d=ssd_d,
             ssd_norm_g=ssd_norm_g, da_lambda_q1=da_lambda_q1, da_lambda_k1=da_lambda_k1,
             da_lambda_q2=da_lambda_q2, da_lambda_k2=da_lambda_k2, da_subln_g=da_subln_g,
             ab_w_out=ab_w_out, ret_w_in=ret_w_in, ret_log_decay_fwd=ret_log_decay_fwd,
             ret_log_decay_bwd=ret_log_decay_bwd, ret_w_out=ret_w_out, moe_w_group=moe_w_group,
             moe_b_group=moe_b_group, moe_w_expert=moe_w_expert, moe_b_expert=moe_b_expert,
             moe_w1=moe_w1, moe_w3=moe_w3, moe_w2=moe_w2)
    bp, lp, d = x_prompt.shape
    bs, ls, _ = x_sample.shape
    depth = ada_w.shape[0]
    groups = ((0, bp, lp), (bp * lp, bs, ls))
    seq_starts = tuple(r0 + b * l for (r0, nb, l) in groups for b in range(nb))
    seq_ends = tuple(r0 + (b + 1) * l for (r0, nb, l) in groups for b in range(nb))
    lay = (seq_starts, seq_ends, groups)
    nseq = len(seq_starts)
    group_rows = (bp * lp, bs * ls)
    x = (x_prompt.reshape(bp * lp, d), x_sample.reshape(bs * ls, d))
    c = jnp.concatenate([c_prompt, c_sample], axis=0).astype(F32)
    c_pad = jnp.pad(c, ((0, -nseq % 8), (0, 0)))
    mod = _ada_modulation(c_pad, ada_w.astype(F32), ada_b.astype(F32))[:, :nseq]

    for layer in range(depth):
        sh1, sc1, g1, sh2, sc2, g2 = [m.reshape(nseq, 1, d) for m in jnp.split(mod[layer], 6, axis=-1)]
        a1 = norm_mix_g[layer].astype(F32)[None, None, :] * (1.0 + sc1)
        a2 = norm_ffn_g[layer].astype(F32)[None, None, :] * (1.0 + sc2)
        if layer % 2 == 0:
            mix = _ssd_diff_layer(x, a1, sh1, p, layer // 2, layer, lay)
        else:
            mix = _retention_layer(x, a1, sh1, p, layer // 2, lay)
        last = layer == depth - 1
        final_g = final_norm_g.astype(F32).reshape(1, d) if last else None
        x = _moe_layer(x, mix, g1, a2, sh2, g2, p, layer, lay, final_g,
                       group_rows if last else (sum(group_rows),))
    return (x[0].reshape(bp, lp, d), x[1].reshape(bs, ls, d))
```

```python
import functools
import math

import jax
import jax.numpy as jnp
from jax import lax
from jax.experimental import pallas as pl
from jax.experimental.pallas import tpu as pltpu

F32 = jnp.float32
BF16 = jnp.bfloat16
HIGHEST = lax.Precision.HIGHEST

EPS = 1e-6
SUBLN_EPS = 1e-5
ROPE_THETA = 10000.0

SSD_HEADS = 32
SSD_HEAD_DIM = 64
SSD_WIDTH = SSD_HEADS * SSD_HEAD_DIM
SSD_GROUPS = 4
SSD_STATE = 128
SSD_CONV = 5
SSD_BC = SSD_GROUPS * SSD_STATE
SSD_XBC = SSD_WIDTH + 2 * SSD_BC
SSD_CHUNK = 128
DA_HEADS = 8
DA_HEAD_DIM = 128
DA_QK_WIDTH = 2 * DA_HEADS * DA_HEAD_DIM
DA_V_WIDTH = DA_HEADS * 2 * DA_HEAD_DIM
RET_HEADS = 8
RET_QK_DIM = 256
RET_V_DIM = 512
RET_QK_WIDTH = RET_HEADS * RET_QK_DIM
RET_V_WIDTH = RET_HEADS * RET_V_DIM
RET_CHUNK = 256
MOE_GROUPS = 4
MOE_EXPERTS_PER_GROUP = 8
MOE_EXPERTS = MOE_GROUPS * MOE_EXPERTS_PER_GROUP
MOE_TOP_K = 2
MOE_ROW_BLOCK = 512
MOE_FF_CHUNKS = 2
NMM_COL_CHUNK = 256
ROUTER_PAD = 128
DT_PAD = 128
HALO = 16

VMEM_LIMIT_BYTES = 56 * 1024 * 1024
MOE_VMEM_LIMIT_BYTES = 62 * 1024 * 1024


def _params(*semantics, vmem_limit_bytes=VMEM_LIMIT_BYTES):
    return pltpu.CompilerParams(dimension_semantics=semantics, vmem_limit_bytes=vmem_limit_bytes)


def _silu(x):
    return x * jax.nn.sigmoid(x)


def _softplus(x):
    return jnp.maximum(x, 0.0) + jnp.log1p(jnp.exp(-jnp.abs(x)))


def _lane_repeat(x, n):
    return x if n == 1 else jnp.concatenate([x] * n, axis=1)


def _tile(n, pref):
    t = min(n, pref)
    assert n % t == 0, (n, pref)
    return t


def _row_tile(t, pref, seq_starts):
    tm = min(t, pref)
    while t % tm or any(st % tm for st in seq_starts):
        tm //= 2
    return tm


def _seq_index(row, seq_starts):
    s = 0
    for st in seq_starts[1:]:
        s = s + jnp.where(row >= st, 1, 0)
    return s


def _is_any(row, values):
    hit = row == values[0]
    for v in values[1:]:
        hit = jnp.logical_or(hit, row == v)
    return hit


def _ada_kernel(c_ref, w_ref, b_ref, o_ref):
    o_ref[...] = jnp.dot(_silu(c_ref[...]), w_ref[...], preferred_element_type=F32,
                         precision=HIGHEST) + b_ref[...]


def _ada_modulation(c_pad, ada_w, ada_b):
    depth, d, n = ada_w.shape
    rows = c_pad.shape[0]
    tn = _tile(n, 1024)
    return pl.pallas_call(
        _ada_kernel,
        grid=(depth, n // tn),
        in_specs=[pl.BlockSpec((rows, d), lambda l, j: (0, 0)),
                  pl.BlockSpec((None, d, tn), lambda l, j: (l, 0, j)),
                  pl.BlockSpec((None, 1, tn), lambda l, j: (l, 0, j))],
        out_specs=pl.BlockSpec((None, rows, tn), lambda l, j: (l, 0, j)),
        out_shape=jax.ShapeDtypeStruct((depth, rows, n), F32),
        compiler_params=_params("arbitrary", "arbitrary"),
        name="ada_modulation",
    )(c_pad, ada_w, ada_b.reshape(depth, 1, n))


def _norm_mod(x, a, sh):
    ms = jnp.mean(x * x, axis=-1, keepdims=True)
    return (x * lax.rsqrt(ms + EPS)) * a + sh


def _rotate_half_tile(x, cos, sin, head_dim):
    half = head_dim // 2
    outs = []
    for h in range(x.shape[1] // head_dim):
        xh = x[:, h * head_dim:(h + 1) * head_dim]
        if half % 128 == 0:
            rot = jnp.concatenate([xh[:, half:], xh[:, :half]], axis=-1)
        else:
            rot = pltpu.roll(xh, half, 1)
        outs.append(xh * cos + rot * sin)
    return jnp.concatenate(outs, axis=-1)


def _row_part_specs(parts, tm, width, row_of, col_of, single_buffer=False):
    specs, bounds = [], [0]
    mode = dict(pipeline_mode=pl.Buffered(1)) if single_buffer else {}
    for arr in parts:
        b0, nb = bounds[-1], arr.shape[0] // tm
        assert arr.shape[0] % tm == 0
        specs.append(pl.BlockSpec(
            (tm, width), lambda *g, b0=b0, nb=nb: (jnp.clip(row_of(*g) - b0, 0, nb - 1), col_of(*g)), **mode))
        bounds.append(b0 + nb)
    return specs, tuple(bounds)


def _for_row_part(i, bounds, fn):
    if len(bounds) == 2:
        fn(0)
        return
    for k in range(len(bounds) - 1):
        pl.when(jnp.logical_and(i >= bounds[k], i < bounds[k + 1]))(functools.partial(fn, k))


def _nmm_kernel(*refs, with_dt, rope_tiles, head_dim, bounds):
    n_parts = len(bounds) - 1
    x_refs = refs[:n_parts]
    a_ref, sh_ref, w_ref, cq_ref, sq_ref, ck_ref, sk_ref = refs[n_parts:n_parts + 7]
    if with_dt:
        wdt_ref, o_ref, odt_ref, h_scr = refs[n_parts + 7:]
    else:
        o_ref, h_scr = refs[n_parts + 7:]
    j = pl.program_id(1)

    def normalize(k):
        hb = _norm_mod(x_refs[k][...], a_ref[...], sh_ref[...]).astype(BF16)
        h_scr[...] = hb
        if with_dt:
            odt_ref[...] = jnp.dot(hb, wdt_ref[...], preferred_element_type=F32)

    @pl.when(j == 0)
    def _():
        _for_row_part(pl.program_id(0), bounds, normalize)

    (q_lo, q_hi), (k_lo, k_hi) = rope_tiles
    is_q = jnp.logical_and(j >= q_lo, j < q_hi)
    is_k = jnp.logical_and(j >= k_lo, j < k_hi)
    rotary = jnp.logical_or(is_q, is_k)
    cos = jnp.where(rotary, jnp.where(is_q, cq_ref[...], ck_ref[...]), 1.0)
    sin = jnp.where(rotary, jnp.where(is_q, sq_ref[...], sk_ref[...]), 0.0)
    cw = max(head_dim, NMM_COL_CHUNK)
    for c in range(w_ref.shape[1] // cw):
        acc = jnp.dot(h_scr[...], w_ref[:, c * cw:(c + 1) * cw], preferred_element_type=F32)
        o_ref[:, c * cw:(c + 1) * cw] = _rotate_half_tile(acc, cos, sin, head_dim).astype(o_ref.dtype)


def _norm_mod_matmul(x_parts, a, sh, w, wdt, seq_starts, rope_tiles, head_dim, tables):
    t = sum(xp.shape[0] for xp in x_parts)
    d = x_parts[0].shape[1]
    n = w.shape[1]
    tm = _row_tile(t, 1024, seq_starts)
    tn = _tile(n, 1024)
    with_dt = wdt is not None

    def seq_map(i, j):
        return (_seq_index(i * tm, seq_starts), 0, 0)

    def pos_map(i, j):
        r = i * tm
        s = _seq_index(r, seq_starts)
        st = 0
        for k, v in enumerate(seq_starts):
            st = st + jnp.where(s == k, v, 0)
        return ((r - st) // tm, 0)

    x_specs, bounds = _row_part_specs(x_parts, tm, d, lambda i, j: i, lambda i, j: 0,
                                      single_buffer=len(x_parts) > 1)
    in_specs = x_specs + [pl.BlockSpec((None, 1, d), seq_map),
                          pl.BlockSpec((None, 1, d), seq_map),
                          pl.BlockSpec((d, tn), lambda i, j: (0, j))] + [pl.BlockSpec((tm, head_dim), pos_map)] * 4
    out_specs = [pl.BlockSpec((tm, tn), lambda i, j: (i, j))]
    out_shape = [jax.ShapeDtypeStruct((t, n), BF16)]
    args = [*x_parts, a, sh, w, *tables]
    if with_dt:
        in_specs.append(pl.BlockSpec((d, DT_PAD), lambda i, j: (0, 0)))
        out_specs.append(pl.BlockSpec((tm, DT_PAD), lambda i, j: (i, 0)))
        out_shape.append(jax.ShapeDtypeStruct((t, DT_PAD), F32))
        args.append(wdt)
    res = pl.pallas_call(
        functools.partial(_nmm_kernel, with_dt=with_dt, rope_tiles=rope_tiles, head_dim=head_dim, bounds=bounds),
        grid=(t // tm, n // tn),
        in_specs=in_specs, out_specs=out_specs, out_shape=out_shape,
        scratch_shapes=[pltpu.VMEM((tm, d), BF16)],
        compiler_params=_params("arbitrary", "arbitrary"),
        name="norm_mod_matmul",
    )(*args)
    return res if with_dt else res[0]


def _conv_kernel(cur_ref, prev_ref, next_ref, w_ref, b_ref, o_ref, ext_scr, *, tm, seq_starts, seq_ends):
    r0 = pl.program_id(0) * tm
    at_start = _is_any(r0, seq_starts)
    at_end = _is_any(r0 + tm, seq_ends)
    prev = prev_ref[...].astype(F32)[HALO - 8:HALO]
    nxt = next_ref[...].astype(F32)[0:8]
    ext_scr[0:8, :] = jnp.where(at_start, 0.0, prev)
    ext_scr[8:8 + tm, :] = cur_ref[...].astype(F32)
    ext_scr[8 + tm:16 + tm, :] = jnp.where(at_end, 0.0, nxt)
    ext = ext_scr[...]
    n = tm + 16
    pad = SSD_CONV // 2
    below = w_ref[0:1, :] * ext
    for k in range(1, pad):
        below = w_ref[k:k + 1, :] * ext + pltpu.roll(below, 1, 0)
    above = w_ref[SSD_CONV - 1:SSD_CONV, :] * ext
    for k in range(SSD_CONV - 2, pad, -1):
        above = w_ref[k:k + 1, :] * ext + pltpu.roll(above, n - 1, 0)
    acc = w_ref[pad:pad + 1, :] * ext + pltpu.roll(below, 1, 0) + pltpu.roll(above, n - 1, 0)
    acc = acc[8:8 + tm, :] + b_ref[...]
    o_ref[...] = _silu(acc).astype(o_ref.dtype)


def _conv_silu(proj, col0, conv_w, conv_b, seq_starts, seq_ends):
    t = proj.shape[0]
    tm = _row_tile(t, 512, seq_starts)
    tc = 1024
    assert col0 % tc == 0 and SSD_XBC % tc == 0 and tm % HALO == 0
    cb = col0 // tc
    hb = tm // HALO
    last_halo = t // HALO - 1
    return pl.pallas_call(
        functools.partial(_conv_kernel, tm=tm, seq_starts=seq_starts, seq_ends=seq_ends),
        grid=(t // tm, SSD_XBC // tc),
        in_specs=[pl.BlockSpec((tm, tc), lambda i, j: (i, cb + j)),
                  pl.BlockSpec((HALO, tc), lambda i, j: (jnp.maximum(i * hb - 1, 0), cb + j)),
                  pl.BlockSpec((HALO, tc), lambda i, j: (jnp.minimum((i + 1) * hb, last_halo), cb + j)),
                  pl.BlockSpec((SSD_CONV, tc), lambda i, j: (0, j)),
                  pl.BlockSpec((1, tc), lambda i, j: (0, j))],
        out_specs=pl.BlockSpec((tm, tc), lambda i, j: (i, j)),
        out_shape=jax.ShapeDtypeStruct((t, SSD_XBC), BF16),
        scratch_shapes=[pltpu.VMEM((tm + 16, tc), F32)],
        compiler_params=_params("arbitrary", "arbitrary"),
        name="ssd_conv_silu",
    )(proj, proj, proj, conv_w, conv_b.reshape(1, SSD_XBC))


def _ssd_kernel(*refs, reverse, dcol, nchunks, seq_starts, seq_ends):
    ch = SSD_CHUNK
    if reverse:
        (xact_ref, bt_ref, dt_ref, dtt_ref, bias_ref, a_ref, biast_ref, at_ref, e_ref,
         z_ref, yf_ref, dskip_ref, g_ref, o_ref, h_scr, y_scr) = refs
    else:
        (xact_ref, bt_ref, dt_ref, dtt_ref, bias_ref, a_ref, biast_ref, at_ref, e_ref,
         o_ref, h_scr) = refs
    step = pl.program_id(0)
    if reverse:
        row_end = (nchunks - step) * ch
        fresh = _is_any(row_end, seq_ends)
    else:
        fresh = _is_any(step * ch, seq_starts)

    @pl.when(fresh)
    def _():
        h_scr[...] = jnp.zeros_like(h_scr)

    dt = _softplus(dt_ref[:, dcol:dcol + SSD_HEADS] + bias_ref[...])
    dta = dt * a_ref[...]
    dtt = _softplus(dtt_ref[dcol:dcol + SSD_HEADS, :] + biast_ref[...])
    dtat = dtt * at_ref[...]
    row = lax.broadcasted_iota(jnp.int32, (ch, ch), 0)
    col = lax.broadcasted_iota(jnp.int32, (ch, ch), 1)
    if reverse:
        keep = col >= row
    else:
        keep = col <= row
    tri = jnp.where(keep, 1.0, 0.0).astype(F32)
    trit = jnp.where(keep, 0.0, 1.0).astype(F32) + jnp.where(row == col, 1.0, 0.0).astype(F32)
    cs = jnp.dot(tri, dta, preferred_element_type=F32, precision=HIGHEST)
    cst = jnp.dot(dtat, trit, preferred_element_type=F32, precision=HIGHEST)
    last = 0 if reverse else ch - 1
    ecs = jnp.exp(cs)
    wt = dtt * jnp.exp(cst[:, last:last + 1] - cst)
    cdec_x = jnp.dot(jnp.broadcast_to(ecs[last:last + 1, :], (8, SSD_HEADS)), e_ref[...],
                     preferred_element_type=F32, precision=HIGHEST)[0:1, :]

    lane = lax.broadcasted_iota(jnp.int32, (1, 2 * SSD_HEAD_DIM), 1)
    first = lane < SSD_HEAD_DIM
    gw = SSD_WIDTH // SSD_GROUPS
    hpg = SSD_HEADS // SSD_GROUPS
    pw = 2 * SSD_HEAD_DIM
    ssq = jnp.zeros((ch, pw), F32)
    for g in range(SSD_GROUPS):
        bgt = bt_ref[g * SSD_STATE:(g + 1) * SSD_STATE, :]
        c0 = SSD_WIDTH + SSD_BC + g * SSD_STATE
        cg = xact_ref[:, c0:c0 + SSD_STATE]
        scores = jnp.dot(cg, bgt, preferred_element_type=F32)
        cgf = cg.astype(F32)
        bgtf = bgt.astype(F32)
        for pr in range(hpg // 2):
            lo = g * gw + pr * pw
            hi = lo + pw
            xpair = xact_ref[:, lo:hi].astype(F32)
            hpair = h_scr[g, :, pr * pw:(pr + 1) * pw]
            xsel = (jnp.where(first, xpair, 0.0).astype(BF16), jnp.where(first, 0.0, xpair).astype(BF16))
            hsel = (jnp.where(first, hpair, 0.0).astype(BF16), jnp.where(first, 0.0, hpair).astype(BF16))
            y = None
            bws = []
            for u in range(2):
                h = g * hpg + 2 * pr + u
                csb = jnp.broadcast_to(cs[:, h:h + 1], (ch, ch))
                seg = jnp.where(keep, csb - cst[h:h + 1, :], -jnp.inf)
                m = (scores * jnp.exp(seg) * dtt[h:h + 1, :]).astype(BF16)
                ce = (cgf * jnp.exp(csb)).astype(BF16)
                part = jnp.dot(jnp.concatenate([m, ce], axis=1), jnp.concatenate([xsel[u], hsel[u]], axis=0),
                               preferred_element_type=F32)
                y = part if y is None else y + part
                bws.append((bgtf * wt[h:h + 1, :]).astype(BF16))
            h_scr[g, :, pr * pw:(pr + 1) * pw] = hpair * cdec_x[:, lo:hi] + jnp.dot(
                jnp.concatenate(bws, axis=1), jnp.concatenate(xsel, axis=0), preferred_element_type=F32)
            if reverse:
                y = y + yf_ref[:, lo:hi].astype(F32) + dskip_ref[:, lo:hi] * xpair
                y = y * _silu(z_ref[:, lo:hi].astype(F32))
                ssq = ssq + y * y
                y_scr[:, lo:hi] = y
            else:
                o_ref[:, lo:hi] = y.astype(o_ref.dtype)
    if reverse:
        inv = lax.rsqrt(jnp.sum(ssq, axis=-1, keepdims=True) * (1.0 / SSD_WIDTH) + EPS)
        o_ref[...] = (y_scr[...] * inv * g_ref[...]).astype(o_ref.dtype)


def _ssd_direction(xact, bt, dt, dtt, bias, a, expand, seq_starts, seq_ends, reverse, final=None):
    t = xact.shape[0]
    ch = SSD_CHUNK
    nchunks = t // ch
    dcol = SSD_HEADS if reverse else 0

    def rb(i):
        return nchunks - 1 - i if reverse else i

    in_specs = [pl.BlockSpec((ch, SSD_XBC), lambda i: (rb(i), 0)),
                pl.BlockSpec((SSD_BC, ch), lambda i: (0, rb(i))),
                pl.BlockSpec((ch, DT_PAD), lambda i: (rb(i), 0)),
                pl.BlockSpec((DT_PAD, ch), lambda i: (0, rb(i))),
                pl.BlockSpec((1, SSD_HEADS), lambda i: (0, 0)),
                pl.BlockSpec((1, SSD_HEADS), lambda i: (0, 0)),
                pl.BlockSpec((SSD_HEADS, 1), lambda i: (0, 0)),
                pl.BlockSpec((SSD_HEADS, 1), lambda i: (0, 0)),
                pl.BlockSpec((SSD_HEADS, SSD_WIDTH), lambda i: (0, 0))]
    args = [xact, bt, dt, dtt, bias.reshape(1, -1), a.reshape(1, -1), bias.reshape(-1, 1), a.reshape(-1, 1), expand]
    scratch = [pltpu.VMEM((SSD_GROUPS, SSD_STATE, SSD_WIDTH // SSD_GROUPS), F32)]
    if reverse:
        proj, yf, dskip, g = final
        in_specs += [pl.BlockSpec((ch, SSD_WIDTH), lambda i: (rb(i), 0)),
                     pl.BlockSpec((ch, SSD_WIDTH), lambda i: (rb(i), 0)),
                     pl.BlockSpec((1, SSD_WIDTH), lambda i: (0, 0)),
                     pl.BlockSpec((1, SSD_WIDTH), lambda i: (0, 0))]
        args += [proj, yf, dskip, g]
        scratch.append(pltpu.VMEM((ch, SSD_WIDTH), F32))
    return pl.pallas_call(
        functools.partial(_ssd_kernel, reverse=reverse, dcol=dcol, nchunks=nchunks,
                          seq_starts=seq_starts, seq_ends=seq_ends),
        grid=(nchunks,),
        in_specs=in_specs,
        out_specs=pl.BlockSpec((ch, SSD_WIDTH), lambda i: (rb(i), 0)),
        out_shape=jax.ShapeDtypeStruct((t, SSD_WIDTH), BF16),
        scratch_shapes=scratch,
        compiler_params=_params("arbitrary"),
        name="ssd_bwd_gate_norm" if reverse else "ssd_fwd",
    )(*args)


def _rope_tables(lmax, head_dim, scale):
    half = head_dim // 2
    inv = 1.0 / (ROPE_THETA ** (jnp.arange(half, dtype=F32) / half))
    ang = jnp.arange(lmax, dtype=F32)[:, None] * inv[None, :]
    cos = jnp.cos(ang) * scale
    sin = jnp.sin(ang) * scale
    return jnp.concatenate([cos, cos], axis=-1), jnp.concatenate([-sin, sin], axis=-1)


def _seq_lookup(row, seq_starts, values):
    s = _seq_index(row, seq_starts)
    out = 0
    for k, v in enumerate(values):
        out = out + jnp.where(s == k, v, 0)
    return out


def _attn_kernel(lam_ref, q_ref, k_ref, v_ref, g_ref, o_ref, m_scr, l_scr, acc_scr, s_scr, p_scr, alpha_scr, *,
                 tk, nkv, unroll, strip, out_scale, seq_starts, seq_kv_blocks, seq_sub_block):
    dh = DA_HEAD_DIM
    tq = q_ref.shape[0]
    lanes = m_scr.shape[-1]
    row0 = pl.program_id(1) * tq
    n_valid = _seq_lookup(row0, seq_starts, seq_kv_blocks)
    sub0 = _seq_lookup(row0, seq_starts, seq_sub_block)
    m_scr[...] = jnp.full_like(m_scr, -jnp.inf)
    l_scr[...] = jnp.zeros_like(l_scr)
    acc_scr[...] = jnp.zeros_like(acc_scr)

    def chunk(c):
        sub = sub0 + c // nkv
        off = pl.multiple_of((c % nkv) * tk, tk)
        for u in range(2):
            s_scr[u] = lax.dot_general(q_ref[:, u * dh:(u + 1) * dh],
                                       k_ref[sub, pl.ds(off, tk), u * dh:(u + 1) * dh],
                                       (((1,), (1,)), ((), ())), preferred_element_type=F32)
        for u in range(2):
            for r in range(tq // strip):
                rows = slice(r * strip, (r + 1) * strip)
                s = s_scr[u, rows, :]
                m_prev = m_scr[u, rows, :]
                m_new = jnp.maximum(m_prev, jnp.max(s, axis=-1, keepdims=True))
                alpha = jnp.exp2(m_prev - m_new)
                p = jnp.exp2(s - _lane_repeat(m_new, tk // lanes))
                l_scr[u, rows, :] = alpha * l_scr[u, rows, :] + jnp.sum(p, axis=-1, keepdims=True)
                m_scr[u, rows, :] = m_new
                alpha_scr[u, rows, :] = alpha
                p_scr[u, rows, :] = p.astype(BF16)
            pv = jnp.dot(p_scr[u], v_ref[sub, pl.ds(off, tk), :], preferred_element_type=F32)
            acc_scr[u] = acc_scr[u] * _lane_repeat(alpha_scr[u], acc_scr.shape[-1] // lanes) + pv

    def body(j, carry):
        for w in range(unroll):
            chunk(j * unroll + w)
        return carry

    lax.fori_loop(0, n_valid * (nkv // unroll), body, 0)
    lam = lam_ref[0]
    rep = acc_scr.shape[-1] // lanes
    o = (acc_scr[0] / _lane_repeat(l_scr[0], rep)
         - lam * (acc_scr[1] / _lane_repeat(l_scr[1], rep)))
    ms = jnp.mean(o * o, axis=-1, keepdims=True)
    o_ref[...] = (o * lax.rsqrt(ms + SUBLN_EPS) * g_ref[...] * out_scale).astype(o_ref.dtype)


def _diff_attention(proj, qcol0, kcol0, vcol0, lam, subln_g, out_scale, seq_starts, seq_ends):
    t = proj.shape[0]
    pw = 2 * DA_HEAD_DIM
    lens = tuple(e - s for s, e in zip(seq_starts, seq_ends))
    kvb = min(lens)
    group = max(lens) // kvb
    assert all(n % kvb == 0 for n in lens) and all(s % kvb == 0 for s in seq_starts) and (t // kvb) % group == 0
    tq = _tile(kvb, 1024)
    tk = _tile(kvb, 2048)
    nkv = kvb // tk
    strip = 16
    lanes = 128
    assert vcol0 % pw == 0 and qcol0 % pw == 0 and kcol0 % pw == 0
    qc0, kc0, vc0 = qcol0 // pw, kcol0 // pw, vcol0 // pw
    seq_kv_blocks = tuple(n // kvb for n in lens)
    seq_group = tuple((s // kvb) // group for s in seq_starts)
    seq_sub_block = tuple((s // kvb) % group for s in seq_starts)
    assert all(sb + nb <= group for sb, nb in zip(seq_sub_block, seq_kv_blocks))
    kv_view = proj.reshape(t // kvb, kvb, proj.shape[1])

    def kv_map(col0):
        return lambda h, i: (_seq_lookup(i * tq, seq_starts, seq_group), 0, col0 + h)

    return pl.pallas_call(
        functools.partial(_attn_kernel, tk=tk, nkv=nkv, unroll=2 if nkv % 2 == 0 else 1, strip=strip,
                          out_scale=out_scale, seq_starts=seq_starts, seq_kv_blocks=seq_kv_blocks,
                          seq_sub_block=seq_sub_block),
        grid=(DA_HEADS, t // tq),
        in_specs=[pl.BlockSpec(memory_space=pltpu.SMEM),
                  pl.BlockSpec((tq, pw), lambda h, i: (i, qc0 + h)),
                  pl.BlockSpec((group, kvb, pw), kv_map(kc0)),
                  pl.BlockSpec((group, kvb, pw), kv_map(vc0)),
                  pl.BlockSpec((1, pw), lambda h, i: (0, 0))],
        out_specs=pl.BlockSpec((tq, pw), lambda h, i: (i, h)),
        out_shape=jax.ShapeDtypeStruct((t, DA_V_WIDTH), BF16),
        scratch_shapes=[pltpu.VMEM((2, tq, lanes), F32), pltpu.VMEM((2, tq, lanes), F32),
                        pltpu.VMEM((2, tq, pw), F32), pltpu.VMEM((2, tq, tk), F32), pltpu.VMEM((2, tq, tk), BF16),
                        pltpu.VMEM((2, tq, lanes), F32)],
        compiler_params=_params("arbitrary", "arbitrary"),
        name="diff_attention",
    )(lam, proj, kv_view, kv_view, subln_g)


def _outproj_kernel(y1_ref, y2_ref, w1_ref, w2_ref, g_ref, *rest, bounds):
    x_refs, o_ref = rest[:-1], rest[-1]
    acc = jnp.dot(y1_ref[...], w1_ref[...], preferred_element_type=F32)
    acc = acc + jnp.dot(y2_ref[...], w2_ref[...], preferred_element_type=F32)
    upd = g_ref[...] * acc

    def residual(k):
        o_ref[...] = x_refs[k][...] + upd

    _for_row_part(pl.program_id(1), bounds, residual)


def _outproj_residual(y1, c1, y2, c2, w, x_parts, gate, seq_starts):
    t = sum(xp.shape[0] for xp in x_parts)
    d = x_parts[0].shape[1]
    kh = w.shape[0] // 2
    tm = _row_tile(t, 512, seq_starts)
    tn = _tile(d, 1024)
    x_specs, bounds = _row_part_specs(x_parts, tm, tn, lambda j, i: i, lambda j, i: j)
    return pl.pallas_call(
        functools.partial(_outproj_kernel, bounds=bounds),
        grid=(d // tn, t // tm),
        in_specs=[pl.BlockSpec((tm, kh), lambda j, i: (i, c1)),
                  pl.BlockSpec((tm, kh), lambda j, i: (i, c2)),
                  pl.BlockSpec((kh, tn), lambda j, i: (0, j)),
                  pl.BlockSpec((kh, tn), lambda j, i: (1, j)),
                  pl.BlockSpec((None, 1, tn), lambda j, i: (_seq_index(i * tm, seq_starts), 0, j))] + x_specs,
        out_specs=pl.BlockSpec((tm, tn), lambda j, i: (i, j)),
        out_shape=jax.ShapeDtypeStruct((t, d), F32),
        compiler_params=_params("arbitrary", "arbitrary"),
        name="outproj_residual",
    )(y1, y2, w, w, gate, *x_parts)


def _router_kernel(x_ref, a_ref, sh_ref, wr_ref, br_ref, h_ref, rt_ref, cnt_ref, carry_scr):
    @pl.when(pl.program_id(0) == 0)
    def _():
        carry_scr[...] = jnp.zeros_like(carry_scr)

    h = _norm_mod(x_ref[...], a_ref[...], sh_ref[...])
    h_ref[...] = h.astype(h_ref.dtype)
    lg = jnp.dot(h, wr_ref[...], preferred_element_type=F32, precision=HIGHEST) + br_ref[...]
    tm, lanes = lg.shape
    lane = lax.broadcasted_iota(jnp.int32, (tm, lanes), 1).astype(F32)
    none = float(lanes)

    def top1(mask):
        v = jnp.max(jnp.where(mask, lg, -jnp.inf), axis=-1, keepdims=True)
        idx = jnp.min(jnp.where(mask, jnp.where(lg == v, lane, none), none), axis=-1, keepdims=True)
        return v, idx

    is_group = lane < MOE_GROUPS
    cm, grp = top1(is_group)
    p_grp = 1.0 / jnp.sum(jnp.where(is_group, jnp.exp(lg - cm), 0.0), axis=-1, keepdims=True)
    lo = MOE_GROUPS + grp * MOE_EXPERTS_PER_GROUP
    in_grp = jnp.logical_and(lane >= lo, lane < lo + MOE_EXPERTS_PER_GROUP)
    v1, i1 = top1(in_grp)
    v2, i2 = top1(jnp.logical_and(in_grp, lane != i1))
    e12 = jnp.exp(v2 - v1)
    g1 = p_grp / (1.0 + e12)
    g2 = g1 * e12

    onehot = jnp.where(lane == i1, 1.0, 0.0) + jnp.where(lane == i2, 1.0, 0.0)
    row = lax.broadcasted_iota(jnp.int32, (tm, tm), 0)
    col = lax.broadcasted_iota(jnp.int32, (tm, tm), 1)
    before = jnp.where(col < row, 1.0, 0.0).astype(BF16)
    base = carry_scr[0:1, :] + jnp.dot(before, onehot.astype(BF16), preferred_element_type=F32)
    r1 = jnp.sum(jnp.where(lane == i1, base, 0.0), axis=-1, keepdims=True)
    r2 = jnp.sum(jnp.where(lane == i2, base, 0.0), axis=-1, keepdims=True)
    carry_scr[...] = carry_scr[...] + jnp.sum(onehot, axis=0, keepdims=True)
    cnt_ref[...] = carry_scr[...]

    fields = (i1 - MOE_GROUPS, i2 - MOE_GROUPS, r1, r2, g1, g2)
    rt = jnp.zeros((tm, lanes), F32)
    for k, f in enumerate(fields):
        rt = jnp.where(lane == k, f, rt)
    rt_ref[...] = rt


def _norm_mod_router(x, a, sh, wr, br, seq_starts):
    t, d = x.shape
    tm = _row_tile(t, 512, seq_starts)

    def seq_map(i):
        return (_seq_index(i * tm, seq_starts), 0, 0)

    return pl.pallas_call(
        _router_kernel,
        grid=(t // tm,),
        in_specs=[pl.BlockSpec((tm, d), lambda i: (i, 0)),
                  pl.BlockSpec((None, 1, d), seq_map),
                  pl.BlockSpec((None, 1, d), seq_map),
                  pl.BlockSpec((d, ROUTER_PAD), lambda i: (0, 0)),
                  pl.BlockSpec((1, ROUTER_PAD), lambda i: (0, 0))],
        out_specs=[pl.BlockSpec((tm, d), lambda i: (i, 0)),
                   pl.BlockSpec((tm, ROUTER_PAD), lambda i: (i, 0)),
                   pl.BlockSpec((8, ROUTER_PAD), lambda i: (0, 0))],
        out_shape=[jax.ShapeDtypeStruct((t, d), BF16), jax.ShapeDtypeStruct((t, ROUTER_PAD), F32),
                   jax.ShapeDtypeStruct((8, ROUTER_PAD), F32)],
        scratch_shapes=[pltpu.VMEM((8, ROUTER_PAD), F32)],
        compiler_params=_params("arbitrary"),
        name="norm_mod_router",
    )(x, a, sh, wr, br)


def _moe_ffn_kernel(ib_ref, ie_ref, inew_ref, nxe_ref, hasnx_ref, nu_ref,
                    xs_ref, w1_hbm, w3_hbm, w2_hbm, o_ref,
                    w1_stage, w3_stage, w2_stage, w1_scr, w3_scr, w2_scr, sems, *, layer, ff_chunks):
    del ib_ref
    s = pl.program_id(0)

    @pl.when(s >= nu_ref[0])
    def _():
        o_ref[...] = jnp.zeros_like(o_ref)

    def weight_copies(e):
        return (pltpu.make_async_copy(w1_hbm.at[layer, e], w1_stage, sems.at[0]),
                pltpu.make_async_copy(w3_hbm.at[layer, e], w3_stage, sems.at[1]),
                pltpu.make_async_copy(w2_hbm.at[layer, e], w2_stage, sems.at[2]))

    @pl.when(s == 0)
    def _():
        for c in weight_copies(ie_ref[0]):
            c.start()

    @pl.when(s < nu_ref[0])
    def _():
        @pl.when(inew_ref[s] == 1)
        def _():
            for c in weight_copies(ie_ref[s]):
                c.wait()
            w1_scr[...] = w1_stage[...].astype(BF16)
            w3_scr[...] = w3_stage[...].astype(BF16)
            w2_scr[...] = w2_stage[...].astype(BF16)

            @pl.when(hasnx_ref[s] == 1)
            def _():
                for c in weight_copies(nxe_ref[s]):
                    c.start()

        x = xs_ref[...]
        fc = w1_scr.shape[1] // ff_chunks
        acc = None
        for k in range(ff_chunks):
            h1 = jnp.dot(x, w1_scr[:, k * fc:(k + 1) * fc], preferred_element_type=F32)
            h3 = jnp.dot(x, w3_scr[:, k * fc:(k + 1) * fc], preferred_element_type=F32)
            hid = (_silu(h1) * h3).astype(BF16)
            part = jnp.dot(hid, w2_scr[k * fc:(k + 1) * fc, :], preferred_element_type=F32)
            acc = part if acc is None else acc + part
        o_ref[...] = acc.astype(o_ref.dtype)


def _moe_ffn(xs, tables, w1, w3, w2, layer):
    n_rows, d = xs.shape
    ff = w1.shape[3]
    blk = MOE_ROW_BLOCK
    n_tab = len(tables)

    grid_spec = pltpu.PrefetchScalarGridSpec(
        num_scalar_prefetch=n_tab,
        grid=(n_rows // blk,),
        in_specs=[pl.BlockSpec((blk, d), lambda s, ib, *_: (ib[s], 0)),
                  pl.BlockSpec(memory_space=pl.ANY),
                  pl.BlockSpec(memory_space=pl.ANY),
                  pl.BlockSpec(memory_space=pl.ANY)],
        out_specs=pl.BlockSpec((blk, d), lambda s, *_: (s, 0)),
        scratch_shapes=[pltpu.VMEM((d, ff), F32), pltpu.VMEM((d, ff), F32), pltpu.VMEM((ff, d), F32),
                        pltpu.VMEM((d, ff), BF16), pltpu.VMEM((d, ff), BF16), pltpu.VMEM((ff, d), BF16),
                        pltpu.SemaphoreType.DMA((3,))])
    return pl.pallas_call(
        functools.partial(_moe_ffn_kernel, layer=layer, ff_chunks=MOE_FF_CHUNKS),
        grid_spec=grid_spec,
        out_shape=jax.ShapeDtypeStruct((n_rows, d), BF16),
        compiler_params=_params("arbitrary", vmem_limit_bytes=MOE_VMEM_LIMIT_BYTES),
        name="moe_ffn",
    )(*tables, xs, w1, w3, w2)


def _combine_kernel(x_ref, o0_ref, o1_ref, gt_ref, g_ref, *rest, final, bounds):
    if final:
        fg_ref, o_refs = rest[0], rest[1:]
    else:
        o_refs = rest
    gt = gt_ref[...]
    g0 = 2 * MOE_TOP_K
    y = gt[:, g0:g0 + 1] * o0_ref[...].astype(F32) + gt[:, g0 + 1:g0 + 2] * o1_ref[...].astype(F32)
    x = x_ref[...] + g_ref[...] * y
    if final:
        ms = jnp.mean(x * x, axis=-1, keepdims=True)
        x = x * lax.rsqrt(ms + EPS) * fg_ref[...]

    def write(k):
        o_refs[k][...] = x

    _for_row_part(pl.program_id(0), bounds, write)


def _moe_combine(x, parts, gates, gate_mod, final_g, seq_starts, out_rows):
    t, d = x.shape
    tm = _row_tile(t, 512, seq_starts)
    final = final_g is not None
    out_shape = [jax.ShapeDtypeStruct((r, d), F32) for r in out_rows]
    out_specs, bounds = _row_part_specs(out_shape, tm, d, lambda i: i, lambda i: 0)
    in_specs = ([pl.BlockSpec((tm, d), lambda i: (i, 0))]
                + [pl.BlockSpec((tm, d), lambda i: (i, 0))] * len(parts)
                + [pl.BlockSpec((tm, ROUTER_PAD), lambda i: (i, 0)),
                   pl.BlockSpec((None, 1, d), lambda i: (_seq_index(i * tm, seq_starts), 0, 0))])
    args = [x, *parts, gates, gate_mod]
    if final:
        in_specs.append(pl.BlockSpec((1, d), lambda i: (0, 0)))
        args.append(final_g)
    return pl.pallas_call(
        functools.partial(_combine_kernel, final=final, bounds=bounds),
        grid=(t // tm,),
        in_specs=in_specs,
        out_specs=out_specs,
        out_shape=out_shape,
        compiler_params=_params("arbitrary"),
        name="moe_combine_residual",
    )(*args)


def _ret_kernel(*refs, reverse, nchunks, seq_starts, seq_ends):
    rc = RET_CHUNK
    if reverse:
        (cd_ref, q_ref, kt_ref, v_ref, inner_ref, cross_ref, kvd_ref, of_ref, gate_ref, o_ref, st_scr) = refs
    else:
        (cd_ref, q_ref, kt_ref, v_ref, inner_ref, cross_ref, kvd_ref, o_ref, st_scr) = refs
    step = pl.program_id(0)
    if reverse:
        fresh = _is_any((nchunks - step) * rc, seq_ends)
    else:
        fresh = _is_any(step * rc, seq_starts)

    @pl.when(fresh)
    def _():
        st_scr[...] = jnp.zeros_like(st_scr)

    dk, dv = RET_QK_DIM, RET_V_DIM
    for h in range(RET_HEADS):
        q = q_ref[:, h * dk:(h + 1) * dk]
        kt = kt_ref[h * dk:(h + 1) * dk, :]
        v = v_ref[:, h * dv:(h + 1) * dv]
        s = jnp.dot(q, kt, preferred_element_type=F32) * inner_ref[h]
        st = st_scr[h]
        o = jnp.dot(s.astype(BF16), v, preferred_element_type=F32)
        o = o + jnp.dot(q, st.astype(BF16), preferred_element_type=F32) * cross_ref[h]
        ktd = (kt.astype(F32) * kvd_ref[h]).astype(BF16)
        st_scr[h] = st * cd_ref[h] + jnp.dot(ktd, v, preferred_element_type=F32)
        if reverse:
            o = o + of_ref[:, h * dv:(h + 1) * dv].astype(F32)
            ms = jnp.mean(o * o, axis=-1, keepdims=True)
            o = o * lax.rsqrt(ms + EPS) * _silu(gate_ref[:, h * dv:(h + 1) * dv].astype(F32))
        o_ref[:, h * dv:(h + 1) * dv] = o.astype(o_ref.dtype)


def _retention_direction(krt, proj, vcol0, gcol0, log_decay, seq_starts, seq_ends, reverse, o_fwd=None):
    t = proj.shape[0]
    rc = _tile(min(s2 - s1 for s1, s2 in zip(seq_starts, seq_ends)), RET_CHUNK)
    assert rc == RET_CHUNK
    nchunks = t // rc
    pos = jnp.arange(rc, dtype=F32)
    diff = pos[:, None] - pos[None, :]
    lg = log_decay.astype(F32)
    if reverse:
        mask = diff < 0
        dist = -diff
        cross = jnp.exp((rc - pos)[None, :, None] * lg[:, None, None])
        kvd = jnp.exp(pos[None, None, :] * lg[:, None, None])
    else:
        mask = diff >= 0
        dist = diff
        cross = jnp.exp((pos + 1.0)[None, :, None] * lg[:, None, None])
        kvd = jnp.exp((rc - 1.0 - pos)[None, None, :] * lg[:, None, None])
    inner = jnp.where(mask[None], jnp.exp(jnp.where(mask, dist, 0.0)[None] * lg[:, None, None]), 0.0)
    cross = jnp.broadcast_to(cross, (RET_HEADS, rc, RET_V_DIM))
    kvd = jnp.broadcast_to(kvd, (RET_HEADS, RET_QK_DIM, rc))
    chunk_decay = jnp.exp(rc * lg)

    def rb(i):
        return nchunks - 1 - i if reverse else i

    vb = vcol0 // RET_V_WIDTH
    assert vcol0 % RET_V_WIDTH == 0 and gcol0 % RET_V_WIDTH == 0
    in_specs = [pl.BlockSpec(memory_space=pltpu.SMEM),
                pl.BlockSpec((rc, RET_QK_WIDTH), lambda i: (rb(i), 0)),
                pl.BlockSpec((RET_QK_WIDTH, rc), lambda i: (0, rb(i))),
                pl.BlockSpec((rc, RET_V_WIDTH), lambda i: (rb(i), vb)),
                pl.BlockSpec((RET_HEADS, rc, rc), lambda i: (0, 0, 0)),
                pl.BlockSpec((RET_HEADS, rc, RET_V_DIM), lambda i: (0, 0, 0)),
                pl.BlockSpec((RET_HEADS, RET_QK_DIM, rc), lambda i: (0, 0, 0))]
    args = [chunk_decay, proj, krt, proj, inner, cross, kvd]
    if reverse:
        gb = gcol0 // RET_V_WIDTH
        in_specs += [pl.BlockSpec((rc, RET_V_WIDTH), lambda i: (rb(i), 0)),
                     pl.BlockSpec((rc, RET_V_WIDTH), lambda i: (rb(i), gb))]
        args += [o_fwd, proj]
    return pl.pallas_call(
        functools.partial(_ret_kernel, reverse=reverse, nchunks=nchunks,
                          seq_starts=seq_starts, seq_ends=seq_ends),
        grid=(nchunks,),
        in_specs=in_specs,
        out_specs=pl.BlockSpec((rc, RET_V_WIDTH), lambda i: (rb(i), 0)),
        out_shape=jax.ShapeDtypeStruct((t, RET_V_WIDTH), BF16),
        scratch_shapes=[pltpu.VMEM((RET_HEADS, RET_QK_DIM, RET_V_DIM), F32)],
        compiler_params=_params("arbitrary"),
        name="retention_bwd_norm_gate" if reverse else "retention_fwd",
    )(*args)


def _ssd_diff_layer(x, a1, sh1, p, i, layer, lay):
    seq_starts, seq_ends, groups = lay
    w_in = p['ab_w_in'][i]
    o2 = SSD_WIDTH + SSD_XBC
    o3 = o2 + 2 * SSD_HEADS
    w_main = jnp.concatenate([w_in[:, :o2], w_in[:, o3:]], axis=1).astype(BF16)
    w_dt = jnp.pad(w_in[:, o2:o3], ((0, 0), (0, DT_PAD - 2 * SSD_HEADS))).astype(BF16)
    qcol = o2
    kcol = qcol + DA_QK_WIDTH
    vcol = kcol + DA_QK_WIDTH
    lmax = max(g[2] for g in groups)
    tables = (*_rope_tables(lmax, DA_HEAD_DIM, DA_HEAD_DIM ** -0.5 * math.log2(math.e)),
              *_rope_tables(lmax, DA_HEAD_DIM, 1.0))
    tn = 1024
    assert qcol % tn == 0 and kcol % tn == 0 and vcol % tn == 0
    proj, dt = _norm_mod_matmul(x, a1, sh1, w_main, w_dt, seq_starts,
                                ((qcol // tn, kcol // tn), (kcol // tn, vcol // tn)), DA_HEAD_DIM, tables)

    xact = _conv_silu(proj, SSD_WIDTH, p['ssd_conv_w'][i], p['ssd_conv_b'][i], seq_starts, seq_ends)
    bt = xact[:, SSD_WIDTH:SSD_WIDTH + SSD_BC].T
    dtt = dt.T
    expand = (jnp.arange(SSD_WIDTH)[None, :] // SSD_HEAD_DIM == jnp.arange(SSD_HEADS)[:, None]).astype(F32)
    a_f = -jnp.exp(p['ssd_a_log_fwd'][i].astype(F32))
    a_b = -jnp.exp(p['ssd_a_log_bwd'][i].astype(F32))
    yf = _ssd_direction(xact, bt, dt, dtt, p['ssd_dt_bias_fwd'][i].astype(F32), a_f, expand,
                        seq_starts, seq_ends, reverse=False)
    dskip = jnp.repeat(p['ssd_d'][i].astype(F32), SSD_HEAD_DIM).reshape(1, SSD_WIDTH)
    y_ssd = _ssd_direction(xact, bt, dt, dtt, p['ssd_dt_bias_bwd'][i].astype(F32), a_b, expand,
                           seq_starts, seq_ends, reverse=True,
                           final=(proj, yf, dskip, p['ssd_norm_g'][i].astype(F32).reshape(1, SSD_WIDTH)))

    lam_init = 0.8 - 0.6 * math.exp(-0.3 * layer)
    lam = (jnp.exp(jnp.sum(p['da_lambda_q1'][i].astype(F32) * p['da_lambda_k1'][i].astype(F32)))
           - jnp.exp(jnp.sum(p['da_lambda_q2'][i].astype(F32) * p['da_lambda_k2'][i].astype(F32)))
           + lam_init).reshape(1)
    subln = p['da_subln_g'][i].astype(F32).reshape(1, 2 * DA_HEAD_DIM)
    o = _diff_attention(proj, qcol, kcol, vcol, lam, subln, 1.0 - lam_init, seq_starts, seq_ends)

    w_out = p['ab_w_out'][i].astype(BF16)
    return (y_ssd, 0, o, 0, w_out)


def _retention_layer(x, a1, sh1, p, j, lay):
    seq_starts, seq_ends, groups = lay
    w_in = p['ret_w_in'][j].astype(BF16)
    lmax = max(g[2] for g in groups)
    tables = (*_rope_tables(lmax, RET_QK_DIM, 1.0), *_rope_tables(lmax, RET_QK_DIM, RET_QK_DIM ** -0.5))
    tn = 1024
    kcol = RET_QK_WIDTH
    vcol = 2 * RET_QK_WIDTH
    gcol = vcol + RET_V_WIDTH
    proj = _norm_mod_matmul(x, a1, sh1, w_in, None, seq_starts,
                            ((0, kcol // tn), (kcol // tn, vcol // tn)), RET_QK_DIM, tables)
    krt = proj[:, kcol:vcol].T
    lg_f = -jnp.exp(p['ret_log_decay_fwd'][j].astype(F32))
    lg_b = -jnp.exp(p['ret_log_decay_bwd'][j].astype(F32))
    of = _retention_direction(krt, proj, vcol, gcol, lg_f, seq_starts, seq_ends, reverse=False)
    o = _retention_direction(krt, proj, vcol, gcol, lg_b, seq_starts, seq_ends, reverse=True, o_fwd=of)
    w_out = p['ret_w_out'][j].astype(BF16)
    return (o, 0, o, 1, w_out)


def _moe_layer(x_parts, mix, g1, a2, sh2, g2, p, layer, lay, final_g, out_rows):
    seq_starts, _, _ = lay
    t = sum(xp.shape[0] for xp in x_parts)
    d = x_parts[0].shape[1]
    wr = jnp.concatenate([p['moe_w_group'][layer], p['moe_w_expert'][layer]], axis=1).astype(F32)
    br = jnp.concatenate([p['moe_b_group'][layer], p['moe_b_expert'][layer]]).astype(F32)
    nr = wr.shape[1]
    wr = jnp.pad(wr, ((0, 0), (0, ROUTER_PAD - nr)))
    br = jnp.pad(br, (0, ROUTER_PAD - nr)).reshape(1, ROUTER_PAD)
    x = _outproj_residual(*mix, x_parts, g1, seq_starts)
    h, route, cnt = _norm_mod_router(x, a2, sh2, wr, br, seq_starts)

    blk = MOE_ROW_BLOCK
    n_assign = t * MOE_TOP_K
    e_flat = route[:, 0:MOE_TOP_K].astype(jnp.int32).reshape(n_assign)
    rank = route[:, MOE_TOP_K:2 * MOE_TOP_K].astype(jnp.int32).reshape(n_assign)
    counts = cnt[0, MOE_GROUPS:MOE_GROUPS + MOE_EXPERTS].astype(jnp.int32)
    padded = (counts + blk - 1) // blk * blk
    ends_pad = jnp.cumsum(padded)
    start_pad = ends_pad - padded
    dest = (start_pad[e_flat] + rank).astype(jnp.int32)
    n_rows = n_assign + MOE_EXPERTS * blk
    n_blocks = n_rows // blk
    tok_flat = jnp.repeat(jnp.arange(t, dtype=jnp.int32), MOE_TOP_K)
    row_tok = (jnp.arange(n_rows, dtype=jnp.int32) % t).at[dest].set(tok_flat)
    ends_blk = jnp.cumsum(padded // blk).astype(jnp.int32)
    n_used = ends_blk[-1]
    pos = jnp.arange(n_blocks, dtype=jnp.int32)
    block = jnp.minimum(pos, n_used - 1)
    block_expert = jnp.minimum(jnp.sum(ends_blk[None, :] <= block[:, None], axis=1),
                               MOE_EXPERTS - 1).astype(jnp.int32)
    block_new = jnp.concatenate([jnp.ones((1,), jnp.int32),
                                 (block_expert[1:] != block_expert[:-1]).astype(jnp.int32)])
    run_start = jnp.where(block_new == 1, pos, n_blocks)
    next_start = lax.cummin(jnp.concatenate([run_start[1:], jnp.full((1,), n_blocks, jnp.int32)]), reverse=True)
    has_next = (next_start < n_blocks).astype(jnp.int32)
    next_expert = block_expert[jnp.minimum(next_start, n_blocks - 1)]
    tables = (block, block_expert, block_new, next_expert, has_next, n_used.reshape(1))

    xs = jnp.take(h, row_tok, axis=0, mode='clip')
    out = _moe_ffn(xs, tables, p['moe_w1'], p['moe_w3'], p['moe_w2'], layer)
    dest2 = dest.reshape(t, MOE_TOP_K)
    parts = [jnp.take(out, dest2[:, k], axis=0, mode='clip') for k in range(MOE_TOP_K)]
    return tuple(_moe_combine(x, parts, route, g2, final_g, seq_starts, out_rows))


def kernel(x_prompt, x_sample, c_prompt, c_sample, ada_w, ada_b, norm_mix_g, norm_ffn_g, ab_w_in, ssd_conv_w, ssd_conv_b, ssd_a_log_fwd, ssd_a_log_bwd, ssd_dt_bias_fwd, ssd_dt_bias_bwd, ssd_d, ssd_norm_g, da_lambda_q1, da_lambda_k1, da_lambda_q2, da_lambda_k2, da_subln_g, ab_w_out, ret_w_in, ret_log_decay_fwd, ret_log_decay_bwd, ret_w_out, moe_w_group, moe_b_group, moe_w_expert, moe_b_expert, moe_w1, moe_w3, moe_w2, final_norm_g):
    p = dict(ab_w_in=ab_w_in, ssd_conv_w=ssd_conv_w, ssd_conv_b=ssd_conv_b,
             ssd_a_log_fwd=ssd_a_log_fwd, ssd_a_log_bwd=ssd_a_log_bwd,
             ssd_dt_bias_fwd=ssd_dt_bias_fwd, ssd_dt_bias_bwd=ssd_dt_bias_bwd, ssd_d=ssd_d,
             ssd_norm_g=ssd_norm_g, da_lambda_q1=da_lambda_q1, da_lambda_k1=da_lambda_k1,
             da_lambda_q2=da_lambda_q2, da_lambda_k2=da_lambda_k2, da_subln_g=da_subln_g,
             ab_w_out=ab_w_out, ret_w_in=ret_w_in, ret_log_decay_fwd=ret_log_decay_fwd,
             ret_log_decay_bwd=ret_log_decay_bwd, ret_w_out=ret_w_out, moe_w_group=moe_w_group,
             moe_b_group=moe_b_group, moe_w_expert=moe_w_expert, moe_b_expert=moe_b_expert,
             moe_w1=moe_w1, moe_w3=moe_w3, moe_w2=moe_w2)
    bp, lp, d = x_prompt.shape
    bs, ls, _ = x_sample.shape
    depth = ada_w.shape[0]
    groups = ((0, bp, lp), (bp * lp, bs, ls))
    seq_starts = tuple(r0 + b * l for (r0, nb, l) in groups for b in range(nb))
    seq_ends = tuple(r0 + (b + 1) * l for (r0, nb, l) in groups for b in range(nb))
    lay = (seq_starts, seq_ends, groups)
    nseq = len(seq_starts)
    group_rows = (bp * lp, bs * ls)
    x = (x_prompt.reshape(bp * lp, d), x_sample.reshape(bs * ls, d))
    c = jnp.concatenate([c_prompt, c_sample], axis=0).astype(F32)
    c_pad = jnp.pad(c, ((0, -nseq % 8), (0, 0)))
    mod = _ada_modulation(c_pad, ada_w.astype(F32), ada_b.astype(F32))[:, :nseq]

    for layer in range(depth):
        sh1, sc1, g1, sh2, sc2, g2 = [m.reshape(nseq, 1, d) for m in jnp.split(mod[layer], 6, axis=-1)]
        a1 = norm_mix_g[layer].astype(F32)[None, None, :] * (1.0 + sc1)
        a2 = norm_ffn_g[layer].astype(F32)[None, None, :] * (1.0 + sc2)
        if layer % 2 == 0:
            mix = _ssd_diff_layer(x, a1, sh1, p, layer // 2, layer, lay)
        else:
            mix = _retention_layer(x, a1, sh1, p, layer // 2, lay)
        last = layer == depth - 1
        final_g = final_norm_g.astype(F32).reshape(1, d) if last else None
        x = _moe_layer(x, mix, g1, a2, sh2, g2, p, layer, lay, final_g,
                       group_rows if last else (sum(group_rows),))
    return (x[0].reshape(bp, lp, d), x[1].reshape(bs, ls, d))
```

```python
import functools
import math

import jax
import jax.numpy as jnp
from jax import lax
from jax.experimental import pallas as pl
from jax.experimental.pallas import tpu as pltpu

F32 = jnp.float32
BF16 = jnp.bfloat16
HIGHEST = lax.Precision.HIGHEST

EPS = 1e-6
SUBLN_EPS = 1e-5
ROPE_THETA = 10000.0
ROPE_FINE = 64

SSD_HEADS = 32
SSD_HEAD_DIM = 64
SSD_WIDTH = SSD_HEADS * SSD_HEAD_DIM
SSD_GROUPS = 4
SSD_STATE = 128
SSD_CONV = 5
SSD_BC = SSD_GROUPS * SSD_STATE
SSD_XBC = SSD_WIDTH + 2 * SSD_BC
SSD_CHUNK = 128
DA_HEADS = 8
DA_HEAD_DIM = 128
DA_QK_WIDTH = 2 * DA_HEADS * DA_HEAD_DIM
DA_V_WIDTH = DA_HEADS * 2 * DA_HEAD_DIM
RET_HEADS = 8
RET_QK_DIM = 256
RET_V_DIM = 512
RET_QK_WIDTH = RET_HEADS * RET_QK_DIM
RET_V_WIDTH = RET_HEADS * RET_V_DIM
RET_CHUNK = 256
MOE_GROUPS = 4
MOE_EXPERTS_PER_GROUP = 8
MOE_EXPERTS = MOE_GROUPS * MOE_EXPERTS_PER_GROUP
MOE_TOP_K = 2
MOE_ROW_BLOCK = 512
MOE_FF_CHUNKS = 2
MOE_ROW_RANGES = 1
NMM_COL_CHUNK = 256
ROUTER_PAD = 128
DT_PAD = 128
HALO = 16

VMEM_LIMIT_BYTES = 56 * 1024 * 1024
MOE_VMEM_LIMIT_BYTES = 62 * 1024 * 1024


def _params(*semantics, vmem_limit_bytes=VMEM_LIMIT_BYTES):
    return pltpu.CompilerParams(dimension_semantics=semantics, vmem_limit_bytes=vmem_limit_bytes)


def _silu(x):
    return x * jax.nn.sigmoid(x)


def _softplus(x):
    return jnp.maximum(x, 0.0) + jnp.log1p(jnp.exp(-jnp.abs(x)))


def _lane_repeat(x, n):
    return x if n == 1 else jnp.concatenate([x] * n, axis=1)


def _tile(n, pref):
    t = min(n, pref)
    assert n % t == 0, (n, pref)
    return t


def _row_tile(t, pref, seq_starts):
    tm = min(t, pref)
    while t % tm or any(st % tm for st in seq_starts):
        tm //= 2
    return tm


def _seq_index(row, seq_starts):
    s = 0
    for st in seq_starts[1:]:
        s = s + jnp.where(row >= st, 1, 0)
    return s


def _is_any(row, values):
    hit = row == values[0]
    for v in values[1:]:
        hit = jnp.logical_or(hit, row == v)
    return hit


def _ada_kernel(c_ref, w_ref, b_ref, o_ref):
    o_ref[...] = jnp.dot(_silu(c_ref[...]), w_ref[...], preferred_element_type=F32,
                         precision=HIGHEST) + b_ref[...]


def _ada_modulation(c_pad, ada_w, ada_b):
    depth, d, n = ada_w.shape
    rows = c_pad.shape[0]
    tn = _tile(n, 1024)
    return pl.pallas_call(
        _ada_kernel,
        grid=(depth, n // tn),
        in_specs=[pl.BlockSpec((rows, d), lambda l, j: (0, 0)),
                  pl.BlockSpec((None, d, tn), lambda l, j: (l, 0, j)),
                  pl.BlockSpec((None, 1, tn), lambda l, j: (l, 0, j))],
        out_specs=pl.BlockSpec((None, rows, tn), lambda l, j: (l, 0, j)),
        out_shape=jax.ShapeDtypeStruct((depth, rows, n), F32),
        compiler_params=_params("arbitrary", "arbitrary"),
        name="ada_modulation",
    )(c_pad, ada_w, ada_b.reshape(depth, 1, n))


def _norm_mod(x, a, sh):
    ms = jnp.mean(x * x, axis=-1, keepdims=True)
    return (x * lax.rsqrt(ms + EPS)) * a + sh


def _rotate_half_tile(x, cos, sin, head_dim):
    half = head_dim // 2
    outs = []
    for h in range(x.shape[1] // head_dim):
        xh = x[:, h * head_dim:(h + 1) * head_dim]
        if half % 128 == 0:
            rot = jnp.concatenate([xh[:, half:], xh[:, :half]], axis=-1)
        else:
            rot = pltpu.roll(xh, half, 1)
        outs.append(xh * cos + rot * sin)
    return jnp.concatenate(outs, axis=-1)


def _row_part_specs(parts, tm, width, row_of, col_of, single_buffer=False):
    specs, bounds = [], [0]
    mode = dict(pipeline_mode=pl.Buffered(1)) if single_buffer else {}
    for arr in parts:
        b0, nb = bounds[-1], arr.shape[0] // tm
        assert arr.shape[0] % tm == 0
        specs.append(pl.BlockSpec(
            (tm, width), lambda *g, b0=b0, nb=nb: (jnp.clip(row_of(*g) - b0, 0, nb - 1), col_of(*g)), **mode))
        bounds.append(b0 + nb)
    return specs, tuple(bounds)


def _for_row_part(i, bounds, fn):
    if len(bounds) == 2:
        fn(0)
        return
    for k in range(len(bounds) - 1):
        pl.when(jnp.logical_and(i >= bounds[k], i < bounds[k + 1]))(functools.partial(fn, k))


def _nmm_kernel(*refs, with_dt, rope_tiles, head_dim, bounds):
    n_parts = len(bounds) - 1
    x_refs = refs[:n_parts]
    a_ref, sh_ref, w_ref, cq_ref, sq_ref, ck_ref, sk_ref = refs[n_parts:n_parts + 7]
    if with_dt:
        wdt_ref, o_ref, odt_ref, h_scr = refs[n_parts + 7:]
    else:
        o_ref, h_scr = refs[n_parts + 7:]
    j = pl.program_id(1)

    def normalize(k):
        hb = _norm_mod(x_refs[k][...], a_ref[...], sh_ref[...]).astype(BF16)
        h_scr[...] = hb
        if with_dt:
            odt_ref[...] = jnp.dot(hb, wdt_ref[...], preferred_element_type=F32)

    @pl.when(j == 0)
    def _():
        _for_row_part(pl.program_id(0), bounds, normalize)

    (q_lo, q_hi), (k_lo, k_hi) = rope_tiles
    is_q = jnp.logical_and(j >= q_lo, j < q_hi)
    is_k = jnp.logical_and(j >= k_lo, j < k_hi)
    rotary = jnp.logical_or(is_q, is_k)
    cos = jnp.where(rotary, jnp.where(is_q, cq_ref[...], ck_ref[...]), 1.0)
    sin = jnp.where(rotary, jnp.where(is_q, sq_ref[...], sk_ref[...]), 0.0)
    cw = max(head_dim, NMM_COL_CHUNK)
    for c in range(w_ref.shape[1] // cw):
        acc = jnp.dot(h_scr[...], w_ref[:, c * cw:(c + 1) * cw], preferred_element_type=F32)
        o_ref[:, c * cw:(c + 1) * cw] = _rotate_half_tile(acc, cos, sin, head_dim).astype(o_ref.dtype)


def _norm_mod_matmul(x_parts, a, sh, w, wdt, seq_starts, rope_tiles, head_dim, tables):
    t = sum(xp.shape[0] for xp in x_parts)
    d = x_parts[0].shape[1]
    n = w.shape[1]
    tm = _row_tile(t, 1024, seq_starts)
    tn = _tile(n, 1024)
    with_dt = wdt is not None

    def seq_map(i, j):
        return (_seq_index(i * tm, seq_starts), 0, 0)

    def pos_map(i, j):
        r = i * tm
        s = _seq_index(r, seq_starts)
        st = 0
        for k, v in enumerate(seq_starts):
            st = st + jnp.where(s == k, v, 0)
        return ((r - st) // tm, 0)

    x_specs, bounds = _row_part_specs(x_parts, tm, d, lambda i, j: i, lambda i, j: 0,
                                      single_buffer=len(x_parts) > 1)
    in_specs = x_specs + [pl.BlockSpec((None, 1, d), seq_map),
                          pl.BlockSpec((None, 1, d), seq_map),
                          pl.BlockSpec((d, tn), lambda i, j: (0, j))] + [pl.BlockSpec((tm, head_dim), pos_map)] * 4
    out_specs = [pl.BlockSpec((tm, tn), lambda i, j: (i, j))]
    out_shape = [jax.ShapeDtypeStruct((t, n), BF16)]
    args = [*x_parts, a, sh, w, *tables]
    if with_dt:
        in_specs.append(pl.BlockSpec((d, DT_PAD), lambda i, j: (0, 0)))
        out_specs.append(pl.BlockSpec((tm, DT_PAD), lambda i, j: (i, 0)))
        out_shape.append(jax.ShapeDtypeStruct((t, DT_PAD), F32))
        args.append(wdt)
    res = pl.pallas_call(
        functools.partial(_nmm_kernel, with_dt=with_dt, rope_tiles=rope_tiles, head_dim=head_dim, bounds=bounds),
        grid=(t // tm, n // tn),
        in_specs=in_specs, out_specs=out_specs, out_shape=out_shape,
        scratch_shapes=[pltpu.VMEM((tm, d), BF16)],
        compiler_params=_params("arbitrary", "arbitrary"),
        name="norm_mod_matmul",
    )(*args)
    return res if with_dt else res[0]


def _conv_kernel(cur_ref, prev_ref, next_ref, w_ref, b_ref, o_ref, ext_scr, *, tm, seq_starts, seq_ends):
    r0 = pl.program_id(0) * tm
    at_start = _is_any(r0, seq_starts)
    at_end = _is_any(r0 + tm, seq_ends)
    prev = prev_ref[...].astype(F32)[HALO - 8:HALO]
    nxt = next_ref[...].astype(F32)[0:8]
    ext_scr[0:8, :] = jnp.where(at_start, 0.0, prev)
    ext_scr[8:8 + tm, :] = cur_ref[...].astype(F32)
    ext_scr[8 + tm:16 + tm, :] = jnp.where(at_end, 0.0, nxt)
    ext = ext_scr[...]
    n = tm + 16
    pad = SSD_CONV // 2
    below = w_ref[0:1, :] * ext
    for k in range(1, pad):
        below = w_ref[k:k + 1, :] * ext + pltpu.roll(below, 1, 0)
    above = w_ref[SSD_CONV - 1:SSD_CONV, :] * ext
    for k in range(SSD_CONV - 2, pad, -1):
        above = w_ref[k:k + 1, :] * ext + pltpu.roll(above, n - 1, 0)
    acc = w_ref[pad:pad + 1, :] * ext + pltpu.roll(below, 1, 0) + pltpu.roll(above, n - 1, 0)
    acc = acc[8:8 + tm, :] + b_ref[...]
    o_ref[...] = _silu(acc).astype(o_ref.dtype)


def _conv_silu(proj, col0, conv_w, conv_b, seq_starts, seq_ends):
    t = proj.shape[0]
    tm = _row_tile(t, 512, seq_starts)
    tc = 1024
    assert col0 % tc == 0 and SSD_XBC % tc == 0 and tm % HALO == 0
    cb = col0 // tc
    hb = tm // HALO
    last_halo = t // HALO - 1
    return pl.pallas_call(
        functools.partial(_conv_kernel, tm=tm, seq_starts=seq_starts, seq_ends=seq_ends),
        grid=(t // tm, SSD_XBC // tc),
        in_specs=[pl.BlockSpec((tm, tc), lambda i, j: (i, cb + j)),
                  pl.BlockSpec((HALO, tc), lambda i, j: (jnp.maximum(i * hb - 1, 0), cb + j)),
                  pl.BlockSpec((HALO, tc), lambda i, j: (jnp.minimum((i + 1) * hb, last_halo), cb + j)),
                  pl.BlockSpec((SSD_CONV, tc), lambda i, j: (0, j)),
                  pl.BlockSpec((1, tc), lambda i, j: (0, j))],
        out_specs=pl.BlockSpec((tm, tc), lambda i, j: (i, j)),
        out_shape=jax.ShapeDtypeStruct((t, SSD_XBC), BF16),
        scratch_shapes=[pltpu.VMEM((tm + 16, tc), F32)],
        compiler_params=_params("arbitrary", "arbitrary"),
        name="ssd_conv_silu",
    )(proj, proj, proj, conv_w, conv_b.reshape(1, SSD_XBC))


def _ssd_kernel(*refs, reverse, dcol, nchunks, seq_starts, seq_ends):
    ch = SSD_CHUNK
    if reverse:
        (xact_ref, bt_ref, dt_ref, dtt_ref, bias_ref, a_ref, biast_ref, at_ref, e_ref,
         z_ref, yf_ref, dskip_ref, g_ref, o_ref, h_scr, y_scr) = refs
    else:
        (xact_ref, bt_ref, dt_ref, dtt_ref, bias_ref, a_ref, biast_ref, at_ref, e_ref,
         o_ref, h_scr) = refs
    step = pl.program_id(0)
    if reverse:
        row_end = (nchunks - step) * ch
        fresh = _is_any(row_end, seq_ends)
    else:
        fresh = _is_any(step * ch, seq_starts)

    @pl.when(fresh)
    def _():
        h_scr[...] = jnp.zeros_like(h_scr)

    dt = _softplus(dt_ref[:, dcol:dcol + SSD_HEADS] + bias_ref[...])
    dta = dt * a_ref[...]
    dtt = _softplus(dtt_ref[dcol:dcol + SSD_HEADS, :] + biast_ref[...])
    dtat = dtt * at_ref[...]
    row = lax.broadcasted_iota(jnp.int32, (ch, ch), 0)
    col = lax.broadcasted_iota(jnp.int32, (ch, ch), 1)
    if reverse:
        keep = col >= row
    else:
        keep = col <= row
    tri = jnp.where(keep, 1.0, 0.0).astype(F32)
    trit = jnp.where(keep, 0.0, 1.0).astype(F32) + jnp.where(row == col, 1.0, 0.0).astype(F32)
    cs = jnp.dot(tri, dta, preferred_element_type=F32, precision=HIGHEST)
    cst = jnp.dot(dtat, trit, preferred_element_type=F32, precision=HIGHEST)
    last = 0 if reverse else ch - 1
    ecs = jnp.exp(cs)
    wt = dtt * jnp.exp(cst[:, last:last + 1] - cst)
    cdec_x = jnp.dot(jnp.broadcast_to(ecs[last:last + 1, :], (8, SSD_HEADS)), e_ref[...],
                     preferred_element_type=F32, precision=HIGHEST)[0:1, :]

    lane = lax.broadcasted_iota(jnp.int32, (1, 2 * SSD_HEAD_DIM), 1)
    first = lane < SSD_HEAD_DIM
    gw = SSD_WIDTH // SSD_GROUPS
    hpg = SSD_HEADS // SSD_GROUPS
    pw = 2 * SSD_HEAD_DIM
    ssq = jnp.zeros((ch, pw), F32)
    for g in range(SSD_GROUPS):
        bgt = bt_ref[g * SSD_STATE:(g + 1) * SSD_STATE, :]
        c0 = SSD_WIDTH + SSD_BC + g * SSD_STATE
        cg = xact_ref[:, c0:c0 + SSD_STATE]
        scores = jnp.dot(cg, bgt, preferred_element_type=F32)
        cgf = cg.astype(F32)
        bgtf = bgt.astype(F32)
        for pr in range(hpg // 2):
            lo = g * gw + pr * pw
            hi = lo + pw
            xpair = xact_ref[:, lo:hi].astype(F32)
            hpair = h_scr[g, :, pr * pw:(pr + 1) * pw]
            xsel = (jnp.where(first, xpair, 0.0).astype(BF16), jnp.where(first, 0.0, xpair).astype(BF16))
            hsel = (jnp.where(first, hpair, 0.0).astype(BF16), jnp.where(first, 0.0, hpair).astype(BF16))
            y = None
            bws = []
            for u in range(2):
                h = g * hpg + 2 * pr + u
                csb = jnp.broadcast_to(cs[:, h:h + 1], (ch, ch))
                seg = jnp.where(keep, csb - cst[h:h + 1, :], -jnp.inf)
                m = (scores * jnp.exp(seg) * dtt[h:h + 1, :]).astype(BF16)
                ce = (cgf * jnp.exp(csb)).astype(BF16)
                part = jnp.dot(jnp.concatenate([m, ce], axis=1), jnp.concatenate([xsel[u], hsel[u]], axis=0),
                               preferred_element_type=F32)
                y = part if y is None else y + part
                bws.append((bgtf * wt[h:h + 1, :]).astype(BF16))
            h_scr[g, :, pr * pw:(pr + 1) * pw] = hpair * cdec_x[:, lo:hi] + jnp.dot(
                jnp.concatenate(bws, axis=1), jnp.concatenate(xsel, axis=0), preferred_element_type=F32)
            if reverse:
                y = y + yf_ref[:, lo:hi].astype(F32) + dskip_ref[:, lo:hi] * xpair
                y = y * _silu(z_ref[:, lo:hi].astype(F32))
                ssq = ssq + y * y
                y_scr[:, lo:hi] = y
            else:
                o_ref[:, lo:hi] = y.astype(o_ref.dtype)
    if reverse:
        inv = lax.rsqrt(jnp.sum(ssq, axis=-1, keepdims=True) * (1.0 / SSD_WIDTH) + EPS)
        o_ref[...] = (y_scr[...] * inv * g_ref[...]).astype(o_ref.dtype)


def _ssd_direction(xact, bt, dt, dtt, bias, a, expand, seq_starts, seq_ends, reverse, final=None):
    t = xact.shape[0]
    ch = SSD_CHUNK
    nchunks = t // ch
    dcol = SSD_HEADS if reverse else 0

    def rb(i):
        return nchunks - 1 - i if reverse else i

    in_specs = [pl.BlockSpec((ch, SSD_XBC), lambda i: (rb(i), 0)),
                pl.BlockSpec((SSD_BC, ch), lambda i: (0, rb(i))),
                pl.BlockSpec((ch, DT_PAD), lambda i: (rb(i), 0)),
                pl.BlockSpec((DT_PAD, ch), lambda i: (0, rb(i))),
                pl.BlockSpec((1, SSD_HEADS), lambda i: (0, 0)),
                pl.BlockSpec((1, SSD_HEADS), lambda i: (0, 0)),
                pl.BlockSpec((SSD_HEADS, 1), lambda i: (0, 0)),
                pl.BlockSpec((SSD_HEADS, 1), lambda i: (0, 0)),
                pl.BlockSpec((SSD_HEADS, SSD_WIDTH), lambda i: (0, 0))]
    args = [xact, bt, dt, dtt, bias.reshape(1, -1), a.reshape(1, -1), bias.reshape(-1, 1), a.reshape(-1, 1), expand]
    scratch = [pltpu.VMEM((SSD_GROUPS, SSD_STATE, SSD_WIDTH // SSD_GROUPS), F32)]
    if reverse:
        proj, yf, dskip, g = final
        in_specs += [pl.BlockSpec((ch, SSD_WIDTH), lambda i: (rb(i), 0)),
                     pl.BlockSpec((ch, SSD_WIDTH), lambda i: (rb(i), 0)),
                     pl.BlockSpec((1, SSD_WIDTH), lambda i: (0, 0)),
                     pl.BlockSpec((1, SSD_WIDTH), lambda i: (0, 0))]
        args += [proj, yf, dskip, g]
        scratch.append(pltpu.VMEM((ch, SSD_WIDTH), F32))
    return pl.pallas_call(
        functools.partial(_ssd_kernel, reverse=reverse, dcol=dcol, nchunks=nchunks,
                          seq_starts=seq_starts, seq_ends=seq_ends),
        grid=(nchunks,),
        in_specs=in_specs,
        out_specs=pl.BlockSpec((ch, SSD_WIDTH), lambda i: (rb(i), 0)),
        out_shape=jax.ShapeDtypeStruct((t, SSD_WIDTH), BF16),
        scratch_shapes=scratch,
        compiler_params=_params("arbitrary"),
        name="ssd_bwd_gate_norm" if reverse else "ssd_fwd",
    )(*args)


def _rope_tables(lmax, head_dim, scale):
    half = head_dim // 2
    inv = 1.0 / (ROPE_THETA ** (jnp.arange(half, dtype=F32) / half))
    fine = min(lmax, ROPE_FINE)
    assert lmax % fine == 0
    a = (jnp.arange(lmax // fine, dtype=F32) * fine)[:, None] * inv[None, :]
    b = jnp.arange(fine, dtype=F32)[:, None] * inv[None, :]
    ca, sa, cb, sb = jnp.cos(a)[:, None, :], jnp.sin(a)[:, None, :], jnp.cos(b)[None], jnp.sin(b)[None]
    cos = (ca * cb - sa * sb).reshape(lmax, half) * scale
    sin = (sa * cb + ca * sb).reshape(lmax, half) * scale
    return jnp.concatenate([cos, cos], axis=-1), jnp.concatenate([-sin, sin], axis=-1)


def _seq_lookup(row, seq_starts, values):
    s = _seq_index(row, seq_starts)
    out = 0
    for k, v in enumerate(values):
        out = out + jnp.where(s == k, v, 0)
    return out


def _attn_kernel(lam_ref, q_ref, k_ref, v_ref, g_ref, o_ref, m_scr, l_scr, acc_scr, s_scr, p_scr, alpha_scr, *,
                 tk, nkv, unroll, strip, out_scale, seq_starts, seq_kv_blocks, seq_sub_block):
    dh = DA_HEAD_DIM
    tq = q_ref.shape[0]
    lanes = m_scr.shape[-1]
    row0 = pl.program_id(1) * tq
    n_valid = _seq_lookup(row0, seq_starts, seq_kv_blocks)
    sub0 = _seq_lookup(row0, seq_starts, seq_sub_block)
    m_scr[...] = jnp.full_like(m_scr, -jnp.inf)
    l_scr[...] = jnp.zeros_like(l_scr)
    acc_scr[...] = jnp.zeros_like(acc_scr)

    def chunk(c):
        sub = sub0 + c // nkv
        off = pl.multiple_of((c % nkv) * tk, tk)
        for u in range(2):
            s_scr[u] = lax.dot_general(q_ref[:, u * dh:(u + 1) * dh],
                                       k_ref[sub, pl.ds(off, tk), u * dh:(u + 1) * dh],
                                       (((1,), (1,)), ((), ())), preferred_element_type=F32)
        for u in range(2):
            for r in range(tq // strip):
                rows = slice(r * strip, (r + 1) * strip)
                s = s_scr[u, rows, :]
                m_prev = m_scr[u, rows, :]
                m_new = jnp.maximum(m_prev, jnp.max(s, axis=-1, keepdims=True))
                alpha = jnp.exp2(m_prev - m_new)
                p = jnp.exp2(s - _lane_repeat(m_new, tk // lanes))
                l_scr[u, rows, :] = alpha * l_scr[u, rows, :] + jnp.sum(p, axis=-1, keepdims=True)
                m_scr[u, rows, :] = m_new
                alpha_scr[u, rows, :] = alpha
                p_scr[u, rows, :] = p.astype(BF16)
            pv = jnp.dot(p_scr[u], v_ref[sub, pl.ds(off, tk), :], preferred_element_type=F32)
            acc_scr[u] = acc_scr[u] * _lane_repeat(alpha_scr[u], acc_scr.shape[-1] // lanes) + pv

    def body(j, carry):
        for w in range(unroll):
            chunk(j * unroll + w)
        return carry

    lax.fori_loop(0, n_valid * (nkv // unroll), body, 0)
    lam = lam_ref[0]
    rep = acc_scr.shape[-1] // lanes
    o = (acc_scr[0] / _lane_repeat(l_scr[0], rep)
         - lam * (acc_scr[1] / _lane_repeat(l_scr[1], rep)))
    ms = jnp.mean(o * o, axis=-1, keepdims=True)
    o_ref[...] = (o * lax.rsqrt(ms + SUBLN_EPS) * g_ref[...] * out_scale).astype(o_ref.dtype)


def _diff_attention(proj, qcol0, kcol0, vcol0, lam, subln_g, out_scale, seq_starts, seq_ends):
    t = proj.shape[0]
    pw = 2 * DA_HEAD_DIM
    lens = tuple(e - s for s, e in zip(seq_starts, seq_ends))
    kvb = min(lens)
    group = max(lens) // kvb
    assert all(n % kvb == 0 for n in lens) and all(s % kvb == 0 for s in seq_starts) and (t // kvb) % group == 0
    tq = _tile(kvb, 1024)
    tk = _tile(kvb, 2048)
    nkv = kvb // tk
    strip = 16
    lanes = 128
    assert vcol0 % pw == 0 and qcol0 % pw == 0 and kcol0 % pw == 0
    qc0, kc0, vc0 = qcol0 // pw, kcol0 // pw, vcol0 // pw
    seq_kv_blocks = tuple(n // kvb for n in lens)
    seq_group = tuple((s // kvb) // group for s in seq_starts)
    seq_sub_block = tuple((s // kvb) % group for s in seq_starts)
    assert all(sb + nb <= group for sb, nb in zip(seq_sub_block, seq_kv_blocks))
    kv_view = proj.reshape(t // kvb, kvb, proj.shape[1])

    def kv_map(col0):
        return lambda h, i: (_seq_lookup(i * tq, seq_starts, seq_group), 0, col0 + h)

    return pl.pallas_call(
        functools.partial(_attn_kernel, tk=tk, nkv=nkv, unroll=2 if nkv % 2 == 0 else 1, strip=strip,
                          out_scale=out_scale, seq_starts=seq_starts, seq_kv_blocks=seq_kv_blocks,
                          seq_sub_block=seq_sub_block),
        grid=(DA_HEADS, t // tq),
        in_specs=[pl.BlockSpec(memory_space=pltpu.SMEM),
                  pl.BlockSpec((tq, pw), lambda h, i: (i, qc0 + h)),
                  pl.BlockSpec((group, kvb, pw), kv_map(kc0)),
                  pl.BlockSpec((group, kvb, pw), kv_map(vc0)),
                  pl.BlockSpec((1, pw), lambda h, i: (0, 0))],
        out_specs=pl.BlockSpec((tq, pw), lambda h, i: (i, h)),
        out_shape=jax.ShapeDtypeStruct((t, DA_V_WIDTH), BF16),
        scratch_shapes=[pltpu.VMEM((2, tq, lanes), F32), pltpu.VMEM((2, tq, lanes), F32),
                        pltpu.VMEM((2, tq, pw), F32), pltpu.VMEM((2, tq, tk), F32), pltpu.VMEM((2, tq, tk), BF16),
                        pltpu.VMEM((2, tq, lanes), F32)],
        compiler_params=_params("arbitrary", "arbitrary"),
        name="diff_attention",
    )(lam, proj, kv_view, kv_view, subln_g)


def _outproj_kernel(y1_ref, y2_ref, w1_ref, w2_ref, g_ref, *rest, bounds):
    x_refs, o_ref = rest[:-1], rest[-1]
    acc = jnp.dot(y1_ref[...], w1_ref[...], preferred_element_type=F32)
    acc = acc + jnp.dot(y2_ref[...], w2_ref[...], preferred_element_type=F32)
    upd = g_ref[...] * acc

    def residual(k):
        o_ref[...] = x_refs[k][...] + upd

    _for_row_part(pl.program_id(1), bounds, residual)


def _outproj_residual(y1, c1, y2, c2, w, x_parts, gate, seq_starts):
    t = sum(xp.shape[0] for xp in x_parts)
    d = x_parts[0].shape[1]
    kh = w.shape[0] // 2
    tm = _row_tile(t, 512, seq_starts)
    tn = _tile(d, 1024)
    x_specs, bounds = _row_part_specs(x_parts, tm, tn, lambda j, i: i, lambda j, i: j)
    return pl.pallas_call(
        functools.partial(_outproj_kernel, bounds=bounds),
        grid=(d // tn, t // tm),
        in_specs=[pl.BlockSpec((tm, kh), lambda j, i: (i, c1)),
                  pl.BlockSpec((tm, kh), lambda j, i: (i, c2)),
                  pl.BlockSpec((kh, tn), lambda j, i: (0, j)),
                  pl.BlockSpec((kh, tn), lambda j, i: (1, j)),
                  pl.BlockSpec((None, 1, tn), lambda j, i: (_seq_index(i * tm, seq_starts), 0, j))] + x_specs,
        out_specs=pl.BlockSpec((tm, tn), lambda j, i: (i, j)),
        out_shape=jax.ShapeDtypeStruct((t, d), F32),
        compiler_params=_params("arbitrary", "arbitrary"),
        name="outproj_residual",
    )(y1, y2, w, w, gate, *x_parts)


def _router_kernel(x_ref, a_ref, sh_ref, wr_ref, br_ref, h_ref, rt_ref, cnt_ref, carry_scr):
    @pl.when(pl.program_id(0) == 0)
    def _():
        carry_scr[...] = jnp.zeros_like(carry_scr)

    h = _norm_mod(x_ref[...], a_ref[...], sh_ref[...])
    h_ref[...] = h.astype(h_ref.dtype)
    lg = jnp.dot(h, wr_ref[...], preferred_element_type=F32, precision=HIGHEST) + br_ref[...]
    tm, lanes = lg.shape
    lane = lax.broadcasted_iota(jnp.int32, (tm, lanes), 1).astype(F32)
    none = float(lanes)

    def top1(mask):
        v = jnp.max(jnp.where(mask, lg, -jnp.inf), axis=-1, keepdims=True)
        idx = jnp.min(jnp.where(mask, jnp.where(lg == v, lane, none), none), axis=-1, keepdims=True)
        return v, idx

    is_group = lane < MOE_GROUPS
    cm, grp = top1(is_group)
    p_grp = 1.0 / jnp.sum(jnp.where(is_group, jnp.exp(lg - cm), 0.0), axis=-1, keepdims=True)
    lo = MOE_GROUPS + grp * MOE_EXPERTS_PER_GROUP
    in_grp = jnp.logical_and(lane >= lo, lane < lo + MOE_EXPERTS_PER_GROUP)
    v1, i1 = top1(in_grp)
    v2, i2 = top1(jnp.logical_and(in_grp, lane != i1))
    e12 = jnp.exp(v2 - v1)
    g1 = p_grp / (1.0 + e12)
    g2 = g1 * e12

    onehot = jnp.where(lane == i1, 1.0, 0.0) + jnp.where(lane == i2, 1.0, 0.0)
    row = lax.broadcasted_iota(jnp.int32, (tm, tm), 0)
    col = lax.broadcasted_iota(jnp.int32, (tm, tm), 1)
    before = jnp.where(col < row, 1.0, 0.0).astype(BF16)
    base = carry_scr[0:1, :] + jnp.dot(before, onehot.astype(BF16), preferred_element_type=F32)
    r1 = jnp.sum(jnp.where(lane == i1, base, 0.0), axis=-1, keepdims=True)
    r2 = jnp.sum(jnp.where(lane == i2, base, 0.0), axis=-1, keepdims=True)
    carry_scr[...] = carry_scr[...] + jnp.sum(onehot, axis=0, keepdims=True)
    cnt_ref[...] = carry_scr[...]

    fields = (i1 - MOE_GROUPS, i2 - MOE_GROUPS, r1, r2, g1, g2)
    rt = jnp.zeros((tm, lanes), F32)
    for k, f in enumerate(fields):
        rt = jnp.where(lane == k, f, rt)
    rt_ref[...] = rt


def _norm_mod_router(x, a, sh, wr, br, seq_starts):
    t, d = x.shape
    tm = _row_tile(t, 512, seq_starts)

    def seq_map(i):
        return (_seq_index(i * tm, seq_starts), 0, 0)

    return pl.pallas_call(
        _router_kernel,
        grid=(t // tm,),
        in_specs=[pl.BlockSpec((tm, d), lambda i: (i, 0)),
                  pl.BlockSpec((None, 1, d), seq_map),
                  pl.BlockSpec((None, 1, d), seq_map),
                  pl.BlockSpec((d, ROUTER_PAD), lambda i: (0, 0)),
                  pl.BlockSpec((1, ROUTER_PAD), lambda i: (0, 0))],
        out_specs=[pl.BlockSpec((tm, d), lambda i: (i, 0)),
                   pl.BlockSpec((tm, ROUTER_PAD), lambda i: (i, 0)),
                   pl.BlockSpec((8, ROUTER_PAD), lambda i: (0, 0))],
        out_shape=[jax.ShapeDtypeStruct((t, d), BF16), jax.ShapeDtypeStruct((t, ROUTER_PAD), F32),
                   jax.ShapeDtypeStruct((8, ROUTER_PAD), F32)],
        scratch_shapes=[pltpu.VMEM((8, ROUTER_PAD), F32)],
        compiler_params=_params("arbitrary"),
        name="norm_mod_router",
    )(x, a, sh, wr, br)


def _moe_ffn_kernel(ib_ref, ie_ref, inew_ref, nxe_ref, hasnx_ref, nu_ref,
                    xs_ref, w1_hbm, w3_hbm, w2_hbm, *rest, layer, ff_chunks, chained):
    if chained:
        rest = rest[1:]
    o_ref, w1_stage, w3_stage, w2_stage, w1_scr, w3_scr, w2_scr, sems = rest
    del ib_ref
    s = pl.program_id(0)

    @pl.when(s >= nu_ref[0])
    def _():
        o_ref[...] = jnp.zeros_like(o_ref)

    def weight_copies(e):
        return (pltpu.make_async_copy(w1_hbm.at[layer, e], w1_stage, sems.at[0]),
                pltpu.make_async_copy(w3_hbm.at[layer, e], w3_stage, sems.at[1]),
                pltpu.make_async_copy(w2_hbm.at[layer, e], w2_stage, sems.at[2]))

    @pl.when(jnp.logical_and(s == 0, nu_ref[0] > 0))
    def _():
        for c in weight_copies(ie_ref[0]):
            c.start()

    @pl.when(s < nu_ref[0])
    def _():
        @pl.when(inew_ref[s] == 1)
        def _():
            for c in weight_copies(ie_ref[s]):
                c.wait()
            w1_scr[...] = w1_stage[...].astype(BF16)
            w3_scr[...] = w3_stage[...].astype(BF16)
            w2_scr[...] = w2_stage[...].astype(BF16)

            @pl.when(hasnx_ref[s] == 1)
            def _():
                for c in weight_copies(nxe_ref[s]):
                    c.start()

        x = xs_ref[...]
        fc = w1_scr.shape[1] // ff_chunks
        acc = None
        for k in range(ff_chunks):
            h1 = jnp.dot(x, w1_scr[:, k * fc:(k + 1) * fc], preferred_element_type=F32)
            h3 = jnp.dot(x, w3_scr[:, k * fc:(k + 1) * fc], preferred_element_type=F32)
            hid = (_silu(h1) * h3).astype(BF16)
            part = jnp.dot(hid, w2_scr[k * fc:(k + 1) * fc, :], preferred_element_type=F32)
            acc = part if acc is None else acc + part
        o_ref[...] = acc.astype(o_ref.dtype)


def _moe_ffn(xs, tables, w1, w3, w2, layer, n_rows, block0, prior):
    rows, d = xs.shape
    ff = w1.shape[3]
    blk = MOE_ROW_BLOCK
    n_tab = len(tables)
    chained = prior is not None
    in_specs = [pl.BlockSpec((blk, d), lambda s, ib, *_: (ib[s], 0)),
                pl.BlockSpec(memory_space=pl.ANY),
                pl.BlockSpec(memory_space=pl.ANY),
                pl.BlockSpec(memory_space=pl.ANY)]
    args = [*tables, xs, w1, w3, w2]
    if chained:
        in_specs.append(pl.BlockSpec(memory_space=pl.ANY))
        args.append(prior)

    grid_spec = pltpu.PrefetchScalarGridSpec(
        num_scalar_prefetch=n_tab,
        grid=(rows // blk,),
        in_specs=in_specs,
        out_specs=pl.BlockSpec((blk, d), lambda s, *_: (block0 + s, 0)),
        scratch_shapes=[pltpu.VMEM((d, ff), F32), pltpu.VMEM((d, ff), F32), pltpu.VMEM((ff, d), F32),
                        pltpu.VMEM((d, ff), BF16), pltpu.VMEM((d, ff), BF16), pltpu.VMEM((ff, d), BF16),
                        pltpu.SemaphoreType.DMA((3,))])
    return pl.pallas_call(
        functools.partial(_moe_ffn_kernel, layer=layer, ff_chunks=MOE_FF_CHUNKS, chained=chained),
        grid_spec=grid_spec,
        out_shape=jax.ShapeDtypeStruct((n_rows, d), BF16),
        input_output_aliases={len(args) - 1: 0} if chained else {},
        compiler_params=_params("arbitrary", vmem_limit_bytes=MOE_VMEM_LIMIT_BYTES),
        name="moe_ffn",
    )(*args)


def _combine_kernel(x_ref, o0_ref, o1_ref, gt_ref, g_ref, *rest, final, bounds):
    if final:
        fg_ref, o_refs = rest[0], rest[1:]
    else:
        o_refs = rest
    gt = gt_ref[...]
    g0 = 2 * MOE_TOP_K
    y = gt[:, g0:g0 + 1] * o0_ref[...].astype(F32) + gt[:, g0 + 1:g0 + 2] * o1_ref[...].astype(F32)
    x = x_ref[...] + g_ref[...] * y
    if final:
        ms = jnp.mean(x * x, axis=-1, keepdims=True)
        x = x * lax.rsqrt(ms + EPS) * fg_ref[...]

    def write(k):
        o_refs[k][...] = x

    _for_row_part(pl.program_id(0), bounds, write)


def _moe_combine(x, parts, gates, gate_mod, final_g, seq_starts, out_rows):
    t, d = x.shape
    tm = _row_tile(t, 512, seq_starts)
    final = final_g is not None
    out_shape = [jax.ShapeDtypeStruct((r, d), F32) for r in out_rows]
    out_specs, bounds = _row_part_specs(out_shape, tm, d, lambda i: i, lambda i: 0)
    in_specs = ([pl.BlockSpec((tm, d), lambda i: (i, 0))]
                + [pl.BlockSpec((tm, d), lambda i: (i, 0))] * len(parts)
                + [pl.BlockSpec((tm, ROUTER_PAD), lambda i: (i, 0)),
                   pl.BlockSpec((None, 1, d), lambda i: (_seq_index(i * tm, seq_starts), 0, 0))])
    args = [x, *parts, gates, gate_mod]
    if final:
        in_specs.append(pl.BlockSpec((1, d), lambda i: (0, 0)))
        args.append(final_g)
    return pl.pallas_call(
        functools.partial(_combine_kernel, final=final, bounds=bounds),
        grid=(t // tm,),
        in_specs=in_specs,
        out_specs=out_specs,
        out_shape=out_shape,
        compiler_params=_params("arbitrary"),
        name="moe_combine_residual",
    )(*args)


def _ret_kernel(*refs, reverse, nchunks, seq_starts, seq_ends):
    rc = RET_CHUNK
    if reverse:
        (cd_ref, q_ref, kt_ref, v_ref, inner_ref, cross_ref, kvd_ref, of_ref, gate_ref, o_ref, st_scr) = refs
    else:
        (cd_ref, q_ref, kt_ref, v_ref, inner_ref, cross_ref, kvd_ref, o_ref, st_scr) = refs
    step = pl.program_id(0)
    if reverse:
        fresh = _is_any((nchunks - step) * rc, seq_ends)
    else:
        fresh = _is_any(step * rc, seq_starts)

    @pl.when(fresh)
    def _():
        st_scr[...] = jnp.zeros_like(st_scr)

    dk, dv = RET_QK_DIM, RET_V_DIM
    for h in range(RET_HEADS):
        q = q_ref[:, h * dk:(h + 1) * dk]
        kt = kt_ref[h * dk:(h + 1) * dk, :]
        v = v_ref[:, h * dv:(h + 1) * dv]
        s = jnp.dot(q, kt, preferred_element_type=F32) * inner_ref[h]
        st = st_scr[h]
        o = jnp.dot(s.astype(BF16), v, preferred_element_type=F32)
        o = o + jnp.dot(q, st.astype(BF16), preferred_element_type=F32) * cross_ref[h]
        ktd = (kt.astype(F32) * kvd_ref[h]).astype(BF16)
        st_scr[h] = st * cd_ref[h] + jnp.dot(ktd, v, preferred_element_type=F32)
        if reverse:
            o = o + of_ref[:, h * dv:(h + 1) * dv].astype(F32)
            ms = jnp.mean(o * o, axis=-1, keepdims=True)
            o = o * lax.rsqrt(ms + EPS) * _silu(gate_ref[:, h * dv:(h + 1) * dv].astype(F32))
        o_ref[:, h * dv:(h + 1) * dv] = o.astype(o_ref.dtype)


def _retention_direction(krt, proj, vcol0, gcol0, log_decay, seq_starts, seq_ends, reverse, o_fwd=None):
    t = proj.shape[0]
    rc = _tile(min(s2 - s1 for s1, s2 in zip(seq_starts, seq_ends)), RET_CHUNK)
    assert rc == RET_CHUNK
    nchunks = t // rc
    pos = jnp.arange(rc, dtype=F32)
    diff = pos[:, None] - pos[None, :]
    lg = log_decay.astype(F32)
    if reverse:
        mask = diff < 0
        dist = -diff
        cross = jnp.exp((rc - pos)[None, :, None] * lg[:, None, None])
        kvd = jnp.exp(pos[None, None, :] * lg[:, None, None])
    else:
        mask = diff >= 0
        dist = diff
        cross = jnp.exp((pos + 1.0)[None, :, None] * lg[:, None, None])
        kvd = jnp.exp((rc - 1.0 - pos)[None, None, :] * lg[:, None, None])
    inner = jnp.where(mask[None], jnp.exp(jnp.where(mask, dist, 0.0)[None] * lg[:, None, None]), 0.0)
    cross = jnp.broadcast_to(cross, (RET_HEADS, rc, RET_V_DIM))
    kvd = jnp.broadcast_to(kvd, (RET_HEADS, RET_QK_DIM, rc))
    chunk_decay = jnp.exp(rc * lg)

    def rb(i):
        return nchunks - 1 - i if reverse else i

    vb = vcol0 // RET_V_WIDTH
    assert vcol0 % RET_V_WIDTH == 0 and gcol0 % RET_V_WIDTH == 0
    in_specs = [pl.BlockSpec(memory_space=pltpu.SMEM),
                pl.BlockSpec((rc, RET_QK_WIDTH), lambda i: (rb(i), 0)),
                pl.BlockSpec((RET_QK_WIDTH, rc), lambda i: (0, rb(i))),
                pl.BlockSpec((rc, RET_V_WIDTH), lambda i: (rb(i), vb)),
                pl.BlockSpec((RET_HEADS, rc, rc), lambda i: (0, 0, 0)),
                pl.BlockSpec((RET_HEADS, rc, RET_V_DIM), lambda i: (0, 0, 0)),
                pl.BlockSpec((RET_HEADS, RET_QK_DIM, rc), lambda i: (0, 0, 0))]
    args = [chunk_decay, proj, krt, proj, inner, cross, kvd]
    if reverse:
        gb = gcol0 // RET_V_WIDTH
        in_specs += [pl.BlockSpec((rc, RET_V_WIDTH), lambda i: (rb(i), 0)),
                     pl.BlockSpec((rc, RET_V_WIDTH), lambda i: (rb(i), gb))]
        args += [o_fwd, proj]
    return pl.pallas_call(
        functools.partial(_ret_kernel, reverse=reverse, nchunks=nchunks,
                          seq_starts=seq_starts, seq_ends=seq_ends),
        grid=(nchunks,),
        in_specs=in_specs,
        out_specs=pl.BlockSpec((rc, RET_V_WIDTH), lambda i: (rb(i), 0)),
        out_shape=jax.ShapeDtypeStruct((t, RET_V_WIDTH), BF16),
        scratch_shapes=[pltpu.VMEM((RET_HEADS, RET_QK_DIM, RET_V_DIM), F32)],
        compiler_params=_params("arbitrary"),
        name="retention_bwd_norm_gate" if reverse else "retention_fwd",
    )(*args)


def _ssd_diff_layer(x, a1, sh1, p, i, layer, lay):
    seq_starts, seq_ends, groups = lay
    w_in = p['ab_w_in'][i]
    o2 = SSD_WIDTH + SSD_XBC
    o3 = o2 + 2 * SSD_HEADS
    w_main = jnp.concatenate([w_in[:, :o2].astype(BF16), w_in[:, o3:].astype(BF16)], axis=1)
    w_dt = jnp.pad(w_in[:, o2:o3], ((0, 0), (0, DT_PAD - 2 * SSD_HEADS))).astype(BF16)
    qcol = o2
    kcol = qcol + DA_QK_WIDTH
    vcol = kcol + DA_QK_WIDTH
    lmax = max(g[2] for g in groups)
    tables = (*_rope_tables(lmax, DA_HEAD_DIM, DA_HEAD_DIM ** -0.5 * math.log2(math.e)),
              *_rope_tables(lmax, DA_HEAD_DIM, 1.0))
    tn = 1024
    assert qcol % tn == 0 and kcol % tn == 0 and vcol % tn == 0
    proj, dt = _norm_mod_matmul(x, a1, sh1, w_main, w_dt, seq_starts,
                                ((qcol // tn, kcol // tn), (kcol // tn, vcol // tn)), DA_HEAD_DIM, tables)

    xact = _conv_silu(proj, SSD_WIDTH, p['ssd_conv_w'][i], p['ssd_conv_b'][i], seq_starts, seq_ends)
    bt = xact[:, SSD_WIDTH:SSD_WIDTH + SSD_BC].T
    dtt = dt.T
    expand = (jnp.arange(SSD_WIDTH)[None, :] // SSD_HEAD_DIM == jnp.arange(SSD_HEADS)[:, None]).astype(F32)
    a_f = -jnp.exp(p['ssd_a_log_fwd'][i].astype(F32))
    a_b = -jnp.exp(p['ssd_a_log_bwd'][i].astype(F32))
    yf = _ssd_direction(xact, bt, dt, dtt, p['ssd_dt_bias_fwd'][i].astype(F32), a_f, expand,
                        seq_starts, seq_ends, reverse=False)
    dskip = jnp.repeat(p['ssd_d'][i].astype(F32), SSD_HEAD_DIM).reshape(1, SSD_WIDTH)
    y_ssd = _ssd_direction(xact, bt, dt, dtt, p['ssd_dt_bias_bwd'][i].astype(F32), a_b, expand,
                           seq_starts, seq_ends, reverse=True,
                           final=(proj, yf, dskip, p['ssd_norm_g'][i].astype(F32).reshape(1, SSD_WIDTH)))

    lam_init = 0.8 - 0.6 * math.exp(-0.3 * layer)
    lam = (jnp.exp(jnp.sum(p['da_lambda_q1'][i].astype(F32) * p['da_lambda_k1'][i].astype(F32)))
           - jnp.exp(jnp.sum(p['da_lambda_q2'][i].astype(F32) * p['da_lambda_k2'][i].astype(F32)))
           + lam_init).reshape(1)
    subln = p['da_subln_g'][i].astype(F32).reshape(1, 2 * DA_HEAD_DIM)
    o = _diff_attention(proj, qcol, kcol, vcol, lam, subln, 1.0 - lam_init, seq_starts, seq_ends)

    w_out = p['ab_w_out'][i].astype(BF16)
    return (y_ssd, 0, o, 0, w_out)


def _retention_layer(x, a1, sh1, p, j, lay):
    seq_starts, seq_ends, groups = lay
    w_in = p['ret_w_in'][j].astype(BF16)
    lmax = max(g[2] for g in groups)
    tables = (*_rope_tables(lmax, RET_QK_DIM, 1.0), *_rope_tables(lmax, RET_QK_DIM, RET_QK_DIM ** -0.5))
    tn = 1024
    kcol = RET_QK_WIDTH
    vcol = 2 * RET_QK_WIDTH
    gcol = vcol + RET_V_WIDTH
    proj = _norm_mod_matmul(x, a1, sh1, w_in, None, seq_starts,
                            ((0, kcol // tn), (kcol // tn, vcol // tn)), RET_QK_DIM, tables)
    krt = proj[:, kcol:vcol].T
    lg_f = -jnp.exp(p['ret_log_decay_fwd'][j].astype(F32))
    lg_b = -jnp.exp(p['ret_log_decay_bwd'][j].astype(F32))
    of = _retention_direction(krt, proj, vcol, gcol, lg_f, seq_starts, seq_ends, reverse=False)
    o = _retention_direction(krt, proj, vcol, gcol, lg_b, seq_starts, seq_ends, reverse=True, o_fwd=of)
    w_out = p['ret_w_out'][j].astype(BF16)
    return (o, 0, o, 1, w_out)


def _moe_layer(x_parts, mix, g1, a2, sh2, g2, p, layer, lay, final_g, out_rows):
    seq_starts, _, _ = lay
    t = sum(xp.shape[0] for xp in x_parts)
    d = x_parts[0].shape[1]
    wr = jnp.concatenate([p['moe_w_group'][layer], p['moe_w_expert'][layer]], axis=1).astype(F32)
    br = jnp.concatenate([p['moe_b_group'][layer], p['moe_b_expert'][layer]]).astype(F32)
    nr = wr.shape[1]
    wr = jnp.pad(wr, ((0, 0), (0, ROUTER_PAD - nr)))
    br = jnp.pad(br, (0, ROUTER_PAD - nr)).reshape(1, ROUTER_PAD)
    x = _outproj_residual(*mix, x_parts, g1, seq_starts)
    h, route, cnt = _norm_mod_router(x, a2, sh2, wr, br, seq_starts)

    blk = MOE_ROW_BLOCK
    n_assign = t * MOE_TOP_K
    e_flat = route[:, 0:MOE_TOP_K].astype(jnp.int32).reshape(n_assign)
    rank = route[:, MOE_TOP_K:2 * MOE_TOP_K].astype(jnp.int32).reshape(n_assign)
    counts = cnt[0, MOE_GROUPS:MOE_GROUPS + MOE_EXPERTS].astype(jnp.int32)
    padded = (counts + blk - 1) // blk * blk
    ends_pad = jnp.cumsum(padded)
    start_pad = ends_pad - padded
    dest = (start_pad[e_flat] + rank).astype(jnp.int32)
    n_rows = n_assign + MOE_EXPERTS * blk
    n_blocks = n_rows // blk
    tok_flat = jnp.repeat(jnp.arange(t, dtype=jnp.int32), MOE_TOP_K)
    row_tok = (jnp.arange(n_rows, dtype=jnp.int32) % t).at[dest].set(tok_flat, unique_indices=True)
    ends_blk = jnp.cumsum(padded // blk).astype(jnp.int32)
    n_used = ends_blk[-1]
    block = jnp.minimum(jnp.arange(n_blocks, dtype=jnp.int32), n_used - 1)
    block_expert = jnp.minimum(jnp.sum(ends_blk[None, :] <= block[:, None], axis=1),
                               MOE_EXPERTS - 1).astype(jnp.int32)

    assert n_blocks % MOE_ROW_RANGES == 0
    nbr = n_blocks // MOE_ROW_RANGES
    pos = jnp.arange(nbr, dtype=jnp.int32)
    out = None
    for rg in range(MOE_ROW_RANGES):
        b0 = rg * nbr
        used = jnp.clip(n_used - b0, 0, nbr)
        expert = block_expert[b0:b0 + nbr]
        first = jnp.concatenate([jnp.ones((1,), jnp.int32), (expert[1:] != expert[:-1]).astype(jnp.int32)])
        run_start = jnp.where(first == 1, pos, nbr)
        next_start = lax.cummin(jnp.concatenate([run_start[1:], jnp.full((1,), nbr, jnp.int32)]), reverse=True)
        has_next = (next_start < nbr).astype(jnp.int32)
        next_expert = expert[jnp.minimum(next_start, nbr - 1)]
        x_block = jnp.clip(pos, 0, jnp.maximum(used - 1, 0))
        tables = (x_block, expert, first, next_expert, has_next, used.reshape(1))
        xs = jnp.take(h, row_tok[b0 * blk:(b0 + nbr) * blk], axis=0, mode='clip')
        out = _moe_ffn(xs, tables, p['moe_w1'], p['moe_w3'], p['moe_w2'], layer, n_rows, b0, out)
    dest2 = dest.reshape(t, MOE_TOP_K)
    parts = [jnp.take(out, dest2[:, k], axis=0, mode='clip', unique_indices=True) for k in range(MOE_TOP_K)]
    return tuple(_moe_combine(x, parts, route, g2, final_g, seq_starts, out_rows))


def kernel(x_prompt, x_sample, c_prompt, c_sample, ada_w, ada_b, norm_mix_g, norm_ffn_g, ab_w_in, ssd_conv_w, ssd_conv_b, ssd_a_log_fwd, ssd_a_log_bwd, ssd_dt_bias_fwd, ssd_dt_bias_bwd, ssd_d, ssd_norm_g, da_lambda_q1, da_lambda_k1, da_lambda_q2, da_lambda_k2, da_subln_g, ab_w_out, ret_w_in, ret_log_decay_fwd, ret_log_decay_bwd, ret_w_out, moe_w_group, moe_b_group, moe_w_expert, moe_b_expert, moe_w1, moe_w3, moe_w2, final_norm_g):
    p = dict(ab_w_in=ab_w_in, ssd_conv_w=ssd_conv_w, ssd_conv_b=ssd_conv_b,
             ssd_a_log_fwd=ssd_a_log_fwd, ssd_a_log_bwd=ssd_a_log_bwd,
             ssd_dt_bias_fwd=ssd_dt_bias_fwd, ssd_dt_bias_bwd=ssd_dt_bias_bwd, ssd_d=ssd_d,
             ssd_norm_g=ssd_norm_g, da_lambda_q1=da_lambda_q1, da_lambda_k1=da_lambda_k1,
             da_lambda_q2=da_lambda_q2, da_lambda_k2=da_lambda_k2, da_subln_g=da_subln_g,
             ab_w_out=ab_w_out, ret_w_in=ret_w_in, ret_log_decay_fwd=ret_log_decay_fwd,
             ret_log_decay_bwd=ret_log_decay_bwd, ret_w_out=ret_w_out, moe_w_group=moe_w_group,
             moe_b_group=moe_b_group, moe_w_expert=moe_w_expert, moe_b_expert=moe_b_expert,
             moe_w1=moe_w1, moe_w3=moe_w3, moe_w2=moe_w2)
    bp, lp, d = x_prompt.shape
    bs, ls, _ = x_sample.shape
    depth = ada_w.shape[0]
    groups = ((0, bp, lp), (bp * lp, bs, ls))
    seq_starts = tuple(r0 + b * l for (r0, nb, l) in groups for b in range(nb))
    seq_ends = tuple(r0 + (b + 1) * l for (r0, nb, l) in groups for b in range(nb))
    lay = (seq_starts, seq_ends, groups)
    nseq = len(seq_starts)
    group_rows = (bp * lp, bs * ls)
    x = (x_prompt.reshape(bp * lp, d), x_sample.reshape(bs * ls, d))
    c = jnp.concatenate([c_prompt, c_sample], axis=0).astype(F32)
    c_pad = jnp.pad(c, ((0, -nseq % 8), (0, 0)))
    mod = _ada_modulation(c_pad, ada_w.astype(F32), ada_b.astype(F32))[:, :nseq]

    for layer in range(depth):
        sh1, sc1, g1, sh2, sc2, g2 = [m.reshape(nseq, 1, d) for m in jnp.split(mod[layer], 6, axis=-1)]
        a1 = norm_mix_g[layer].astype(F32)[None, None, :] * (1.0 + sc1)
        a2 = norm_ffn_g[layer].astype(F32)[None, None, :] * (1.0 + sc2)
        if layer % 2 == 0:
            mix = _ssd_diff_layer(x, a1, sh1, p, layer // 2, layer, lay)
        else:
            mix = _retention_layer(x, a1, sh1, p, layer // 2, lay)
        last = layer == depth - 1
        final_g = final_norm_g.astype(F32).reshape(1, d) if last else None
        x = _moe_layer(x, mix, g1, a2, sh2, g2, p, layer, lay, final_g,
                       group_rows if last else (sum(group_rows),))
    return (x[0].reshape(bp, lp, d), x[1].reshape(bs, ls, d))
```

```python
import functools
import math

import jax
import jax.numpy as jnp
from jax import lax
from jax.experimental import pallas as pl
from jax.experimental.pallas import tpu as pltpu

F32 = jnp.float32
BF16 = jnp.bfloat16
HIGHEST = lax.Precision.HIGHEST

EPS = 1e-6
SUBLN_EPS = 1e-5
ROPE_THETA = 10000.0
ROPE_FINE = 64

SSD_HEADS = 32
SSD_HEAD_DIM = 64
SSD_WIDTH = SSD_HEADS * SSD_HEAD_DIM
SSD_GROUPS = 4
SSD_STATE = 128
SSD_CONV = 5
SSD_BC = SSD_GROUPS * SSD_STATE
SSD_XBC = SSD_WIDTH + 2 * SSD_BC
SSD_CHUNK = 128
DA_HEADS = 8
DA_HEAD_DIM = 128
DA_QK_WIDTH = 2 * DA_HEADS * DA_HEAD_DIM
DA_V_WIDTH = DA_HEADS * 2 * DA_HEAD_DIM
RET_HEADS = 8
RET_QK_DIM = 256
RET_V_DIM = 512
RET_QK_WIDTH = RET_HEADS * RET_QK_DIM
RET_V_WIDTH = RET_HEADS * RET_V_DIM
RET_CHUNK = 256
MOE_GROUPS = 4
MOE_EXPERTS_PER_GROUP = 8
MOE_EXPERTS = MOE_GROUPS * MOE_EXPERTS_PER_GROUP
MOE_TOP_K = 2
MOE_ROW_BLOCK = 512
MOE_FF_CHUNKS = 4
NMM_COL_CHUNK = 256
ROUTER_PAD = 128
DT_PAD = 128
HALO = 16

VMEM_LIMIT_BYTES = 56 * 1024 * 1024
MOE_VMEM_LIMIT_BYTES = 62 * 1024 * 1024


def _params(*semantics, vmem_limit_bytes=VMEM_LIMIT_BYTES):
    return pltpu.CompilerParams(dimension_semantics=semantics, vmem_limit_bytes=vmem_limit_bytes)


def _silu(x):
    hx = 0.5 * x
    return hx + hx * jnp.tanh(hx)


def _softplus(x):
    return jnp.maximum(x, 0.0) + jnp.log1p(jnp.exp(-jnp.abs(x)))


def _lane_repeat(x, n):
    return x if n == 1 else jnp.concatenate([x] * n, axis=1)


def _tile(n, pref):
    t = min(n, pref)
    assert n % t == 0, (n, pref)
    return t


def _row_tile(t, pref, seq_starts):
    tm = min(t, pref)
    while t % tm or any(st % tm for st in seq_starts):
        tm //= 2
    return tm


def _seq_index(row, seq_starts):
    s = 0
    for st in seq_starts[1:]:
        s = s + jnp.where(row >= st, 1, 0)
    return s


def _is_any(row, values):
    hit = row == values[0]
    for v in values[1:]:
        hit = jnp.logical_or(hit, row == v)
    return hit


def _ada_kernel(c_ref, w_ref, b_ref, o_ref):
    o_ref[...] = jnp.dot(_silu(c_ref[...]), w_ref[...], preferred_element_type=F32,
                         precision=HIGHEST) + b_ref[...]


def _ada_modulation(c_pad, ada_w, ada_b):
    depth, d, n = ada_w.shape
    rows = c_pad.shape[0]
    tn = _tile(n, 1024)
    return pl.pallas_call(
        _ada_kernel,
        grid=(depth, n // tn),
        in_specs=[pl.BlockSpec((rows, d), lambda l, j: (0, 0)),
                  pl.BlockSpec((None, d, tn), lambda l, j: (l, 0, j)),
                  pl.BlockSpec((None, 1, tn), lambda l, j: (l, 0, j))],
        out_specs=pl.BlockSpec((None, rows, tn), lambda l, j: (l, 0, j)),
        out_shape=jax.ShapeDtypeStruct((depth, rows, n), F32),
        compiler_params=_params("arbitrary", "arbitrary"),
        name="ada_modulation",
    )(c_pad, ada_w, ada_b.reshape(depth, 1, n))


def _norm_mod(x, a, sh):
    ms = jnp.mean(x * x, axis=-1, keepdims=True)
    return (x * lax.rsqrt(ms + EPS)) * a + sh


def _rotate_half_tile(x, cos, sin, head_dim):
    half = head_dim // 2
    outs = []
    for h in range(x.shape[1] // head_dim):
        xh = x[:, h * head_dim:(h + 1) * head_dim]
        if half % 128 == 0:
            rot = jnp.concatenate([xh[:, half:], xh[:, :half]], axis=-1)
        else:
            rot = pltpu.roll(xh, half, 1)
        outs.append(xh * cos + rot * sin)
    return jnp.concatenate(outs, axis=-1)


def _row_part_specs(parts, tm, width, row_of, col_of, single_buffer=False):
    specs, bounds = [], [0]
    mode = dict(pipeline_mode=pl.Buffered(1)) if single_buffer else {}
    for arr in parts:
        b0, nb = bounds[-1], arr.shape[0] // tm
        assert arr.shape[0] % tm == 0
        specs.append(pl.BlockSpec(
            (tm, width), lambda *g, b0=b0, nb=nb: (jnp.clip(row_of(*g) - b0, 0, nb - 1), col_of(*g)), **mode))
        bounds.append(b0 + nb)
    return specs, tuple(bounds)


def _for_row_part(i, bounds, fn):
    if len(bounds) == 2:
        fn(0)
        return
    for k in range(len(bounds) - 1):
        pl.when(jnp.logical_and(i >= bounds[k], i < bounds[k + 1]))(functools.partial(fn, k))


def _nmm_kernel(*refs, with_dt, rope_tiles, head_dim, bounds):
    n_parts = len(bounds) - 1
    x_refs = refs[:n_parts]
    a_ref, sh_ref, w_ref, cq_ref, sq_ref, ck_ref, sk_ref = refs[n_parts:n_parts + 7]
    if with_dt:
        wdt_ref, o_ref, odt_ref, h_scr = refs[n_parts + 7:]
    else:
        o_ref, h_scr = refs[n_parts + 7:]
    j = pl.program_id(1)

    def normalize(k):
        hb = _norm_mod(x_refs[k][...], a_ref[...], sh_ref[...]).astype(BF16)
        h_scr[...] = hb
        if with_dt:
            odt_ref[...] = jnp.dot(hb, wdt_ref[...], preferred_element_type=F32)

    @pl.when(j == 0)
    def _():
        _for_row_part(pl.program_id(0), bounds, normalize)

    (q_lo, q_hi), (k_lo, k_hi) = rope_tiles
    is_q = jnp.logical_and(j >= q_lo, j < q_hi)
    is_k = jnp.logical_and(j >= k_lo, j < k_hi)
    rotary = jnp.logical_or(is_q, is_k)
    cos = jnp.where(rotary, jnp.where(is_q, cq_ref[...], ck_ref[...]), 1.0)
    sin = jnp.where(rotary, jnp.where(is_q, sq_ref[...], sk_ref[...]), 0.0)
    cw = max(head_dim, NMM_COL_CHUNK)
    for c in range(w_ref.shape[1] // cw):
        acc = jnp.dot(h_scr[...], w_ref[:, c * cw:(c + 1) * cw], preferred_element_type=F32)
        o_ref[:, c * cw:(c + 1) * cw] = _rotate_half_tile(acc, cos, sin, head_dim).astype(o_ref.dtype)


def _norm_mod_matmul(x_parts, a, sh, w, wdt, seq_starts, rope_tiles, head_dim, tables):
    t = sum(xp.shape[0] for xp in x_parts)
    d = x_parts[0].shape[1]
    n = w.shape[1]
    tm = _row_tile(t, 1024, seq_starts)
    tn = _tile(n, 1024)
    with_dt = wdt is not None

    def seq_map(i, j):
        return (_seq_index(i * tm, seq_starts), 0, 0)

    def pos_map(i, j):
        r = i * tm
        s = _seq_index(r, seq_starts)
        st = 0
        for k, v in enumerate(seq_starts):
            st = st + jnp.where(s == k, v, 0)
        return ((r - st) // tm, 0)

    x_specs, bounds = _row_part_specs(x_parts, tm, d, lambda i, j: i, lambda i, j: 0,
                                      single_buffer=len(x_parts) > 1)
    in_specs = x_specs + [pl.BlockSpec((None, 1, d), seq_map),
                          pl.BlockSpec((None, 1, d), seq_map),
                          pl.BlockSpec((d, tn), lambda i, j: (0, j))] + [pl.BlockSpec((tm, head_dim), pos_map)] * 4
    out_specs = [pl.BlockSpec((tm, tn), lambda i, j: (i, j))]
    out_shape = [jax.ShapeDtypeStruct((t, n), BF16)]
    args = [*x_parts, a, sh, w, *tables]
    if with_dt:
        in_specs.append(pl.BlockSpec((d, DT_PAD), lambda i, j: (0, 0)))
        out_specs.append(pl.BlockSpec((tm, DT_PAD), lambda i, j: (i, 0)))
        out_shape.append(jax.ShapeDtypeStruct((t, DT_PAD), F32))
        args.append(wdt)
    res = pl.pallas_call(
        functools.partial(_nmm_kernel, with_dt=with_dt, rope_tiles=rope_tiles, head_dim=head_dim, bounds=bounds),
        grid=(t // tm, n // tn),
        in_specs=in_specs, out_specs=out_specs, out_shape=out_shape,
        scratch_shapes=[pltpu.VMEM((tm, d), BF16)],
        compiler_params=_params("arbitrary", "arbitrary"),
        name="norm_mod_matmul",
    )(*args)
    return res if with_dt else res[0]


def _conv_kernel(cur_ref, prev_ref, next_ref, w_ref, b_ref, o_ref, ext_scr, *, tm, seq_starts, seq_ends):
    r0 = pl.program_id(0) * tm
    at_start = _is_any(r0, seq_starts)
    at_end = _is_any(r0 + tm, seq_ends)
    prev = prev_ref[...].astype(F32)[HALO - 8:HALO]
    nxt = next_ref[...].astype(F32)[0:8]
    ext_scr[0:8, :] = jnp.where(at_start, 0.0, prev)
    ext_scr[8:8 + tm, :] = cur_ref[...].astype(F32)
    ext_scr[8 + tm:16 + tm, :] = jnp.where(at_end, 0.0, nxt)
    ext = ext_scr[...]
    n = tm + 16
    pad = SSD_CONV // 2
    below = w_ref[0:1, :] * ext
    for k in range(1, pad):
        below = w_ref[k:k + 1, :] * ext + pltpu.roll(below, 1, 0)
    above = w_ref[SSD_CONV - 1:SSD_CONV, :] * ext
    for k in range(SSD_CONV - 2, pad, -1):
        above = w_ref[k:k + 1, :] * ext + pltpu.roll(above, n - 1, 0)
    acc = w_ref[pad:pad + 1, :] * ext + pltpu.roll(below, 1, 0) + pltpu.roll(above, n - 1, 0)
    acc = acc[8:8 + tm, :] + b_ref[...]
    o_ref[...] = _silu(acc).astype(o_ref.dtype)


def _conv_silu(proj, col0, conv_w, conv_b, seq_starts, seq_ends):
    t = proj.shape[0]
    tm = _row_tile(t, 512, seq_starts)
    tc = 1024
    assert col0 % tc == 0 and SSD_XBC % tc == 0 and tm % HALO == 0
    cb = col0 // tc
    hb = tm // HALO
    last_halo = t // HALO - 1
    return pl.pallas_call(
        functools.partial(_conv_kernel, tm=tm, seq_starts=seq_starts, seq_ends=seq_ends),
        grid=(t // tm, SSD_XBC // tc),
        in_specs=[pl.BlockSpec((tm, tc), lambda i, j: (i, cb + j)),
                  pl.BlockSpec((HALO, tc), lambda i, j: (jnp.maximum(i * hb - 1, 0), cb + j)),
                  pl.BlockSpec((HALO, tc), lambda i, j: (jnp.minimum((i + 1) * hb, last_halo), cb + j)),
                  pl.BlockSpec((SSD_CONV, tc), lambda i, j: (0, j)),
                  pl.BlockSpec((1, tc), lambda i, j: (0, j))],
        out_specs=pl.BlockSpec((tm, tc), lambda i, j: (i, j)),
        out_shape=jax.ShapeDtypeStruct((t, SSD_XBC), BF16),
        scratch_shapes=[pltpu.VMEM((tm + 16, tc), F32)],
        compiler_params=_params("arbitrary", "arbitrary"),
        name="ssd_conv_silu",
    )(proj, proj, proj, conv_w, conv_b.reshape(1, SSD_XBC))


def _ssd_kernel(*refs, reverse, dcol, nchunks, seq_starts, seq_ends):
    ch = SSD_CHUNK
    if reverse:
        (xact_ref, bt_ref, dt_ref, dtt_ref, bias_ref, a_ref, biast_ref, at_ref, e_ref,
         z_ref, yf_ref, dskip_ref, g_ref, o_ref, h_scr, y_scr) = refs
    else:
        (xact_ref, bt_ref, dt_ref, dtt_ref, bias_ref, a_ref, biast_ref, at_ref, e_ref,
         o_ref, h_scr) = refs
    step = pl.program_id(0)
    if reverse:
        row_end = (nchunks - step) * ch
        fresh = _is_any(row_end, seq_ends)
    else:
        fresh = _is_any(step * ch, seq_starts)

    @pl.when(fresh)
    def _():
        h_scr[...] = jnp.zeros_like(h_scr)

    dt = _softplus(dt_ref[:, dcol:dcol + SSD_HEADS] + bias_ref[...])
    dta = dt * a_ref[...]
    dtt = _softplus(dtt_ref[dcol:dcol + SSD_HEADS, :] + biast_ref[...])
    dtat = dtt * at_ref[...]
    row = lax.broadcasted_iota(jnp.int32, (ch, ch), 0)
    col = lax.broadcasted_iota(jnp.int32, (ch, ch), 1)
    if reverse:
        keep = col >= row
    else:
        keep = col <= row
    tri = jnp.where(keep, 1.0, 0.0).astype(F32)
    trit = jnp.where(keep, 0.0, 1.0).astype(F32) + jnp.where(row == col, 1.0, 0.0).astype(F32)
    cs = jnp.dot(tri, dta, preferred_element_type=F32, precision=HIGHEST)
    cst = jnp.dot(dtat, trit, preferred_element_type=F32, precision=HIGHEST)
    last = 0 if reverse else ch - 1
    ecs = jnp.exp(cs)
    wt = dtt * jnp.exp(cst[:, last:last + 1] - cst)
    cdec_x = jnp.dot(jnp.broadcast_to(ecs[last:last + 1, :], (8, SSD_HEADS)), e_ref[...],
                     preferred_element_type=F32, precision=HIGHEST)[0:1, :]

    lane = lax.broadcasted_iota(jnp.int32, (1, 2 * SSD_HEAD_DIM), 1)
    first = lane < SSD_HEAD_DIM
    gw = SSD_WIDTH // SSD_GROUPS
    hpg = SSD_HEADS // SSD_GROUPS
    pw = 2 * SSD_HEAD_DIM
    ssq = jnp.zeros((ch, pw), F32)
    for g in range(SSD_GROUPS):
        bgt = bt_ref[g * SSD_STATE:(g + 1) * SSD_STATE, :]
        c0 = SSD_WIDTH + SSD_BC + g * SSD_STATE
        cg = xact_ref[:, c0:c0 + SSD_STATE]
        scores = jnp.dot(cg, bgt, preferred_element_type=F32)
        cgf = cg.astype(F32)
        bgtf = bgt.astype(F32)
        for pr in range(hpg // 2):
            lo = g * gw + pr * pw
            hi = lo + pw
            xpair = xact_ref[:, lo:hi].astype(F32)
            hpair = h_scr[g, :, pr * pw:(pr + 1) * pw]
            xsel = (jnp.where(first, xpair, 0.0).astype(BF16), jnp.where(first, 0.0, xpair).astype(BF16))
            hsel = (jnp.where(first, hpair, 0.0).astype(BF16), jnp.where(first, 0.0, hpair).astype(BF16))
            y = None
            bws = []
            for u in range(2):
                h = g * hpg + 2 * pr + u
                csb = jnp.broadcast_to(cs[:, h:h + 1], (ch, ch))
                seg = jnp.where(keep, csb - cst[h:h + 1, :], -jnp.inf)
                m = (scores * jnp.exp(seg) * dtt[h:h + 1, :]).astype(BF16)
                ce = (cgf * jnp.exp(csb)).astype(BF16)
                part = jnp.dot(jnp.concatenate([m, ce], axis=1), jnp.concatenate([xsel[u], hsel[u]], axis=0),
                               preferred_element_type=F32)
                y = part if y is None else y + part
                bws.append((bgtf * wt[h:h + 1, :]).astype(BF16))
            h_scr[g, :, pr * pw:(pr + 1) * pw] = hpair * cdec_x[:, lo:hi] + jnp.dot(
                jnp.concatenate(bws, axis=1), jnp.concatenate(xsel, axis=0), preferred_element_type=F32)
            if reverse:
                y = y + yf_ref[:, lo:hi].astype(F32) + dskip_ref[:, lo:hi] * xpair
                y = y * _silu(z_ref[:, lo:hi].astype(F32))
                ssq = ssq + y * y
                y_scr[:, lo:hi] = y
            else:
                o_ref[:, lo:hi] = y.astype(o_ref.dtype)
    if reverse:
        inv = lax.rsqrt(jnp.sum(ssq, axis=-1, keepdims=True) * (1.0 / SSD_WIDTH) + EPS)
        o_ref[...] = (y_scr[...] * inv * g_ref[...]).astype(o_ref.dtype)


def _ssd_direction(xact, bt, dt, dtt, bias, a, expand, seq_starts, seq_ends, reverse, final=None):
    t = xact.shape[0]
    ch = SSD_CHUNK
    nchunks = t // ch
    dcol = SSD_HEADS if reverse else 0

    def rb(i):
        return nchunks - 1 - i if reverse else i

    in_specs = [pl.BlockSpec((ch, SSD_XBC), lambda i: (rb(i), 0)),
                pl.BlockSpec((SSD_BC, ch), lambda i: (0, rb(i))),
                pl.BlockSpec((ch, DT_PAD), lambda i: (rb(i), 0)),
                pl.BlockSpec((DT_PAD, ch), lambda i: (0, rb(i))),
                pl.BlockSpec((1, SSD_HEADS), lambda i: (0, 0)),
                pl.BlockSpec((1, SSD_HEADS), lambda i: (0, 0)),
                pl.BlockSpec((SSD_HEADS, 1), lambda i: (0, 0)),
                pl.BlockSpec((SSD_HEADS, 1), lambda i: (0, 0)),
                pl.BlockSpec((SSD_HEADS, SSD_WIDTH), lambda i: (0, 0))]
    args = [xact, bt, dt, dtt, bias.reshape(1, -1), a.reshape(1, -1), bias.reshape(-1, 1), a.reshape(-1, 1), expand]
    scratch = [pltpu.VMEM((SSD_GROUPS, SSD_STATE, SSD_WIDTH // SSD_GROUPS), F32)]
    if reverse:
        proj, yf, dskip, g = final
        in_specs += [pl.BlockSpec((ch, SSD_WIDTH), lambda i: (rb(i), 0)),
                     pl.BlockSpec((ch, SSD_WIDTH), lambda i: (rb(i), 0)),
                     pl.BlockSpec((1, SSD_WIDTH), lambda i: (0, 0)),
                     pl.BlockSpec((1, SSD_WIDTH), lambda i: (0, 0))]
        args += [proj, yf, dskip, g]
        scratch.append(pltpu.VMEM((ch, SSD_WIDTH), F32))
    return pl.pallas_call(
        functools.partial(_ssd_kernel, reverse=reverse, dcol=dcol, nchunks=nchunks,
                          seq_starts=seq_starts, seq_ends=seq_ends),
        grid=(nchunks,),
        in_specs=in_specs,
        out_specs=pl.BlockSpec((ch, SSD_WIDTH), lambda i: (rb(i), 0)),
        out_shape=jax.ShapeDtypeStruct((t, SSD_WIDTH), BF16),
        scratch_shapes=scratch,
        compiler_params=_params("arbitrary"),
        name="ssd_bwd_gate_norm" if reverse else "ssd_fwd",
    )(*args)


def _rope_tables(lmax, head_dim, scale):
    half = head_dim // 2
    inv = 1.0 / (ROPE_THETA ** (jnp.arange(half, dtype=F32) / half))
    fine = min(lmax, ROPE_FINE)
    assert lmax % fine == 0
    a = (jnp.arange(lmax // fine, dtype=F32) * fine)[:, None] * inv[None, :]
    b = jnp.arange(fine, dtype=F32)[:, None] * inv[None, :]
    ca, sa, cb, sb = jnp.cos(a)[:, None, :], jnp.sin(a)[:, None, :], jnp.cos(b)[None], jnp.sin(b)[None]
    cos = (ca * cb - sa * sb).reshape(lmax, half) * scale
    sin = (sa * cb + ca * sb).reshape(lmax, half) * scale
    return jnp.concatenate([cos, cos], axis=-1), jnp.concatenate([-sin, sin], axis=-1)


def _seq_lookup(row, seq_starts, values):
    s = _seq_index(row, seq_starts)
    out = 0
    for k, v in enumerate(values):
        out = out + jnp.where(s == k, v, 0)
    return out


def _attn_kernel(lam_ref, q_ref, k_ref, v_ref, g_ref, o_ref, m_scr, l_scr, acc_scr, s_scr, p_scr, alpha_scr, *,
                 tk, nkv, unroll, strip, out_scale, seq_starts, seq_kv_blocks, seq_sub_block):
    dh = DA_HEAD_DIM
    tq = q_ref.shape[0]
    lanes = m_scr.shape[-1]
    row0 = pl.program_id(1) * tq
    n_valid = _seq_lookup(row0, seq_starts, seq_kv_blocks)
    sub0 = _seq_lookup(row0, seq_starts, seq_sub_block)
    m_scr[...] = jnp.full_like(m_scr, -jnp.inf)
    l_scr[...] = jnp.zeros_like(l_scr)
    acc_scr[...] = jnp.zeros_like(acc_scr)

    def chunk(c):
        sub = sub0 + c // nkv
        off = pl.multiple_of((c % nkv) * tk, tk)
        for u in range(2):
            s_scr[u] = lax.dot_general(q_ref[:, u * dh:(u + 1) * dh],
                                       k_ref[sub, pl.ds(off, tk), u * dh:(u + 1) * dh],
                                       (((1,), (1,)), ((), ())), preferred_element_type=F32)
        for u in range(2):
            for r in range(tq // strip):
                rows = slice(r * strip, (r + 1) * strip)
                s = s_scr[u, rows, :]
                m_prev = m_scr[u, rows, :]
                m_new = jnp.maximum(m_prev, jnp.max(s, axis=-1, keepdims=True))
                alpha = jnp.exp2(m_prev - m_new)
                p = jnp.exp2(s - _lane_repeat(m_new, tk // lanes))
                l_scr[u, rows, :] = alpha * l_scr[u, rows, :] + jnp.sum(p, axis=-1, keepdims=True)
                m_scr[u, rows, :] = m_new
                alpha_scr[u, rows, :] = alpha
                p_scr[u, rows, :] = p.astype(BF16)
            pv = jnp.dot(p_scr[u], v_ref[sub, pl.ds(off, tk), :], preferred_element_type=F32)
            acc_scr[u] = acc_scr[u] * _lane_repeat(alpha_scr[u], acc_scr.shape[-1] // lanes) + pv

    def body(j, carry):
        for w in range(unroll):
            chunk(j * unroll + w)
        return carry

    lax.fori_loop(0, n_valid * (nkv // unroll), body, 0)
    lam = lam_ref[0]
    rep = acc_scr.shape[-1] // lanes
    o = (acc_scr[0] / _lane_repeat(l_scr[0], rep)
         - lam * (acc_scr[1] / _lane_repeat(l_scr[1], rep)))
    ms = jnp.mean(o * o, axis=-1, keepdims=True)
    o_ref[...] = (o * lax.rsqrt(ms + SUBLN_EPS) * g_ref[...] * out_scale).astype(o_ref.dtype)


def _diff_attention(proj, qcol0, kcol0, vcol0, lam, subln_g, out_scale, seq_starts, seq_ends):
    t = proj.shape[0]
    pw = 2 * DA_HEAD_DIM
    lens = tuple(e - s for s, e in zip(seq_starts, seq_ends))
    kvb = min(lens)
    group = max(lens) // kvb
    assert all(n % kvb == 0 for n in lens) and all(s % kvb == 0 for s in seq_starts) and (t // kvb) % group == 0
    tq = _tile(kvb, 1024)
    tk = _tile(kvb, 2048)
    nkv = kvb // tk
    strip = 16
    lanes = 128
    assert vcol0 % pw == 0 and qcol0 % pw == 0 and kcol0 % pw == 0
    qc0, kc0, vc0 = qcol0 // pw, kcol0 // pw, vcol0 // pw
    seq_kv_blocks = tuple(n // kvb for n in lens)
    seq_group = tuple((s // kvb) // group for s in seq_starts)
    seq_sub_block = tuple((s // kvb) % group for s in seq_starts)
    assert all(sb + nb <= group for sb, nb in zip(seq_sub_block, seq_kv_blocks))
    kv_view = proj.reshape(t // kvb, kvb, proj.shape[1])

    def kv_map(col0):
        return lambda h, i: (_seq_lookup(i * tq, seq_starts, seq_group), 0, col0 + h)

    return pl.pallas_call(
        functools.partial(_attn_kernel, tk=tk, nkv=nkv, unroll=2 if nkv % 2 == 0 else 1, strip=strip,
                          out_scale=out_scale, seq_starts=seq_starts, seq_kv_blocks=seq_kv_blocks,
                          seq_sub_block=seq_sub_block),
        grid=(DA_HEADS, t // tq),
        in_specs=[pl.BlockSpec(memory_space=pltpu.SMEM),
                  pl.BlockSpec((tq, pw), lambda h, i: (i, qc0 + h)),
                  pl.BlockSpec((group, kvb, pw), kv_map(kc0)),
                  pl.BlockSpec((group, kvb, pw), kv_map(vc0)),
                  pl.BlockSpec((1, pw), lambda h, i: (0, 0))],
        out_specs=pl.BlockSpec((tq, pw), lambda h, i: (i, h)),
        out_shape=jax.ShapeDtypeStruct((t, DA_V_WIDTH), BF16),
        scratch_shapes=[pltpu.VMEM((2, tq, lanes), F32), pltpu.VMEM((2, tq, lanes), F32),
                        pltpu.VMEM((2, tq, pw), F32), pltpu.VMEM((2, tq, tk), F32), pltpu.VMEM((2, tq, tk), BF16),
                        pltpu.VMEM((2, tq, lanes), F32)],
        compiler_params=_params("arbitrary", "arbitrary"),
        name="diff_attention",
    )(lam, proj, kv_view, kv_view, subln_g)


def _outproj_kernel(y1_ref, y2_ref, w1_ref, w2_ref, g_ref, *rest, bounds):
    x_refs, o_ref = rest[:-1], rest[-1]
    acc = jnp.dot(y1_ref[...], w1_ref[...], preferred_element_type=F32)
    acc = acc + jnp.dot(y2_ref[...], w2_ref[...], preferred_element_type=F32)
    upd = g_ref[...] * acc

    def residual(k):
        o_ref[...] = x_refs[k][...] + upd

    _for_row_part(pl.program_id(1), bounds, residual)


def _outproj_residual(y1, c1, y2, c2, w, x_parts, gate, seq_starts):
    t = sum(xp.shape[0] for xp in x_parts)
    d = x_parts[0].shape[1]
    kh = w.shape[0] // 2
    tm = _row_tile(t, 512, seq_starts)
    tn = _tile(d, 1024)
    x_specs, bounds = _row_part_specs(x_parts, tm, tn, lambda j, i: i, lambda j, i: j)
    return pl.pallas_call(
        functools.partial(_outproj_kernel, bounds=bounds),
        grid=(d // tn, t // tm),
        in_specs=[pl.BlockSpec((tm, kh), lambda j, i: (i, c1)),
                  pl.BlockSpec((tm, kh), lambda j, i: (i, c2)),
                  pl.BlockSpec((kh, tn), lambda j, i: (0, j)),
                  pl.BlockSpec((kh, tn), lambda j, i: (1, j)),
                  pl.BlockSpec((None, 1, tn), lambda j, i: (_seq_index(i * tm, seq_starts), 0, j))] + x_specs,
        out_specs=pl.BlockSpec((tm, tn), lambda j, i: (i, j)),
        out_shape=jax.ShapeDtypeStruct((t, d), F32),
        compiler_params=_params("arbitrary", "arbitrary"),
        name="outproj_residual",
    )(y1, y2, w, w, gate, *x_parts)


def _router_kernel(x_ref, a_ref, sh_ref, wr_ref, br_ref, h_ref, rt_ref, cnt_ref, carry_scr):
    @pl.when(pl.program_id(0) == 0)
    def _():
        carry_scr[...] = jnp.zeros_like(carry_scr)

    h = _norm_mod(x_ref[...], a_ref[...], sh_ref[...])
    h_ref[...] = h.astype(h_ref.dtype)
    lg = jnp.dot(h, wr_ref[...], preferred_element_type=F32, precision=HIGHEST) + br_ref[...]
    tm, lanes = lg.shape
    lane = lax.broadcasted_iota(jnp.int32, (tm, lanes), 1).astype(F32)
    none = float(lanes)

    def top1(mask):
        v = jnp.max(jnp.where(mask, lg, -jnp.inf), axis=-1, keepdims=True)
        idx = jnp.min(jnp.where(mask, jnp.where(lg == v, lane, none), none), axis=-1, keepdims=True)
        return v, idx

    is_group = lane < MOE_GROUPS
    cm, grp = top1(is_group)
    p_grp = 1.0 / jnp.sum(jnp.where(is_group, jnp.exp(lg - cm), 0.0), axis=-1, keepdims=True)
    lo = MOE_GROUPS + grp * MOE_EXPERTS_PER_GROUP
    in_grp = jnp.logical_and(lane >= lo, lane < lo + MOE_EXPERTS_PER_GROUP)
    v1, i1 = top1(in_grp)
    v2, i2 = top1(jnp.logical_and(in_grp, lane != i1))
    e12 = jnp.exp(v2 - v1)
    g1 = p_grp / (1.0 + e12)
    g2 = g1 * e12

    onehot = jnp.where(lane == i1, 1.0, 0.0) + jnp.where(lane == i2, 1.0, 0.0)
    row = lax.broadcasted_iota(jnp.int32, (tm, tm), 0)
    col = lax.broadcasted_iota(jnp.int32, (tm, tm), 1)
    before = jnp.where(col < row, 1.0, 0.0).astype(BF16)
    base = carry_scr[0:1, :] + jnp.dot(before, onehot.astype(BF16), preferred_element_type=F32)
    r1 = jnp.sum(jnp.where(lane == i1, base, 0.0), axis=-1, keepdims=True)
    r2 = jnp.sum(jnp.where(lane == i2, base, 0.0), axis=-1, keepdims=True)
    carry_scr[...] = carry_scr[...] + jnp.sum(onehot, axis=0, keepdims=True)
    cnt_ref[...] = carry_scr[...]

    fields = (i1 - MOE_GROUPS, i2 - MOE_GROUPS, r1, r2, g1, g2)
    rt = jnp.zeros((tm, lanes), F32)
    for k, f in enumerate(fields):
        rt = jnp.where(lane == k, f, rt)
    rt_ref[...] = rt


def _norm_mod_router(x, a, sh, wr, br, seq_starts):
    t, d = x.shape
    tm = _row_tile(t, 512, seq_starts)

    def seq_map(i):
        return (_seq_index(i * tm, seq_starts), 0, 0)

    return pl.pallas_call(
        _router_kernel,
        grid=(t // tm,),
        in_specs=[pl.BlockSpec((tm, d), lambda i: (i, 0)),
                  pl.BlockSpec((None, 1, d), seq_map),
                  pl.BlockSpec((None, 1, d), seq_map),
                  pl.BlockSpec((d, ROUTER_PAD), lambda i: (0, 0)),
                  pl.BlockSpec((1, ROUTER_PAD), lambda i: (0, 0))],
        out_specs=[pl.BlockSpec((tm, d), lambda i: (i, 0)),
                   pl.BlockSpec((tm, ROUTER_PAD), lambda i: (i, 0)),
                   pl.BlockSpec((8, ROUTER_PAD), lambda i: (0, 0))],
        out_shape=[jax.ShapeDtypeStruct((t, d), BF16), jax.ShapeDtypeStruct((t, ROUTER_PAD), F32),
                   jax.ShapeDtypeStruct((8, ROUTER_PAD), F32)],
        scratch_shapes=[pltpu.VMEM((8, ROUTER_PAD), F32)],
        compiler_params=_params("arbitrary"),
        name="norm_mod_router",
    )(x, a, sh, wr, br)


def _moe_ffn_kernel(ib_ref, ie_ref, inew_ref, nxe_ref, hasnx_ref, nu_ref,
                    xs_ref, w1_hbm, w3_hbm, w2_hbm, o_ref,
                    w1_stage, w3_stage, w2_stage, w1_scr, w3_scr, w2_scr, sems, *, layer, ff_chunks):
    del ib_ref
    s = pl.program_id(0)

    @pl.when(s >= nu_ref[0])
    def _():
        o_ref[...] = jnp.zeros_like(o_ref)

    def weight_copies(e):
        return (pltpu.make_async_copy(w1_hbm.at[layer, e], w1_stage, sems.at[0]),
                pltpu.make_async_copy(w3_hbm.at[layer, e], w3_stage, sems.at[1]),
                pltpu.make_async_copy(w2_hbm.at[layer, e], w2_stage, sems.at[2]))

    @pl.when(jnp.logical_and(s == 0, nu_ref[0] > 0))
    def _():
        for c in weight_copies(ie_ref[0]):
            c.start()

    fc = w1_scr.shape[1] // ff_chunks

    def ffn(cast_first):
        x = xs_ref[...]
        acc = None
        for k in range(ff_chunks):
            cols = slice(k * fc, (k + 1) * fc)
            if cast_first:
                w1_scr[:, cols] = w1_stage[:, cols].astype(BF16)
                w3_scr[:, cols] = w3_stage[:, cols].astype(BF16)
                w2_scr[cols, :] = w2_stage[cols, :].astype(BF16)
            h1 = jnp.dot(x, w1_scr[:, cols], preferred_element_type=F32)
            h3 = jnp.dot(x, w3_scr[:, cols], preferred_element_type=F32)
            hid = (_silu(h1) * h3).astype(BF16)
            part = jnp.dot(hid, w2_scr[cols, :], preferred_element_type=F32)
            acc = part if acc is None else acc + part
        o_ref[...] = acc.astype(o_ref.dtype)

    used = s < nu_ref[0]
    first = inew_ref[s] == 1

    @pl.when(jnp.logical_and(used, first))
    def _():
        for c in weight_copies(ie_ref[s]):
            c.wait()
        ffn(cast_first=True)

        @pl.when(hasnx_ref[s] == 1)
        def _():
            for c in weight_copies(nxe_ref[s]):
                c.start()

    @pl.when(jnp.logical_and(used, jnp.logical_not(first)))
    def _():
        ffn(cast_first=False)


def _moe_ffn(xs, tables, w1, w3, w2, layer):
    n_rows, d = xs.shape
    ff = w1.shape[3]
    blk = MOE_ROW_BLOCK

    grid_spec = pltpu.PrefetchScalarGridSpec(
        num_scalar_prefetch=len(tables),
        grid=(n_rows // blk,),
        in_specs=[pl.BlockSpec((blk, d), lambda s, ib, *_: (ib[s], 0)),
                  pl.BlockSpec(memory_space=pl.ANY),
                  pl.BlockSpec(memory_space=pl.ANY),
                  pl.BlockSpec(memory_space=pl.ANY)],
        out_specs=pl.BlockSpec((blk, d), lambda s, *_: (s, 0)),
        scratch_shapes=[pltpu.VMEM((d, ff), F32), pltpu.VMEM((d, ff), F32), pltpu.VMEM((ff, d), F32),
                        pltpu.VMEM((d, ff), BF16), pltpu.VMEM((d, ff), BF16), pltpu.VMEM((ff, d), BF16),
                        pltpu.SemaphoreType.DMA((3,))])
    return pl.pallas_call(
        functools.partial(_moe_ffn_kernel, layer=layer, ff_chunks=MOE_FF_CHUNKS),
        grid_spec=grid_spec,
        out_shape=jax.ShapeDtypeStruct((n_rows, d), BF16),
        compiler_params=_params("arbitrary", vmem_limit_bytes=MOE_VMEM_LIMIT_BYTES),
        name="moe_ffn",
    )(*tables, xs, w1, w3, w2)


def _combine_kernel(x_ref, o0_ref, o1_ref, gt_ref, g_ref, *rest, final, bounds):
    if final:
        fg_ref, o_refs = rest[0], rest[1:]
    else:
        o_refs = rest
    gt = gt_ref[...]
    g0 = 2 * MOE_TOP_K
    y = gt[:, g0:g0 + 1] * o0_ref[...].astype(F32) + gt[:, g0 + 1:g0 + 2] * o1_ref[...].astype(F32)
    x = x_ref[...] + g_ref[...] * y
    if final:
        ms = jnp.mean(x * x, axis=-1, keepdims=True)
        x = x * lax.rsqrt(ms + EPS) * fg_ref[...]

    def write(k):
        o_refs[k][...] = x

    _for_row_part(pl.program_id(0), bounds, write)


def _moe_combine(x, parts, gates, gate_mod, final_g, seq_starts, out_rows):
    t, d = x.shape
    tm = _row_tile(t, 512, seq_starts)
    final = final_g is not None
    out_shape = [jax.ShapeDtypeStruct((r, d), F32) for r in out_rows]
    out_specs, bounds = _row_part_specs(out_shape, tm, d, lambda i: i, lambda i: 0)
    in_specs = ([pl.BlockSpec((tm, d), lambda i: (i, 0))]
                + [pl.BlockSpec((tm, d), lambda i: (i, 0))] * len(parts)
                + [pl.BlockSpec((tm, ROUTER_PAD), lambda i: (i, 0)),
                   pl.BlockSpec((None, 1, d), lambda i: (_seq_index(i * tm, seq_starts), 0, 0))])
    args = [x, *parts, gates, gate_mod]
    if final:
        in_specs.append(pl.BlockSpec((1, d), lambda i: (0, 0)))
        args.append(final_g)
    return pl.pallas_call(
        functools.partial(_combine_kernel, final=final, bounds=bounds),
        grid=(t // tm,),
        in_specs=in_specs,
        out_specs=out_specs,
        out_shape=out_shape,
        compiler_params=_params("arbitrary"),
        name="moe_combine_residual",
    )(*args)


def _ret_kernel(*refs, reverse, nchunks, seq_starts, seq_ends):
    rc = RET_CHUNK
    if reverse:
        (cd_ref, q_ref, kt_ref, v_ref, inner_ref, cross_ref, kvd_ref, of_ref, gate_ref, o_ref, st_scr) = refs
    else:
        (cd_ref, q_ref, kt_ref, v_ref, inner_ref, cross_ref, kvd_ref, o_ref, st_scr) = refs
    step = pl.program_id(0)
    if reverse:
        fresh = _is_any((nchunks - step) * rc, seq_ends)
    else:
        fresh = _is_any(step * rc, seq_starts)

    @pl.when(fresh)
    def _():
        st_scr[...] = jnp.zeros_like(st_scr)

    dk, dv = RET_QK_DIM, RET_V_DIM
    for h in range(RET_HEADS):
        q = q_ref[:, h * dk:(h + 1) * dk]
        kt = kt_ref[h * dk:(h + 1) * dk, :]
        v = v_ref[:, h * dv:(h + 1) * dv]
        s = jnp.dot(q, kt, preferred_element_type=F32) * inner_ref[h]
        st = st_scr[h]
        o = jnp.dot(s.astype(BF16), v, preferred_element_type=F32)
        o = o + jnp.dot(q, st.astype(BF16), preferred_element_type=F32) * cross_ref[h]
        ktd = (kt.astype(F32) * kvd_ref[h]).astype(BF16)
        st_scr[h] = st * cd_ref[h] + jnp.dot(ktd, v, preferred_element_type=F32)
        if reverse:
            o = o + of_ref[:, h * dv:(h + 1) * dv].astype(F32)
            ms = jnp.mean(o * o, axis=-1, keepdims=True)
            o = o * lax.rsqrt(ms + EPS) * _silu(gate_ref[:, h * dv:(h + 1) * dv].astype(F32))
        o_ref[:, h * dv:(h + 1) * dv] = o.astype(o_ref.dtype)


def _retention_direction(krt, proj, vcol0, gcol0, log_decay, seq_starts, seq_ends, reverse, o_fwd=None):
    t = proj.shape[0]
    rc = _tile(min(s2 - s1 for s1, s2 in zip(seq_starts, seq_ends)), RET_CHUNK)
    assert rc == RET_CHUNK
    nchunks = t // rc
    pos = jnp.arange(rc, dtype=F32)
    diff = pos[:, None] - pos[None, :]
    lg = log_decay.astype(F32)
    if reverse:
        mask = diff < 0
        dist = -diff
        cross = jnp.exp((rc - pos)[None, :, None] * lg[:, None, None])
        kvd = jnp.exp(pos[None, None, :] * lg[:, None, None])
    else:
        mask = diff >= 0
        dist = diff
        cross = jnp.exp((pos + 1.0)[None, :, None] * lg[:, None, None])
        kvd = jnp.exp((rc - 1.0 - pos)[None, None, :] * lg[:, None, None])
    inner = jnp.where(mask[None], jnp.exp(jnp.where(mask, dist, 0.0)[None] * lg[:, None, None]), 0.0)
    cross = jnp.broadcast_to(cross, (RET_HEADS, rc, RET_V_DIM))
    kvd = jnp.broadcast_to(kvd, (RET_HEADS, RET_QK_DIM, rc))
    chunk_decay = jnp.exp(rc * lg)

    def rb(i):
        return nchunks - 1 - i if reverse else i

    vb = vcol0 // RET_V_WIDTH
    assert vcol0 % RET_V_WIDTH == 0 and gcol0 % RET_V_WIDTH == 0
    in_specs = [pl.BlockSpec(memory_space=pltpu.SMEM),
                pl.BlockSpec((rc, RET_QK_WIDTH), lambda i: (rb(i), 0)),
                pl.BlockSpec((RET_QK_WIDTH, rc), lambda i: (0, rb(i))),
                pl.BlockSpec((rc, RET_V_WIDTH), lambda i: (rb(i), vb)),
                pl.BlockSpec((RET_HEADS, rc, rc), lambda i: (0, 0, 0)),
                pl.BlockSpec((RET_HEADS, rc, RET_V_DIM), lambda i: (0, 0, 0)),
                pl.BlockSpec((RET_HEADS, RET_QK_DIM, rc), lambda i: (0, 0, 0))]
    args = [chunk_decay, proj, krt, proj, inner, cross, kvd]
    if reverse:
        gb = gcol0 // RET_V_WIDTH
        in_specs += [pl.BlockSpec((rc, RET_V_WIDTH), lambda i: (rb(i), 0)),
                     pl.BlockSpec((rc, RET_V_WIDTH), lambda i: (rb(i), gb))]
        args += [o_fwd, proj]
    return pl.pallas_call(
        functools.partial(_ret_kernel, reverse=reverse, nchunks=nchunks,
                          seq_starts=seq_starts, seq_ends=seq_ends),
        grid=(nchunks,),
        in_specs=in_specs,
        out_specs=pl.BlockSpec((rc, RET_V_WIDTH), lambda i: (rb(i), 0)),
        out_shape=jax.ShapeDtypeStruct((t, RET_V_WIDTH), BF16),
        scratch_shapes=[pltpu.VMEM((RET_HEADS, RET_QK_DIM, RET_V_DIM), F32)],
        compiler_params=_params("arbitrary"),
        name="retention_bwd_norm_gate" if reverse else "retention_fwd",
    )(*args)


def _ssd_diff_layer(x, a1, sh1, p, i, layer, lay):
    seq_starts, seq_ends, groups = lay
    w_in = p['ab_w_in'][i]
    o2 = SSD_WIDTH + SSD_XBC
    o3 = o2 + 2 * SSD_HEADS
    w_main = jnp.concatenate([w_in[:, :o2].astype(BF16), w_in[:, o3:].astype(BF16)], axis=1)
    w_dt = jnp.pad(w_in[:, o2:o3], ((0, 0), (0, DT_PAD - 2 * SSD_HEADS))).astype(BF16)
    qcol = o2
    kcol = qcol + DA_QK_WIDTH
    vcol = kcol + DA_QK_WIDTH
    lmax = max(g[2] for g in groups)
    tables = (*_rope_tables(lmax, DA_HEAD_DIM, DA_HEAD_DIM ** -0.5 * math.log2(math.e)),
              *_rope_tables(lmax, DA_HEAD_DIM, 1.0))
    tn = 1024
    assert qcol % tn == 0 and kcol % tn == 0 and vcol % tn == 0
    proj, dt = _norm_mod_matmul(x, a1, sh1, w_main, w_dt, seq_starts,
                                ((qcol // tn, kcol // tn), (kcol // tn, vcol // tn)), DA_HEAD_DIM, tables)

    xact = _conv_silu(proj, SSD_WIDTH, p['ssd_conv_w'][i], p['ssd_conv_b'][i], seq_starts, seq_ends)
    bt = xact[:, SSD_WIDTH:SSD_WIDTH + SSD_BC].T
    dtt = dt.T
    expand = (jnp.arange(SSD_WIDTH)[None, :] // SSD_HEAD_DIM == jnp.arange(SSD_HEADS)[:, None]).astype(F32)
    a_f = -jnp.exp(p['ssd_a_log_fwd'][i].astype(F32))
    a_b = -jnp.exp(p['ssd_a_log_bwd'][i].astype(F32))
    yf = _ssd_direction(xact, bt, dt, dtt, p['ssd_dt_bias_fwd'][i].astype(F32), a_f, expand,
                        seq_starts, seq_ends, reverse=False)
    dskip = jnp.repeat(p['ssd_d'][i].astype(F32), SSD_HEAD_DIM).reshape(1, SSD_WIDTH)
    y_ssd = _ssd_direction(xact, bt, dt, dtt, p['ssd_dt_bias_bwd'][i].astype(F32), a_b, expand,
                           seq_starts, seq_ends, reverse=True,
                           final=(proj, yf, dskip, p['ssd_norm_g'][i].astype(F32).reshape(1, SSD_WIDTH)))

    lam_init = 0.8 - 0.6 * math.exp(-0.3 * layer)
    lam = (jnp.exp(jnp.sum(p['da_lambda_q1'][i].astype(F32) * p['da_lambda_k1'][i].astype(F32)))
           - jnp.exp(jnp.sum(p['da_lambda_q2'][i].astype(F32) * p['da_lambda_k2'][i].astype(F32)))
           + lam_init).reshape(1)
    subln = p['da_subln_g'][i].astype(F32).reshape(1, 2 * DA_HEAD_DIM)
    o = _diff_attention(proj, qcol, kcol, vcol, lam, subln, 1.0 - lam_init, seq_starts, seq_ends)

    w_out = p['ab_w_out'][i].astype(BF16)
    return (y_ssd, 0, o, 0, w_out)


def _retention_layer(x, a1, sh1, p, j, lay):
    seq_starts, seq_ends, groups = lay
    w_in = p['ret_w_in'][j].astype(BF16)
    lmax = max(g[2] for g in groups)
    tables = (*_rope_tables(lmax, RET_QK_DIM, 1.0), *_rope_tables(lmax, RET_QK_DIM, RET_QK_DIM ** -0.5))
    tn = 1024
    kcol = RET_QK_WIDTH
    vcol = 2 * RET_QK_WIDTH
    gcol = vcol + RET_V_WIDTH
    proj = _norm_mod_matmul(x, a1, sh1, w_in, None, seq_starts,
                            ((0, kcol // tn), (kcol // tn, vcol // tn)), RET_QK_DIM, tables)
    krt = proj[:, kcol:vcol].T
    lg_f = -jnp.exp(p['ret_log_decay_fwd'][j].astype(F32))
    lg_b = -jnp.exp(p['ret_log_decay_bwd'][j].astype(F32))
    of = _retention_direction(krt, proj, vcol, gcol, lg_f, seq_starts, seq_ends, reverse=False)
    o = _retention_direction(krt, proj, vcol, gcol, lg_b, seq_starts, seq_ends, reverse=True, o_fwd=of)
    w_out = p['ret_w_out'][j].astype(BF16)
    return (o, 0, o, 1, w_out)


def _moe_layer(x_parts, mix, g1, a2, sh2, g2, p, layer, lay, final_g, out_rows):
    seq_starts, _, _ = lay
    t = sum(xp.shape[0] for xp in x_parts)
    d = x_parts[0].shape[1]
    wr = jnp.concatenate([p['moe_w_group'][layer], p['moe_w_expert'][layer]], axis=1).astype(F32)
    br = jnp.concatenate([p['moe_b_group'][layer], p['moe_b_expert'][layer]]).astype(F32)
    nr = wr.shape[1]
    wr = jnp.pad(wr, ((0, 0), (0, ROUTER_PAD - nr)))
    br = jnp.pad(br, (0, ROUTER_PAD - nr)).reshape(1, ROUTER_PAD)
    x = _outproj_residual(*mix, x_parts, g1, seq_starts)
    h, route, cnt = _norm_mod_router(x, a2, sh2, wr, br, seq_starts)

    blk = MOE_ROW_BLOCK
    n_assign = t * MOE_TOP_K
    e_flat = route[:, 0:MOE_TOP_K].astype(jnp.int32).reshape(n_assign)
    rank = route[:, MOE_TOP_K:2 * MOE_TOP_K].astype(jnp.int32).reshape(n_assign)
    counts = cnt[0, MOE_GROUPS:MOE_GROUPS + MOE_EXPERTS].astype(jnp.int32)
    padded = (counts + blk - 1) // blk * blk
    ends_pad = jnp.cumsum(padded)
    start_pad = ends_pad - padded
    dest = (start_pad[e_flat] + rank).astype(jnp.int32)
    n_rows = n_assign + MOE_EXPERTS * blk
    n_blocks = n_rows // blk
    tok_flat = jnp.repeat(jnp.arange(t, dtype=jnp.int32), MOE_TOP_K)
    row_tok = (jnp.arange(n_rows, dtype=jnp.int32) % t).at[dest].set(tok_flat, unique_indices=True)
    ends_blk = jnp.cumsum(padded // blk).astype(jnp.int32)
    n_used = ends_blk[-1]
    block = jnp.minimum(jnp.arange(n_blocks, dtype=jnp.int32), n_used - 1)
    block_expert = jnp.minimum(jnp.sum(ends_blk[None, :] <= block[:, None], axis=1),
                               MOE_EXPERTS - 1).astype(jnp.int32)

    pos = jnp.arange(n_blocks, dtype=jnp.int32)
    first = jnp.concatenate([jnp.ones((1,), jnp.int32),
                             (block_expert[1:] != block_expert[:-1]).astype(jnp.int32)])
    run_start = jnp.where(first == 1, pos, n_blocks)
    next_start = lax.cummin(jnp.concatenate([run_start[1:], jnp.full((1,), n_blocks, jnp.int32)]), reverse=True)
    has_next = (next_start < n_blocks).astype(jnp.int32)
    next_expert = block_expert[jnp.minimum(next_start, n_blocks - 1)]
    tables = (block, block_expert, first, next_expert, has_next, n_used.reshape(1))
    xs = jnp.take(h, row_tok, axis=0, mode='clip')
    out = _moe_ffn(xs, tables, p['moe_w1'], p['moe_w3'], p['moe_w2'], layer)
    dest2 = dest.reshape(t, MOE_TOP_K)
    parts = [jnp.take(out, dest2[:, k], axis=0, mode='clip', unique_indices=True) for k in range(MOE_TOP_K)]
    return tuple(_moe_combine(x, parts, route, g2, final_g, seq_starts, out_rows))


def kernel(x_prompt, x_sample, c_prompt, c_sample, ada_w, ada_b, norm_mix_g, norm_ffn_g, ab_w_in, ssd_conv_w, ssd_conv_b, ssd_a_log_fwd, ssd_a_log_bwd, ssd_dt_bias_fwd, ssd_dt_bias_bwd, ssd_d, ssd_norm_g, da_lambda_q1, da_lambda_k1, da_lambda_q2, da_lambda_k2, da_subln_g, ab_w_out, ret_w_in, ret_log_decay_fwd, ret_log_decay_bwd, ret_w_out, moe_w_group, moe_b_group, moe_w_expert, moe_b_expert, moe_w1, moe_w3, moe_w2, final_norm_g):
    p = dict(ab_w_in=ab_w_in, ssd_conv_w=ssd_conv_w, ssd_conv_b=ssd_conv_b,
             ssd_a_log_fwd=ssd_a_log_fwd, ssd_a_log_bwd=ssd_a_log_bwd,
             ssd_dt_bias_fwd=ssd_dt_bias_fwd, ssd_dt_bias_bwd=ssd_dt_bias_bwd, ssd_d=ssd_d,
             ssd_norm_g=ssd_norm_g, da_lambda_q1=da_lambda_q1, da_lambda_k1=da_lambda_k1,
             da_lambda_q2=da_lambda_q2, da_lambda_k2=da_lambda_k2, da_subln_g=da_subln_g,
             ab_w_out=ab_w_out, ret_w_in=ret_w_in, ret_log_decay_fwd=ret_log_decay_fwd,
             ret_log_decay_bwd=ret_log_decay_bwd, ret_w_out=ret_w_out, moe_w_group=moe_w_group,
             moe_b_group=moe_b_group, moe_w_expert=moe_w_expert, moe_b_expert=moe_b_expert,
             moe_w1=moe_w1, moe_w3=moe_w3, moe_w2=moe_w2)
    bp, lp, d = x_prompt.shape
    bs, ls, _ = x_sample.shape
    depth = ada_w.shape[0]
    groups = ((0, bp, lp), (bp * lp, bs, ls))
    seq_starts = tuple(r0 + b * l for (r0, nb, l) in groups for b in range(nb))
    seq_ends = tuple(r0 + (b + 1) * l for (r0, nb, l) in groups for b in range(nb))
    lay = (seq_starts, seq_ends, groups)
    nseq = len(seq_starts)
    group_rows = (bp * lp, bs * ls)
    x = (x_prompt.reshape(bp * lp, d), x_sample.reshape(bs * ls, d))
    c = jnp.concatenate([c_prompt, c_sample], axis=0).astype(F32)
    c_pad = jnp.pad(c, ((0, -nseq % 8), (0, 0)))
    mod = _ada_modulation(c_pad, ada_w.astype(F32), ada_b.astype(F32))[:, :nseq]

    for layer in range(depth):
        sh1, sc1, g1, sh2, sc2, g2 = [m.reshape(nseq, 1, d) for m in jnp.split(mod[layer], 6, axis=-1)]
        a1 = norm_mix_g[layer].astype(F32)[None, None, :] * (1.0 + sc1)
        a2 = norm_ffn_g[layer].astype(F32)[None, None, :] * (1.0 + sc2)
        if layer % 2 == 0:
            mix = _ssd_diff_layer(x, a1, sh1, p, layer // 2, layer, lay)
        else:
            mix = _retention_layer(x, a1, sh1, p, layer // 2, lay)
        last = layer == depth - 1
        final_g = final_norm_g.astype(F32).reshape(1, d) if last else None
        x = _moe_layer(x, mix, g1, a2, sh2, g2, p, layer, lay, final_g,
                       group_rows if last else (sum(group_rows),))
    return (x[0].reshape(bp, lp, d), x[1].reshape(bs, ls, d))
```
